```python
import math
import jax
import jax.numpy as jnp
from jax import lax
import numpy as np

D_MODEL = 1024
BATCH = 16
SEQ = 2048
DEPTH = 2

GRID_W = 64
CTX_LEN = 256
HEAD_DIM = 64
MIX_WIDTH = D_MODEL
GROUP_WIDTH = MIX_WIDTH // 2
FFN_HIDDEN = 4 * D_MODEL
ROPE_BASE = 10000.0
LN_EPS = 1e-5

RWKV_HEADS = GROUP_WIDTH // HEAD_DIM
RWKV_DECAY_LORA = 64
RWKV_ICLR_LORA = 64
RWKV_GATE_LORA = 128
RWKV_GN_EPS = 64e-5
RWKV_COLS = 3 * GROUP_WIDTH + RWKV_DECAY_LORA + RWKV_ICLR_LORA + RWKV_GATE_LORA
RWKV_SPLITS = [GROUP_WIDTH, 2 * GROUP_WIDTH, 3 * GROUP_WIDTH,
               3 * GROUP_WIDTH + RWKV_DECAY_LORA, 3 * GROUP_WIDTH + RWKV_DECAY_LORA + RWKV_ICLR_LORA]

DIFF_HEADS = GROUP_WIDTH // (2 * HEAD_DIM)
DIFF_COLS = 3 * GROUP_WIDTH
ATTN_BLOCK = 128

SSD_HEADS = GROUP_WIDTH // HEAD_DIM
SSD_GROUPS = 2
SSD_STATE = 128
SSD_CONV = 5
SSD_CHUNK = 128
SSD_CONV_DIM = GROUP_WIDTH + 2 * SSD_GROUPS * SSD_STATE
SSD_COLS = GROUP_WIDTH + SSD_CONV_DIM + 2 * SSD_HEADS

SWA_HEADS = GROUP_WIDTH // HEAD_DIM
SWA_KV_HEADS = 2
SWA_WINDOW = 128
SWA_BLOCK = 128
SWA_COLS = GROUP_WIDTH + 2 * SWA_KV_HEADS * HEAD_DIM

AB_COLS = RWKV_COLS + DIFF_COLS
CD_COLS = SSD_COLS + SWA_COLS
N_AB = (DEPTH + 1) // 2
N_CD = DEPTH // 2
DEEPNORM_ALPHA = (2 * DEPTH) ** 0.25
DEEPNORM_BETA = (8 * DEPTH) ** -0.25

kernel_name = 'hybrid_rwkv7_diffattn_ssd_swa_prefix_dit'


def layer_norm(x, g, b, eps=LN_EPS):
    xf = x.astype(jnp.float32)
    mu = jnp.mean(xf, -1, keepdims=True)
    var = jnp.mean(jnp.square(xf - mu), -1, keepdims=True)
    return ((xf - mu) * lax.rsqrt(var + eps)).astype(x.dtype) * g + b


def rms_norm(x, g, eps=LN_EPS):
    xf = x.astype(jnp.float32)
    return (xf * lax.rsqrt(jnp.mean(xf * xf, -1, keepdims=True) + eps)).astype(x.dtype) * g


def modulate(h, shift, scale):
    return h * (1.0 + scale) + shift


def rope_1d(x, pos):
    half = x.shape[-1] // 2
    inv = ROPE_BASE ** (-jnp.arange(half, dtype=jnp.float32) / half)
    ang = pos.astype(jnp.float32)[:, None] * inv
    shape = (pos.shape[0],) + (1,) * (x.ndim - 3) + (half,)
    cos = jnp.cos(ang).reshape(shape).astype(x.dtype)
    sin = jnp.sin(ang).reshape(shape).astype(x.dtype)
    x1, x2 = x[..., :half], x[..., half:]
    return jnp.concatenate([x1 * cos - x2 * sin, x2 * cos + x1 * sin], -1)


def rope_2d(x, rows, cols):
    h = x.shape[-1] // 2
    return jnp.concatenate([rope_1d(x[..., :h], rows), rope_1d(x[..., h:], cols)], -1)


def bi_token_shift(p, mu):
    prev = jnp.pad(p, ((0, 0), (1, 0), (0, 0)))[:, :-1]
    nxt = jnp.pad(p, ((0, 0), (0, 1), (0, 0)))[:, 1:]
    return p + mu * (0.5 * (prev + nxt) - p)


def dw_conv_centred(x, w, b):
    k, ch = w.shape
    out = lax.conv_general_dilated(x, w[:, None, :], window_strides=(1,), padding=[(k // 2, k // 2)],
                                   dimension_numbers=('NWC', 'WIO', 'NWC'), feature_group_count=ch)
    return out + b


def segsum(a):
    t = a.shape[-1]
    cs = jnp.cumsum(a, -1)
    diff = cs[..., :, None] - cs[..., None, :]
    return jnp.where(jnp.tril(jnp.ones((t, t), bool)), diff, -jnp.inf)


def rwkv_scan(r, decay, k, v, a_vec, b_vec, s0, reverse, with_y):
    xs = tuple(jnp.moveaxis(t, 1, 0) for t in (r, decay, k, v, a_vec, b_vec))

    def step(s, inp):
        r_t, w_t, k_t, v_t, a_t, b_t = inp
        sa = jnp.einsum('bhvk,bhk->bhv', s, a_t)
        s = s * w_t[:, :, None, :] + sa[..., None] * b_t[:, :, None, :] + v_t[..., None] * k_t[:, :, None, :]
        y = jnp.einsum('bhvk,bhk->bhv', s, r_t) if with_y else None
        return s, y

    s_fin, ys = lax.scan(step, s0, xs, reverse=reverse)
    return (jnp.moveaxis(ys, 0, 1) if with_y else None), s_fin


def rwkv7_mixer(p_ctx, p_lat, mu, w0, w2, a0, a2, g2, k_k, k_a, r_k, gn_g, gn_b, need_ctx):
    def prep(p):
        b_, l_ = p.shape[:2]
        heads = lambda t: t.reshape(b_, l_, RWKV_HEADS, HEAD_DIM)
        p = bi_token_shift(p, mu)
        r, k, v, xw, xa, xg = jnp.split(p, RWKV_SPLITS, axis=-1)
        kk = heads(k * k_k)
        kk = kk * lax.rsqrt(jnp.maximum(jnp.sum(jnp.square(kk.astype(jnp.float32)), -1, keepdims=True),
                                        1e-24)).astype(kk.dtype)
        tw = jnp.tanh(xw)
        dirs = []
        for d in range(2):
            w_log = -jax.nn.softplus(-(w0[d] + tw @ w2[d])) - 0.5
            decay = jnp.exp(-jnp.exp(w_log.astype(jnp.float32))).astype(p.dtype)
            a = jax.nn.sigmoid(a0[d] + xa @ a2[d])
            dirs.append((heads(decay), heads(k * (1.0 + (a - 1.0) * k_a)), -kk, kk * heads(a)))
        g = jax.nn.sigmoid(xg) @ g2
        bonus = jnp.sum(heads(r * k) * r_k, -1, keepdims=True) * heads(v)
        return heads(r), heads(v), g, bonus, dirs

    def finish(y, bonus, g):
        y = layer_norm(y, gn_g.reshape(RWKV_HEADS, HEAD_DIM), gn_b.reshape(RWKV_HEADS, HEAD_DIM), RWKV_GN_EPS)
        return (y + bonus).reshape(g.shape) * g

    rc, vc, gc, bc, dc = prep(p_ctx)
    rl, vl, gl, bl, dl = prep(p_lat)
    s0 = jnp.zeros((p_lat.shape[0], RWKV_HEADS, HEAD_DIM, HEAD_DIM), p_lat.dtype)
    y_ctx, y_lat = 0.0, 0.0
    for d, rev in enumerate((False, True)):
        dec_c, k_c, av_c, bv_c = dc[d]
        dec_l, k_l, av_l, bv_l = dl[d]
        yc, s_ctx = rwkv_scan(rc, dec_c, k_c, vc, av_c, bv_c, s0, rev, need_ctx)
        yl, _ = rwkv_scan(rl, dec_l, k_l, vl, av_l, bv_l, s_ctx, rev, True)
        y_lat = y_lat + yl
        if need_ctx:
            y_ctx = y_ctx + yc
    o_ctx = finish(y_ctx, bc, gc) if need_ctx else None
    return o_ctx, finish(y_lat, bl, gl)


def diff_attend(q, k, v, lam):
    s = jnp.einsum('bqhmd,bkhmd->bhmqk', q, k).astype(jnp.float32) * (HEAD_DIM ** -0.5)
    p = jax.nn.softmax(s, axis=-1)
    w = (p[:, :, 0] - lam * p[:, :, 1]).astype(v.dtype)
    return jnp.einsum('bhqk,bkhe->bqhe', w, v)


def diff_attn_mixer(p_ctx, p_lat, rows, cols, lq1, lk1, lq2, lk2, subln_g, lam_init, need_ctx):
    def split(p):
        b_, l_ = p.shape[:2]
        q, k, v = jnp.split(p, 3, axis=-1)
        return (q.reshape(b_, l_, DIFF_HEADS, 2, HEAD_DIM), k.reshape(b_, l_, DIFF_HEADS, 2, HEAD_DIM),
                v.reshape(b_, l_, DIFF_HEADS, 2 * HEAD_DIM))

    qc, kc, vc = split(p_ctx)
    ql, kl, vl = split(p_lat)
    ql = rope_2d(ql, rows, cols)
    kl = rope_2d(kl, rows, cols)
    lam = (jnp.exp(jnp.sum(lq1 * lk1).astype(jnp.float32)) - jnp.exp(jnp.sum(lq2 * lk2).astype(jnp.float32))
           + lam_init)
    k_all = jnp.concatenate([kc, kl], 1)
    v_all = jnp.concatenate([vc, vl], 1)
    b_, l_ = p_lat.shape[:2]
    nb = l_ // ATTN_BLOCK
    qb = jnp.moveaxis(ql.reshape(b_, nb, ATTN_BLOCK, DIFF_HEADS, 2, HEAD_DIM), 1, 0)
    ob = lax.map(lambda qq: diff_attend(qq, k_all, v_all, lam), qb)
    o_lat = jnp.moveaxis(ob, 0, 1).reshape(b_, l_, DIFF_HEADS, 2 * HEAD_DIM)

    def finish(o):
        return (rms_norm(o, subln_g) * (1.0 - lam_init)).reshape(o.shape[:2] + (GROUP_WIDTH,))

    o_ctx = finish(diff_attend(qc, kc, vc, lam)) if need_ctx else None
    return o_ctx, finish(o_lat)


def ssd_chunked(xdt, adt, bm, cm, init, with_y):
    b, l, h, p = xdt.shape
    g, n = bm.shape[2], bm.shape[3]
    nc, q, e = l // SSD_CHUNK, SSD_CHUNK, h // g
    dt = xdt.dtype
    x = xdt.reshape(b, nc, q, g, e, p)
    a = adt.astype(jnp.float32).reshape(b, nc, q, g, e).transpose(0, 1, 3, 4, 2)
    bc = bm.reshape(b, nc, q, g, n)
    cc = cm.reshape(b, nc, q, g, n)
    a_cs = jnp.cumsum(a, -1)
    decay_to_end = jnp.exp(a_cs[..., -1:] - a_cs).astype(dt)
    states = jnp.einsum('bcsgn,bcges,bcsgep->bcgepn', bc, decay_to_end, x)
    states = jnp.concatenate([init.reshape(b, g, e, p, n)[:, None], states], 1)
    a_chunks = jnp.pad(jnp.moveaxis(a_cs[..., -1], 1, -1), ((0, 0), (0, 0), (0, 0), (1, 0)))
    chunk_decay = jnp.exp(segsum(a_chunks)).astype(dt)
    new_states = jnp.einsum('bgezc,bcgepn->bzgepn', chunk_decay, states)
    final = new_states[:, -1].reshape(b, h, p, n)
    if not with_y:
        return None, final
    prev = new_states[:, :-1]
    lmat = jnp.exp(segsum(a)).astype(dt)
    cb = jnp.einsum('bclgn,bcsgn->bcgls', cc, bc)
    y_diag = jnp.einsum('bcgls,bcgels,bcsgep->bclgep', cb, lmat, x)
    y_off = jnp.einsum('bclgn,bcgepn,bcgel->bclgep', cc, prev, jnp.exp(a_cs).astype(dt))
    return (y_diag + y_off).reshape(b, l, h, p), final


def flip_if(t, rev):
    return jnp.flip(t, 1) if rev else t


def ssd_mixer(p_ctx, p_lat, conv_w, conv_b, dt_bias, a_log, d_skip, norm_g, need_ctx):
    a_neg = -jnp.exp(a_log.astype(jnp.float32))

    def prep(p):
        b_, l_ = p.shape[:2]
        z, xbc, dt = jnp.split(p, [GROUP_WIDTH, GROUP_WIDTH + SSD_CONV_DIM], axis=-1)
        xbc = jax.nn.silu(dw_conv_centred(xbc, conv_w, conv_b))
        xs, bm, cm = jnp.split(xbc, [GROUP_WIDTH, GROUP_WIDTH + SSD_GROUPS * SSD_STATE], axis=-1)
        dt = jax.nn.softplus(dt.reshape(b_, l_, 2, SSD_HEADS) + dt_bias)
        return (z, xs.reshape(b_, l_, SSD_HEADS, HEAD_DIM), bm.reshape(b_, l_, SSD_GROUPS, SSD_STATE),
                cm.reshape(b_, l_, SSD_GROUPS, SSD_STATE), dt)

    zc, xc, bc, cc, dtc = prep(p_ctx)
    zl, xl, bl, cl, dtl = prep(p_lat)
    h0 = jnp.zeros((p_lat.shape[0], SSD_HEADS, HEAD_DIM, SSD_STATE), p_lat.dtype)
    y_ctx, y_lat = 0.0, 0.0
    for d, rev in enumerate((False, True)):
        yc, h_ctx = ssd_chunked(flip_if(xc * dtc[:, :, d, :, None], rev), flip_if(dtc[:, :, d] * a_neg[d], rev),
                                flip_if(bc, rev), flip_if(cc, rev), h0, need_ctx)
        yl, _ = ssd_chunked(flip_if(xl * dtl[:, :, d, :, None], rev), flip_if(dtl[:, :, d] * a_neg[d], rev),
                            flip_if(bl, rev), flip_if(cl, rev), h_ctx, True)
        y_lat = y_lat + flip_if(yl, rev)
        if need_ctx:
            y_ctx = y_ctx + flip_if(yc, rev)

    def finish(y, xs, z):
        y = (y + d_skip[:, None] * xs).reshape(z.shape) * jax.nn.silu(z)
        y = rms_norm(y.reshape(z.shape[:2] + (SSD_GROUPS, GROUP_WIDTH // SSD_GROUPS)),
                     norm_g.reshape(SSD_GROUPS, GROUP_WIDTH // SSD_GROUPS))
        return y.reshape(z.shape)

    o_ctx = finish(y_ctx, xc, zc) if need_ctx else None
    return o_ctx, finish(y_lat, xl, zl)


def sink_attend(q, k, v, mask, sink):
    s = jnp.einsum('bqhgd,bkhd->bhgqk', q, k).astype(jnp.float32) * (HEAD_DIM ** -0.5)
    s = jnp.where(mask, s, -jnp.inf)
    sink_col = jnp.broadcast_to(sink.astype(jnp.float32)[None, :, :, None, None], s.shape[:-1] + (1,))
    p = jax.nn.softmax(jnp.concatenate([sink_col, s], -1), axis=-1)[..., 1:]
    return jnp.einsum('bhgqk,bkhd->bqhgd', p.astype(v.dtype), v)


def swa_mixer(p_ctx, p_lat, rows, cols, sink, need_ctx):
    group = SWA_HEADS // SWA_KV_HEADS
    sink = sink.reshape(SWA_KV_HEADS, group)
    kv_w = SWA_KV_HEADS * HEAD_DIM

    def split(p):
        b_, l_ = p.shape[:2]
        q, k, v = jnp.split(p, [GROUP_WIDTH, GROUP_WIDTH + kv_w], axis=-1)
        return (q.reshape(b_, l_, SWA_KV_HEADS, group, HEAD_DIM), k.reshape(b_, l_, SWA_KV_HEADS, HEAD_DIM),
                v.reshape(b_, l_, SWA_KV_HEADS, HEAD_DIM))

    qc, kc, vc = split(p_ctx)
    ql, kl, vl = split(p_lat)
    ql = rope_2d(ql, rows, cols)
    kl = rope_2d(kl, rows, cols)
    b_, l_ = p_lat.shape[:2]
    n_ctx = kc.shape[1]
    nb = l_ // SWA_BLOCK
    span = SWA_BLOCK + 2 * SWA_WINDOW
    pad = ((0, 0), (SWA_WINDOW, SWA_WINDOW), (0, 0), (0, 0))
    kp, vp = jnp.pad(kl, pad), jnp.pad(vl, pad)
    ctx_mask = jnp.ones((SWA_BLOCK, n_ctx), bool)

    def block(args):
        i, qq = args
        start = i * SWA_BLOCK
        kw = lax.dynamic_slice_in_dim(kp, start, span, axis=1)
        vw = lax.dynamic_slice_in_dim(vp, start, span, axis=1)
        qpos = start + jnp.arange(SWA_BLOCK)
        kpos = start - SWA_WINDOW + jnp.arange(span)
        win = (jnp.abs(qpos[:, None] - kpos[None, :]) <= SWA_WINDOW) & ((kpos >= 0) & (kpos < l_))[None, :]
        return sink_attend(qq, jnp.concatenate([kc, kw], 1), jnp.concatenate([vc, vw], 1),
                           jnp.concatenate([ctx_mask, win], 1), sink)

    qb = jnp.moveaxis(ql.reshape(b_, nb, SWA_BLOCK, SWA_KV_HEADS, group, HEAD_DIM), 1, 0)
    ob = lax.map(block, (jnp.arange(nb), qb))
    o_lat = jnp.moveaxis(ob, 0, 1).reshape(b_, l_, GROUP_WIDTH)
    if need_ctx:
        o_ctx = sink_attend(qc, kc, vc, jnp.ones((n_ctx, n_ctx), bool), sink).reshape(b_, n_ctx, GROUP_WIDTH)
    else:
        o_ctx = None
    return o_ctx, o_lat


def ffn_sublayer(h, shift, scale, gate, w1, w2, ln_g, ln_b):
    f = jnp.square(jax.nn.relu(modulate(h, shift, scale) @ w1)) @ w2
    return layer_norm(DEEPNORM_ALPHA * h + gate * f, ln_g, ln_b)


def setup_inputs(seed: int = 0) -> dict:
    key = jax.random.key(seed)
    ks = iter(jax.random.split(key, 48))

    def nrm(shape, scale=1.0):
        return jax.random.normal(next(ks), shape, jnp.float32) * scale

    def unif(shape, lo, hi):
        return jax.random.uniform(next(ks), shape, jnp.float32, lo, hi)

    D, GW = D_MODEL, GROUP_WIDTH
    dt0 = jnp.exp(unif((N_CD, 2, SSD_HEADS), math.log(1e-3), math.log(1e-1)))
    return {
        'x': nrm((BATCH, SEQ, D)),
        'c': nrm((BATCH, D)),
        'ctx': nrm((BATCH, CTX_LEN, D)),
        'c_ctx': nrm((D,)),
        'mod_w': nrm((DEPTH, D, 6 * D), 0.5 * D ** -0.5),
        'mod_b': nrm((DEPTH, 6 * D), 0.02),
        'ln_mix_g': 1.0 + nrm((DEPTH, D), 0.02),
        'ln_mix_b': nrm((DEPTH, D), 0.02),
        'ln_ffn_g': 1.0 + nrm((DEPTH, D), 0.02),
        'ln_ffn_b': nrm((DEPTH, D), 0.02),
        'ffn_w1': nrm((DEPTH, D, FFN_HIDDEN), D ** -0.5),
        'ffn_w2': nrm((DEPTH, FFN_HIDDEN, D), DEEPNORM_BETA * FFN_HIDDEN ** -0.5),
        'w_out': nrm((DEPTH, MIX_WIDTH, D), DEEPNORM_BETA * MIX_WIDTH ** -0.5),
        'ab_w_in': nrm((N_AB, D, AB_COLS), D ** -0.5),
        'rwkv_mu': unif((N_AB, RWKV_COLS), 0.0, 1.0),
        'rwkv_w0': unif((N_AB, 2, GW), -5.0, 1.0),
        'rwkv_w2': nrm((N_AB, 2, RWKV_DECAY_LORA, GW), 0.1 * RWKV_DECAY_LORA ** -0.5),
        'rwkv_a0': nrm((N_AB, 2, GW), 0.5),
        'rwkv_a2': nrm((N_AB, 2, RWKV_ICLR_LORA, GW), 0.1 * RWKV_ICLR_LORA ** -0.5),
        'rwkv_g2': nrm((N_AB, RWKV_GATE_LORA, GW), RWKV_GATE_LORA ** -0.5),
        'rwkv_k_k': 0.85 + nrm((N_AB, GW), 0.05),
        'rwkv_k_a': 1.0 + nrm((N_AB, GW), 0.05),
        'rwkv_r_k': nrm((N_AB, RWKV_HEADS, HEAD_DIM), 0.1),
        'rwkv_gn_g': 1.0 + nrm((N_AB, GW), 0.02),
        'rwkv_gn_b': nrm((N_AB, GW), 0.02),
        'diff_lq1': nrm((N_AB, HEAD_DIM), 0.1),
        'diff_lk1': nrm((N_AB, HEAD_DIM), 0.1),
        'diff_lq2': nrm((N_AB, HEAD_DIM), 0.1),
        'diff_lk2': nrm((N_AB, HEAD_DIM), 0.1),
        'diff_subln_g': 1.0 + nrm((N_AB, 2 * HEAD_DIM), 0.02),
        'cd_w_in': nrm((N_CD, D, CD_COLS), D ** -0.5),
        'ssd_conv_w': nrm((N_CD, SSD_CONV, SSD_CONV_DIM), SSD_CONV ** -0.5),
        'ssd_conv_b': nrm((N_CD, SSD_CONV_DIM), 0.02),
        'ssd_dt_bias': dt0 + jnp.log(-jnp.expm1(-dt0)),
        'ssd_a_log': jnp.log(unif((N_CD, 2, SSD_HEADS), 1.0, 16.0)),
        'ssd_d': 1.0 + nrm((N_CD, SSD_HEADS), 0.1),
        'ssd_norm_g': 1.0 + nrm((N_CD, GW), 0.02),
        'swa_sink': nrm((N_CD, SWA_HEADS), 0.5),
    }


def reference(x, c, ctx, c_ctx, mod_w, mod_b, ln_mix_g, ln_mix_b, ln_ffn_g, ln_ffn_b, ffn_w1, ffn_w2, w_out,
              ab_w_in, rwkv_mu, rwkv_w0, rwkv_w2, rwkv_a0, rwkv_a2, rwkv_g2, rwkv_k_k, rwkv_k_a, rwkv_r_k,
              rwkv_gn_g, rwkv_gn_b, diff_lq1, diff_lk1, diff_lq2, diff_lk2, diff_subln_g,
              cd_w_in, ssd_conv_w, ssd_conv_b, ssd_dt_bias, ssd_a_log, ssd_d, ssd_norm_g, swa_sink):
    n_lat = x.shape[1]
    n_rows = n_lat // GRID_W
    rows = jnp.repeat(jnp.arange(n_rows, dtype=jnp.int32), GRID_W)
    cols = jnp.tile(jnp.arange(GRID_W, dtype=jnp.int32), n_rows)
    silu_c = jax.nn.silu(c)
    silu_cc = jax.nn.silu(c_ctx)
    h_lat, h_ctx = x, ctx
    for i in range(DEPTH):
        need_ctx = i < DEPTH - 1
        m_lat = jnp.split((silu_c @ mod_w[i] + mod_b[i])[:, None, :], 6, axis=-1)
        m_ctx = jnp.split((silu_cc @ mod_w[i] + mod_b[i])[None, None, :], 6, axis=-1)
        j = i // 2
        w_in = ab_w_in[j] if i % 2 == 0 else cd_w_in[j]
        p_lat = modulate(h_lat, m_lat[0], m_lat[1]) @ w_in
        p_ctx = modulate(h_ctx, m_ctx[0], m_ctx[1]) @ w_in
        if i % 2 == 0:
            o1_ctx, o1_lat = rwkv7_mixer(p_ctx[..., :RWKV_COLS], p_lat[..., :RWKV_COLS], rwkv_mu[j], rwkv_w0[j],
                                         rwkv_w2[j], rwkv_a0[j], rwkv_a2[j], rwkv_g2[j], rwkv_k_k[j], rwkv_k_a[j],
                                         rwkv_r_k[j], rwkv_gn_g[j], rwkv_gn_b[j], need_ctx)
            o2_ctx, o2_lat = diff_attn_mixer(p_ctx[..., RWKV_COLS:], p_lat[..., RWKV_COLS:], rows, cols,
                                             diff_lq1[j], diff_lk1[j], diff_lq2[j], diff_lk2[j], diff_subln_g[j],
                                             0.8 - 0.6 * math.exp(-0.3 * i), need_ctx)
        else:
            o1_ctx, o1_lat = ssd_mixer(p_ctx[..., :SSD_COLS], p_lat[..., :SSD_COLS], ssd_conv_w[j], ssd_conv_b[j],
                                       ssd_dt_bias[j], ssd_a_log[j], ssd_d[j], ssd_norm_g[j], need_ctx)
            o2_ctx, o2_lat = swa_mixer(p_ctx[..., SSD_COLS:], p_lat[..., SSD_COLS:], rows, cols, swa_sink[j], need_ctx)
        o_lat = jnp.concatenate([o1_lat, o2_lat], -1) @ w_out[i]
        h_lat = layer_norm(DEEPNORM_ALPHA * h_lat + m_lat[2] * o_lat, ln_mix_g[i], ln_mix_b[i])
        h_lat = ffn_sublayer(h_lat, m_lat[3], m_lat[4], m_lat[5], ffn_w1[i], ffn_w2[i], ln_ffn_g[i], ln_ffn_b[i])
        if need_ctx:
            o_ctx = jnp.concatenate([o1_ctx, o2_ctx], -1) @ w_out[i]
            h_ctx = layer_norm(DEEPNORM_ALPHA * h_ctx + m_ctx[2] * o_ctx, ln_mix_g[i], ln_mix_b[i])
            h_ctx = ffn_sublayer(h_ctx, m_ctx[3], m_ctx[4], m_ctx[5], ffn_w1[i], ffn_w2[i], ln_ffn_g[i], ln_ffn_b[i])
    return h_lat
```

```python
import functools
import math

import jax
import jax.numpy as jnp
from jax import lax
from jax.experimental import pallas as pl
from jax.experimental.pallas import tpu as pltpu

F32 = jnp.float32
BF16 = jnp.bfloat16

HEAD_DIM = 64
GRID_W = 64
ROPE_BASE = 10000.0
LN_EPS = 1e-5
RWKV_GN_EPS = 64e-5
RWKV_LORA_W = 64
RWKV_LORA_A = 64
RWKV_LORA_G = 128
DIFF_HEADS = 4
SSD_HEADS = 8
SSD_GROUPS = 2
SSD_STATE = 128
SSD_CONV = 5
SWA_HEADS = 8
SWA_KV_HEADS = 2
SWA_WINDOW = 128

ROW_TILE = 256
HALO = 8
RWKV_CHUNK = 64
SSD_CHUNK = 128
SWA_TILE = 128
LANES = 128
VMEM_LIMIT = 56 * 1024 * 1024


def _cparams(n_axes):
    return pltpu.CompilerParams(dimension_semantics=("arbitrary",) * n_axes, vmem_limit_bytes=VMEM_LIMIT)


def _dot(a, b):
    return jnp.dot(a, b, preferred_element_type=F32)


def _dot_nt(a, b):
    return lax.dot_general(a, b, (((1,), (1,)), ((), ())), preferred_element_type=F32)


def _split3(a):
    hi = a.astype(BF16)
    r1 = a - hi.astype(F32)
    mid = r1.astype(BF16)
    lo = (r1 - mid.astype(F32)).astype(BF16)
    return hi, mid, lo


def _dot_f32_by_exact(a, b_exact):
    hi, mid, lo = _split3(a)
    return _dot(hi, b_exact) + _dot(mid, b_exact) + _dot(lo, b_exact)


def _dot_exact_by_f32(a_exact, b):
    hi, mid, lo = _split3(b)
    return _dot(a_exact, hi) + _dot(a_exact, mid) + _dot(a_exact, lo)


def _dot_hilo(a, b):
    ah = a.astype(BF16)
    al = (a - ah.astype(F32)).astype(BF16)
    bh = b.astype(BF16)
    bl = (b - bh.astype(F32)).astype(BF16)
    return _dot(ah, bh) + _dot(ah, bl) + _dot(al, bh)


def _sigmoid(x):
    return 1.0 / (1.0 + jnp.exp(-x))


def _softplus(x):
    return jnp.maximum(x, 0.0) + jnp.log(1.0 + jnp.exp(-jnp.abs(x)))


def _layer_norm(y, g, b, eps):
    mu = jnp.mean(y, -1, keepdims=True)
    yc = y - mu
    var = jnp.mean(yc * yc, -1, keepdims=True)
    return yc * lax.rsqrt(var + eps) * g + b


def _mod_kernel(c_ref, w_ref, b_ref, o_ref):
    c = c_ref[...]
    o_ref[0] = _dot_hilo(c * _sigmoid(c), w_ref[0]) + b_ref[0]


def _mod_call(c_pad, mod_w, mod_b):
    depth, d, n = mod_w.shape
    rows = c_pad.shape[0]
    tn = 1536
    return pl.pallas_call(
        _mod_kernel,
        grid=(depth, n // tn),
        in_specs=[pl.BlockSpec((rows, d), lambda i, j: (0, 0)),
                  pl.BlockSpec((1, d, tn), lambda i, j: (i, 0, j)),
                  pl.BlockSpec((1, 1, tn), lambda i, j: (i, 0, j))],
        out_specs=pl.BlockSpec((1, rows, tn), lambda i, j: (i, 0, j)),
        out_shape=jax.ShapeDtypeStruct((depth, rows, n), F32),
        compiler_params=_cparams(2),
        name="adaln_mod",
    )(c_pad, mod_w, mod_b.reshape(depth, 1, n))


def _rope(x, cos, sin):
    lane = lax.broadcasted_iota(jnp.int32, (x.shape[0], LANES), 1)
    first = (lane % 32) < 16
    parts = []
    for g in range(0, x.shape[1], LANES):
        xg = x[:, g:g + LANES]
        parts.append(jnp.where(first, pltpu.roll(xg, LANES - 16, 1), pltpu.roll(xg, 16, 1)))
    sw = parts[0] if len(parts) == 1 else jnp.concatenate(parts, 1)
    return x * cos + sw * sin


def _inproj_kernel(x_ref, sh_ref, sc_ref, w_ref, cos_ref, sin_ref, *o_refs, plan):
    xm = (x_ref[0] * (1.0 + sc_ref[0, 0, 0]) + sh_ref[0, 0, 0]).astype(BF16)
    for o_ref, (c0, width, rope, mult) in zip(o_refs, plan):
        for j in range(0, width, 256):
            cw = min(256, width - j)
            acc = _dot(xm, w_ref[:, c0 + j:c0 + j + cw])
            if rope:
                acc = _rope(acc, cos_ref[:, :cw], sin_ref[:, :cw])
            if mult != 1.0:
                acc = acc * mult
            o_ref[0, :, j:j + cw] = acc.astype(o_ref.dtype)


def _inproj_call(h, modsel, w_bf16, cos_t, sin_t, plan, out_dtypes, n_ctx_tiles, name):
    b, l, d = h.shape
    n_tiles = l // ROW_TILE
    seg = lambda bi, t: (bi, (t >= n_ctx_tiles).astype(jnp.int32), 0, 0, 0)
    seg_scale = lambda bi, t: (bi, (t >= n_ctx_tiles).astype(jnp.int32), 1, 0, 0)
    out_shape = [jax.ShapeDtypeStruct((b, l, width), dt) for (_, width, _, _), dt in zip(plan, out_dtypes)]
    out_specs = [pl.BlockSpec((1, ROW_TILE, width), lambda bi, t: (bi, t, 0)) for (_, width, _, _) in plan]
    return pl.pallas_call(
        functools.partial(_inproj_kernel, plan=plan),
        grid=(b, n_tiles),
        in_specs=[pl.BlockSpec((1, ROW_TILE, d), lambda bi, t: (bi, t, 0)),
                  pl.BlockSpec((1, 1, 1, 1, d), seg),
                  pl.BlockSpec((1, 1, 1, 1, d), seg_scale),
                  pl.BlockSpec(w_bf16.shape, lambda bi, t: (0, 0), pipeline_mode=pl.Buffered(1)),
                  pl.BlockSpec((ROW_TILE, cos_t.shape[1]), lambda bi, t: (t, 0)),
                  pl.BlockSpec((ROW_TILE, sin_t.shape[1]), lambda bi, t: (t, 0))],
        out_specs=out_specs,
        out_shape=out_shape,
        compiler_params=_cparams(2),
        name=name,
    )(h, modsel, modsel, w_bf16, cos_t, sin_t)


def _rwkv_prep_kernel(p_ref, hp_ref, hn_ref, mu_ref, bias_ref, lora_ref, g2_ref, kk_ref, rk_ref, bd_ref,
                      r_o, v_o, kk_o, k_o, g_o, bonus_o, lw_o, a_o, *, n_ctx_tiles, n_tiles):
    t = pl.program_id(1)
    gw = r_o.shape[-1]
    p = p_ref[0]
    rows = p.shape[0]
    first = jnp.logical_or(t == 0, t == n_ctx_tiles)
    last = jnp.logical_or(t == n_ctx_tiles - 1, t == n_tiles - 1)
    prev_row = jnp.where(first, 0.0, hp_ref[0, HALO - 1:HALO, :])
    next_row = jnp.where(last, 0.0, hn_ref[0, 0:1, :])
    rowi = lax.broadcasted_iota(jnp.int32, p.shape, 0)
    prev = jnp.where(rowi == 0, prev_row, pltpu.roll(p, 1, 0))
    nxt = jnp.where(rowi == rows - 1, next_row, pltpu.roll(p, rows - 1, 0))
    ps = p + mu_ref[...] * (0.5 * (prev + nxt) - p)

    r = ps[:, :gw]
    k = ps[:, gw:2 * gw]
    v = ps[:, 2 * gw:3 * gw]
    slab = ps[:, 3 * gw:3 * gw + LANES]
    xg = ps[:, 3 * gw + LANES:3 * gw + 2 * LANES]
    lane = lax.broadcasted_iota(jnp.int32, slab.shape, 1)
    lora_in = jnp.where(lane < RWKV_LORA_W, jnp.tanh(slab), slab)
    pre = _dot_hilo(lora_in, lora_ref[...]) + bias_ref[...]
    bd = bd_ref[...]
    for d in range(2):
        w_log = -_softplus(-pre[:, d * gw:(d + 1) * gw]) - 0.5
        lw_o[0, d] = -jnp.exp(w_log)
        a_o[0, d] = _sigmoid(pre[:, (2 + d) * gw:(3 + d) * gw])
    g_o[0] = _dot_hilo(_sigmoid(xg), g2_ref[...])
    kk = k * kk_ref[...]
    ss = _dot_f32_by_exact(kk * kk, bd)
    kk_o[0] = kk * lax.rsqrt(jnp.maximum(ss, 1e-24))
    bonus_o[0] = _dot_f32_by_exact(r * k * rk_ref[...], bd) * v
    r_o[0] = r
    k_o[0] = k
    v_o[0] = v


def _rwkv_prep_call(p, mu, bias, lora_w, g2, k_k, r_k, bd, n_ctx_tiles):
    b, l, cols = p.shape
    gw = k_k.shape[-1]
    n_tiles = l // ROW_TILE
    hb = ROW_TILE // HALO
    n_hb = l // HALO
    full = lambda a: pl.BlockSpec(a.shape, lambda bi, t: (0,) * a.ndim)
    tok = pl.BlockSpec((1, ROW_TILE, gw), lambda bi, t: (bi, t, 0))
    tok2 = pl.BlockSpec((1, 2, ROW_TILE, gw), lambda bi, t: (bi, 0, t, 0))
    s1 = jax.ShapeDtypeStruct((b, l, gw), F32)
    s2 = jax.ShapeDtypeStruct((b, 2, l, gw), F32)
    return pl.pallas_call(
        functools.partial(_rwkv_prep_kernel, n_ctx_tiles=n_ctx_tiles, n_tiles=n_tiles),
        grid=(b, n_tiles),
        in_specs=[pl.BlockSpec((1, ROW_TILE, cols), lambda bi, t: (bi, t, 0)),
                  pl.BlockSpec((1, HALO, cols), lambda bi, t: (bi, jnp.maximum(t * hb - 1, 0), 0)),
                  pl.BlockSpec((1, HALO, cols), lambda bi, t: (bi, jnp.minimum((t + 1) * hb, n_hb - 1), 0)),
                  full(mu), full(bias), full(lora_w), full(g2), full(k_k), full(r_k), full(bd)],
        out_specs=[tok, tok, tok, tok, tok, tok, tok2, tok2],
        out_shape=[s1, s1, s1, s1, s1, s1, s2, s2],
        compiler_params=_cparams(2),
        name="rwkv_prep",
    )(p, p, p, mu, bias, lora_w, g2, k_k, r_k, bd)


def _rwkv_scan_kernel(r_ref, v_ref, kk_ref, k_ref, lw_ref, a_ref, ka_ref, y_ref, s_ref):
    d = pl.program_id(1)
    i = pl.program_id(2)
    c = r_ref.shape[1]
    n_pairs = r_ref.shape[2] // LANES
    c2 = 2 * c

    @pl.when(i == 0)
    def _():
        s_ref[...] = jnp.zeros_like(s_ref)

    sgn = 1 - 2 * d
    row = lax.broadcasted_iota(jnp.int32, (c, c), 0)
    col = lax.broadcasted_iota(jnp.int32, (c, c), 1)
    tri = ((row - col) * sgn >= 0).astype(BF16)
    lw = lw_ref[0, 0]
    cum = _dot_exact_by_f32(tri, lw)
    total = jnp.where(d == 0, cum[c - 1:c, :], cum[0:1, :])
    p_in = jnp.exp(cum)
    p_inv = jnp.exp(-cum)
    p_ex = jnp.exp(cum - lw)
    a = a_ref[0, 0]
    kk = kk_ref[0]
    a_t = -kk * p_ex
    b_t = kk * a * p_inv
    k_t = k_ref[0] * (1.0 + (a - 1.0) * ka_ref[...]) * p_inv
    r_t = r_ref[0] * p_in
    v_all = v_ref[0]
    p_end = jnp.exp(total)

    row2 = lax.broadcasted_iota(jnp.int32, (c2, c2), 0)
    col2 = lax.broadcasted_iota(jnp.int32, (c2, c2), 1)
    same = (row2 // c) == (col2 // c)
    dt2 = (row2 - col2) * sgn
    strict = jnp.logical_and(same, dt2 > 0)
    incl = jnp.logical_and(same, dt2 >= 0)
    eye = (row2 == col2).astype(F32)
    lane = lax.broadcasted_iota(jnp.int32, (c, LANES), 1)
    low = lane < HEAD_DIM

    def stack(x):
        return jnp.concatenate([jnp.where(low, x, 0.0), jnp.where(low, 0.0, x)], 0)

    for pr in range(n_pairs):
        sl = slice(pr * LANES, (pr + 1) * LANES)
        ah, bh, kh, rh, vh = (stack(x[:, sl]) for x in (a_t, b_t, k_t, r_t, v_all))
        ar = jnp.concatenate([ah, rh], 0).astype(BF16)
        bk = jnp.concatenate([bh, kh], 0)
        bk16 = bk.astype(BF16)
        gram = _dot_nt(ar, bk16)
        l_ab = jnp.where(strict, gram[:c2, :c2], 0.0)
        l_ak = jnp.where(strict, gram[:c2, c2:], 0.0)
        l_rb = jnp.where(incl, gram[c2:, :c2], 0.0)
        l_rk = jnp.where(incl, gram[c2:, c2:], 0.0)
        inv = eye + l_ab
        pw = l_ab
        steps = max(1, int(math.ceil(math.log2(c))) - 1)
        for _ in range(steps):
            pw16 = pw.astype(BF16)
            pw = _dot(pw16, pw16)
            inv = inv + _dot(inv.astype(BF16), pw.astype(BF16))
        s0 = s_ref[pr]
        ars = _dot_nt(ar, s0.astype(BF16))
        vh16 = vh.astype(BF16)
        rhs = ars[:c2] + _dot(l_ak.astype(BF16), vh16)
        u = _dot(inv.astype(BF16), rhs.astype(BF16))
        uv = jnp.concatenate([u, vh], 0)
        y = ars[c2:] + _dot(jnp.concatenate([l_rb, l_rk], 1).astype(BF16), uv.astype(BF16))
        y_ref[0, 0, :, sl] = y[:c] + y[c:]
        s_ref[pr] = (s0 + _dot(uv.T.astype(BF16), bk16)) * p_end[:, sl]


def _rwkv_scan_call(r, v, kk, k, lw, a, k_a, n_ctx):
    b, l, gw = r.shape
    c = RWKV_CHUNK
    nc = l // c
    ncc = n_ctx // c

    def chunk(d, i):
        rev = jnp.where(i < ncc, ncc - 1 - i, nc - 1 - (i - ncc))
        return jnp.where(d == 0, i, rev)

    tok = pl.BlockSpec((1, c, gw), lambda bi, d, i: (bi, chunk(d, i), 0))
    tok2 = pl.BlockSpec((1, 1, c, gw), lambda bi, d, i: (bi, d, chunk(d, i), 0))
    return pl.pallas_call(
        _rwkv_scan_kernel,
        grid=(b, 2, nc),
        in_specs=[tok, tok, tok, tok, tok2, tok2, pl.BlockSpec(k_a.shape, lambda bi, d, i: (0, 0))],
        out_specs=tok2,
        out_shape=jax.ShapeDtypeStruct((b, 2, l, gw), F32),
        scratch_shapes=[pltpu.VMEM((gw // LANES, LANES, LANES), F32)],
        compiler_params=_cparams(3),
        name="rwkv_scan",
    )(r, v, kk, k, lw, a, k_a)


def _rwkv_finish_kernel(y_ref, bonus_ref, g_ref, gg_ref, gb_ref, bd_ref, o_ref):
    y = y_ref[0, 0] + y_ref[0, 1]
    bd = bd_ref[...]
    inv_n = 1.0 / HEAD_DIM
    mu = _dot_f32_by_exact(y, bd) * inv_n
    yc = y - mu
    var = _dot_f32_by_exact(yc * yc, bd) * inv_n
    yn = yc * lax.rsqrt(var + RWKV_GN_EPS) * gg_ref[...] + gb_ref[...]
    o_ref[0] = (yn + bonus_ref[0]) * g_ref[0]


def _rwkv_finish_call(y, bonus, g, gn_g, gn_b, bd):
    b, _, l, gw = y.shape
    tok = pl.BlockSpec((1, ROW_TILE, gw), lambda bi, t: (bi, t, 0))
    full = lambda a: pl.BlockSpec(a.shape, lambda bi, t: (0,) * a.ndim)
    return pl.pallas_call(
        _rwkv_finish_kernel,
        grid=(b, l // ROW_TILE),
        in_specs=[pl.BlockSpec((1, 2, ROW_TILE, gw), lambda bi, t: (bi, 0, t, 0)), tok, tok,
                  full(gn_g), full(gn_b), full(bd)],
        out_specs=tok,
        out_shape=jax.ShapeDtypeStruct((b, l, gw), F32),
        compiler_params=_cparams(2),
        name="rwkv_finish",
    )(y, bonus, g, gn_g, gn_b, bd)


def _diff_attn_kernel(lam_ref, q_ref, k_ref, v_ref, g_ref, o_ref, *, n_ctx, n_ctx_tiles, out_scale):
    t = pl.program_id(1)
    lam = lam_ref[0]
    vd = 2 * HEAD_DIM

    def run(nk):
        for h in range(DIFF_HEADS):
            w = None
            for m in range(2):
                c0 = (2 * h + m) * HEAD_DIM
                s = _dot_nt(q_ref[0, :, c0:c0 + HEAD_DIM], k_ref[0, :nk, c0:c0 + HEAD_DIM])
                e = jnp.exp(s - jnp.max(s, -1, keepdims=True))
                inv = 1.0 / jnp.sum(e, -1, keepdims=True)
                w = e * inv if m == 0 else w - e * (lam * inv)
            o = _dot(w.astype(BF16), v_ref[0, :nk, h * vd:(h + 1) * vd])
            o = o * lax.rsqrt(jnp.mean(o * o, -1, keepdims=True) + LN_EPS) * (g_ref[...] * out_scale)
            o_ref[0, :, h * vd:(h + 1) * vd] = o

    @pl.when(t < n_ctx_tiles)
    def _():
        run(n_ctx)

    @pl.when(t >= n_ctx_tiles)
    def _():
        run(k_ref.shape[1])


def _diff_attn_call(lam, q, k, v, subln_g, n_ctx, lam_init):
    b, l, w = q.shape
    kv = pl.BlockSpec((1, l, w), lambda bi, t: (bi, 0, 0))
    tok = pl.BlockSpec((1, ROW_TILE, w), lambda bi, t: (bi, t, 0))
    return pl.pallas_call(
        functools.partial(_diff_attn_kernel, n_ctx=n_ctx, n_ctx_tiles=n_ctx // ROW_TILE, out_scale=1.0 - lam_init),
        grid=(b, l // ROW_TILE),
        in_specs=[pl.BlockSpec(memory_space=pltpu.SMEM), tok, kv, kv,
                  pl.BlockSpec(subln_g.shape, lambda bi, t: (0, 0))],
        out_specs=tok,
        out_shape=jax.ShapeDtypeStruct((b, l, w), F32),
        compiler_params=_cparams(2),
        name="diff_attn",
    )(lam, q, k, v, subln_g)


def _outproj_ffn_kernel(o1_ref, o2_ref, h_ref, gm_ref, sh_ref, sc_ref, gf_ref, wo_ref, w1_ref, w2_ref,
                        lmg_ref, lmb_ref, lfg_ref, lfb_ref, out_ref, *, alpha):
    gw = o1_ref.shape[-1]
    o = _dot(o1_ref[0].astype(BF16), wo_ref[:gw, :]) + _dot(o2_ref[0].astype(BF16), wo_ref[gw:, :])
    h1 = _layer_norm(alpha * h_ref[0] + gm_ref[0, 0, 0] * o, lmg_ref[...], lmb_ref[...], LN_EPS)
    xm = (h1 * (1.0 + sc_ref[0, 0, 0]) + sh_ref[0, 0, 0]).astype(BF16)
    hidden = w1_ref.shape[1]
    step = 512
    acc = None
    for j in range(0, hidden, step):
        hj = jnp.maximum(_dot(xm, w1_ref[:, j:j + step]), 0.0)
        part = _dot((hj * hj).astype(BF16), w2_ref[j:j + step, :])
        acc = part if acc is None else acc + part
    out_ref[0] = _layer_norm(alpha * h1 + gf_ref[0, 0, 0] * acc, lfg_ref[...], lfb_ref[...], LN_EPS)


def _outproj_ffn_call(o1, o2, h, modsel, wo, w1, w2, lmg, lmb, lfg, lfb, *, o_off, h_off, n_out_tiles,
                      n_ctx_tiles, alpha, name):
    b, _, gw = o1.shape
    d = h.shape[-1]
    seg = lambda col: (lambda bi, t: (bi, ((t + h_off) >= n_ctx_tiles).astype(jnp.int32), col, 0, 0))
    modspec = lambda col: pl.BlockSpec((1, 1, 1, 1, d), seg(col))
    full = lambda a: pl.BlockSpec(a.shape, lambda bi, t: (0,) * a.ndim, pipeline_mode=pl.Buffered(1))
    otok = pl.BlockSpec((1, ROW_TILE, gw), lambda bi, t: (bi, t + o_off, 0))
    return pl.pallas_call(
        functools.partial(_outproj_ffn_kernel, alpha=alpha),
        grid=(b, n_out_tiles),
        in_specs=[otok, otok, pl.BlockSpec((1, ROW_TILE, d), lambda bi, t: (bi, t + h_off, 0)),
                  modspec(2), modspec(3), modspec(4), modspec(5),
                  full(wo), full(w1), full(w2), full(lmg), full(lmb), full(lfg), full(lfb)],
        out_specs=pl.BlockSpec((1, ROW_TILE, d), lambda bi, t: (bi, t, 0)),
        out_shape=jax.ShapeDtypeStruct((b, n_out_tiles * ROW_TILE, d), F32),
        compiler_params=_cparams(2),
        name=name,
    )(o1, o2, h, modsel, modsel, modsel, modsel, wo, w1, w2, lmg, lmb, lfg, lfb)


def _ssd_prep_kernel(x_ref, hp_ref, hn_ref, dt_ref, cw_ref, cb_ref, dtb_ref, an_ref, xc_o, dt_o, adt_o,
                     *, n_ctx_tiles, n_tiles):
    t = pl.program_id(1)
    x = x_ref[0]
    rows = x.shape[0]
    first = jnp.logical_or(t == 0, t == n_ctx_tiles)
    last = jnp.logical_or(t == n_ctx_tiles - 1, t == n_tiles - 1)
    ext = jnp.concatenate([jnp.where(first, 0.0, hp_ref[0]), x, jnp.where(last, 0.0, hn_ref[0])], 0)
    n_ext = rows + 2 * HALO
    acc = cb_ref[...] + cw_ref[SSD_CONV // 2:SSD_CONV // 2 + 1, :] * x
    for j in range(SSD_CONV):
        s = j - SSD_CONV // 2
        if s == 0:
            continue
        shifted = pltpu.roll(ext, (n_ext - s) % n_ext, 0)[HALO:HALO + rows]
        acc = acc + cw_ref[j:j + 1, :] * shifted
    xc_o[0] = acc * _sigmoid(acc)
    dt = _softplus(dt_ref[0] + dtb_ref[...])
    dt_o[0] = dt
    adt_o[0] = dt * an_ref[...]


def _ssd_prep_call(xbc, dt, conv_w, conv_b, dt_bias, a_neg, n_ctx_tiles):
    b, l, cw = xbc.shape
    n_tiles = l // ROW_TILE
    hb = ROW_TILE // HALO
    n_hb = l // HALO
    full = lambda a: pl.BlockSpec(a.shape, lambda bi, t: (0,) * a.ndim)
    tokd = pl.BlockSpec((1, ROW_TILE, LANES), lambda bi, t: (bi, t, 0))
    sd = jax.ShapeDtypeStruct((b, l, LANES), F32)
    return pl.pallas_call(
        functools.partial(_ssd_prep_kernel, n_ctx_tiles=n_ctx_tiles, n_tiles=n_tiles),
        grid=(b, n_tiles),
        in_specs=[pl.BlockSpec((1, ROW_TILE, cw), lambda bi, t: (bi, t, 0)),
                  pl.BlockSpec((1, HALO, cw), lambda bi, t: (bi, jnp.maximum(t * hb - 1, 0), 0)),
                  pl.BlockSpec((1, HALO, cw), lambda bi, t: (bi, jnp.minimum((t + 1) * hb, n_hb - 1), 0)),
                  tokd, full(conv_w), full(conv_b), full(dt_bias), full(a_neg)],
        out_specs=[pl.BlockSpec((1, ROW_TILE, cw), lambda bi, t: (bi, t, 0)), tokd, tokd],
        out_shape=[jax.ShapeDtypeStruct((b, l, cw), F32), sd, sd],
        compiler_params=_cparams(2),
        name="ssd_prep",
    )(xbc, xbc, xbc, dt, conv_w, conv_b, dt_bias, a_neg)


def _ssd_scan_kernel(xc_ref, dt_ref, adt_ref, adtt_ref, e_ref, y_ref, st_ref):
    d = pl.program_id(1)
    i = pl.program_id(2)
    q = xc_ref.shape[1]
    gw = y_ref.shape[-1]
    n = SSD_STATE
    hpg = SSD_HEADS // SSD_GROUPS
    gcols = hpg * HEAD_DIM

    @pl.when(i == 0)
    def _():
        st_ref[...] = jnp.zeros_like(st_ref)

    sgn = 1 - 2 * d
    row = lax.broadcasted_iota(jnp.int32, (q, q), 0)
    col = lax.broadcasted_iota(jnp.int32, (q, q), 1)
    incl = (row - col) * sgn >= 0
    tri = incl.astype(BF16)
    tri_t = ((col - row) * sgn >= 0).astype(BF16)
    cs = _dot_exact_by_f32(tri, adt_ref[0])
    cs_t = _dot_f32_by_exact(adtt_ref[0], tri_t)
    e = e_ref[d]
    cse = _dot_f32_by_exact(cs, e)
    dte = _dot_f32_by_exact(dt_ref[0], e)
    tot = jnp.where(d == 0, cse[q - 1:q, :], cse[0:1, :])
    xc = xc_ref[0]
    xdt = xc[:, :gw] * dte
    e_cs = jnp.exp(cse)
    x_end = (xdt * jnp.exp(tot - cse)).astype(BF16)
    e_tot = jnp.exp(tot)
    xdt16 = xdt.astype(BF16)
    for g in range(SSD_GROUPS):
        bg = xc[:, gw + g * n:gw + (g + 1) * n]
        cg = xc[:, gw + SSD_GROUPS * n + g * n:gw + SSD_GROUPS * n + (g + 1) * n].astype(BF16)
        cb = _dot_nt(cg, bg.astype(BF16))
        st = st_ref[g]
        gsl = slice(g * gcols, (g + 1) * gcols)
        y_off = _dot(cg, st.astype(BF16)) * e_cs[:, gsl]
        ys = []
        for hh in range(hpg):
            h = g * hpg + hh
            colv = cse[:, h * HEAD_DIM:h * HEAD_DIM + 1]
            rowv = jnp.where(d == 0, cs_t[h:h + 1, :], cs_t[SSD_HEADS + h:SSD_HEADS + h + 1, :])
            lm = jnp.where(incl, jnp.exp(colv - rowv), 0.0)
            ys.append(_dot((cb * lm).astype(BF16), xdt16[:, h * HEAD_DIM:(h + 1) * HEAD_DIM]))
        y_ref[0, 0, :, gsl] = jnp.concatenate(ys, 1) + y_off
        st_ref[g] = st * e_tot[:, gsl] + _dot(bg.T.astype(BF16), x_end[:, gsl])


def _ssd_scan_call(xc, dt, adt, adt_t, e_sel, n_ctx):
    b, l, cw = xc.shape
    gw = e_sel.shape[-1]
    q = SSD_CHUNK
    nc = l // q
    ncc = n_ctx // q

    def chunk(d, i):
        rev = jnp.where(i < ncc, ncc - 1 - i, nc - 1 - (i - ncc))
        return jnp.where(d == 0, i, rev)

    tokd = pl.BlockSpec((1, q, LANES), lambda bi, d, i: (bi, chunk(d, i), 0))
    return pl.pallas_call(
        _ssd_scan_kernel,
        grid=(b, 2, nc),
        in_specs=[pl.BlockSpec((1, q, cw), lambda bi, d, i: (bi, chunk(d, i), 0)), tokd, tokd,
                  pl.BlockSpec((1, 2 * SSD_HEADS, q), lambda bi, d, i: (bi, 0, chunk(d, i))),
                  pl.BlockSpec(e_sel.shape, lambda bi, d, i: (0, 0, 0))],
        out_specs=pl.BlockSpec((1, 1, q, gw), lambda bi, d, i: (bi, d, chunk(d, i), 0)),
        out_shape=jax.ShapeDtypeStruct((b, 2, l, gw), F32),
        scratch_shapes=[pltpu.VMEM((SSD_GROUPS, SSD_STATE, gw // SSD_GROUPS), F32)],
        compiler_params=_cparams(3),
        name="ssd_scan",
    )(xc, dt, adt, adt_t, e_sel)


def _ssd_finish_kernel(y_ref, xc_ref, z_ref, dsk_ref, ng_ref, o_ref):
    gw = o_ref.shape[-1]
    z = z_ref[0]
    y = (y_ref[0, 0] + y_ref[0, 1] + dsk_ref[...] * xc_ref[0, :, :gw]) * (z * _sigmoid(z))
    gwid = gw // SSD_GROUPS
    for g in range(SSD_GROUPS):
        yg = y[:, g * gwid:(g + 1) * gwid]
        o_ref[0, :, g * gwid:(g + 1) * gwid] = (yg * lax.rsqrt(jnp.mean(yg * yg, -1, keepdims=True) + LN_EPS)
                                                * ng_ref[:, g * gwid:(g + 1) * gwid])


def _ssd_finish_call(y, xc, z, d_exp, norm_g, n_ctx_tiles, n_lat_tiles):
    b, _, l, gw = y.shape
    cw = xc.shape[-1]
    off = n_ctx_tiles
    full = lambda a: pl.BlockSpec(a.shape, lambda bi, t: (0,) * a.ndim)
    return pl.pallas_call(
        _ssd_finish_kernel,
        grid=(b, n_lat_tiles),
        in_specs=[pl.BlockSpec((1, 2, ROW_TILE, gw), lambda bi, t: (bi, 0, t + off, 0)),
                  pl.BlockSpec((1, ROW_TILE, cw), lambda bi, t: (bi, t + off, 0)),
                  pl.BlockSpec((1, ROW_TILE, gw), lambda bi, t: (bi, t + off, 0)),
                  full(d_exp), full(norm_g)],
        out_specs=pl.BlockSpec((1, ROW_TILE, gw), lambda bi, t: (bi, t, 0)),
        out_shape=jax.ShapeDtypeStruct((b, n_lat_tiles * ROW_TILE, gw), F32),
        compiler_params=_cparams(2),
        name="ssd_finish",
    )(y, xc, z, d_exp, norm_g)


def _swa_kernel(sink_ref, q_ref, k_ref, v_ref, o_ref, *, n_ctx, n_lat):
    i = pl.program_id(1)
    tq = q_ref.shape[1]
    span = tq + 2 * SWA_WINDOW
    group = SWA_HEADS // SWA_KV_HEADS
    start = jnp.clip(i * tq - SWA_WINDOW, 0, n_lat - span)
    w0 = pl.multiple_of(n_ctx + start, LANES)
    qpos = i * tq + lax.broadcasted_iota(jnp.int32, (tq, span), 0)
    kpos = start + lax.broadcasted_iota(jnp.int32, (tq, span), 1)
    win = jnp.abs(qpos - kpos) <= SWA_WINDOW
    win = jnp.concatenate([win.astype(F32)] * group, 0) > 0.5
    for kvh in range(SWA_KV_HEADS):
        hs = slice(kvh * HEAD_DIM, (kvh + 1) * HEAD_DIM)
        qs = jnp.concatenate([q_ref[0, :, (kvh * group + g) * HEAD_DIM:(kvh * group + g + 1) * HEAD_DIM]
                              for g in range(group)], 0)
        s_c = _dot_nt(qs, k_ref[0, :n_ctx, hs])
        s_w = jnp.where(win, _dot_nt(qs, k_ref[0, pl.ds(w0, span), hs]), -1e30)
        sink = jnp.concatenate([jnp.full((tq, 1), sink_ref[kvh * group + g], F32) for g in range(group)], 0)
        m = jnp.maximum(jnp.maximum(jnp.max(s_c, -1, keepdims=True), jnp.max(s_w, -1, keepdims=True)), sink)
        e_c = jnp.exp(s_c - m)
        e_w = jnp.exp(s_w - m)
        den = jnp.exp(sink - m) + jnp.sum(e_c, -1, keepdims=True) + jnp.sum(e_w, -1, keepdims=True)
        o = (_dot(e_c.astype(BF16), v_ref[0, :n_ctx, hs]) + _dot(e_w.astype(BF16), v_ref[0, pl.ds(w0, span), hs])) / den
        for g in range(group):
            c0 = (kvh * group + g) * HEAD_DIM
            o_ref[0, :, c0:c0 + HEAD_DIM] = o[g * tq:(g + 1) * tq]


def _swa_call(sink, q, k, v, n_ctx):
    b, l, qw = q.shape
    kw = k.shape[-1]
    n_lat = l - n_ctx
    off = n_ctx // SWA_TILE
    kv = pl.BlockSpec((1, l, kw), lambda bi, t: (bi, 0, 0))
    return pl.pallas_call(
        functools.partial(_swa_kernel, n_ctx=n_ctx, n_lat=n_lat),
        grid=(b, n_lat // SWA_TILE),
        in_specs=[pl.BlockSpec(memory_space=pltpu.SMEM),
                  pl.BlockSpec((1, SWA_TILE, qw), lambda bi, t: (bi, t + off, 0)), kv, kv],
        out_specs=pl.BlockSpec((1, SWA_TILE, qw), lambda bi, t: (bi, t, 0)),
        out_shape=jax.ShapeDtypeStruct((b, n_lat, qw), F32),
        compiler_params=_cparams(2),
        name="swa",
    )(sink, q, k, v)


def _rope_tables(n_ctx, n_lat, width):
    half = HEAD_DIM // 4
    inv = ROPE_BASE ** (-jnp.arange(half, dtype=F32) / half)
    pos = jnp.arange(n_lat, dtype=jnp.int32)
    rows = (pos // GRID_W).astype(F32)[:, None] * inv
    cols = (pos % GRID_W).astype(F32)[:, None] * inv
    cos = jnp.concatenate([jnp.cos(rows), jnp.cos(rows), jnp.cos(cols), jnp.cos(cols)], -1)
    sin = jnp.concatenate([-jnp.sin(rows), jnp.sin(rows), -jnp.sin(cols), jnp.sin(cols)], -1)
    cos = jnp.concatenate([jnp.ones((n_ctx, HEAD_DIM), F32), cos], 0)
    sin = jnp.concatenate([jnp.zeros((n_ctx, HEAD_DIM), F32), sin], 0)
    reps = width // HEAD_DIM
    return jnp.tile(cos, (1, reps)), jnp.tile(sin, (1, reps))


def kernel(x, c, ctx, c_ctx, mod_w, mod_b, ln_mix_g, ln_mix_b, ln_ffn_g, ln_ffn_b, ffn_w1, ffn_w2, w_out, ab_w_in, rwkv_mu, rwkv_w0, rwkv_w2, rwkv_a0, rwkv_a2, rwkv_g2, rwkv_k_k, rwkv_k_a, rwkv_r_k, rwkv_gn_g, rwkv_gn_b, diff_lq1, diff_lk1, diff_lq2, diff_lk2, diff_subln_g, cd_w_in, ssd_conv_w, ssd_conv_b, ssd_dt_bias, ssd_a_log, ssd_d, ssd_norm_g, swa_sink):
    b, n_lat, d = x.shape
    n_ctx = ctx.shape[1]
    depth = mod_w.shape[0]
    assert depth == 2 and n_ctx % ROW_TILE == 0 and n_lat % ROW_TILE == 0 and n_lat % GRID_W == 0
    assert n_lat >= SWA_TILE + 2 * SWA_WINDOW
    gw = d // 2
    n_ctx_tiles = n_ctx // ROW_TILE
    n_lat_tiles = n_lat // ROW_TILE
    alpha = (2 * depth) ** 0.25
    q_scale = HEAD_DIM ** -0.5

    rows_pad = -(-(b + 1) // 8) * 8
    c_pad = jnp.zeros((rows_pad, d), F32).at[:b].set(c).at[b].set(c_ctx)
    m = _mod_call(c_pad, mod_w, mod_b)

    def modsel(i):
        return jnp.stack([jnp.broadcast_to(m[i, b], (b, 6 * d)), m[i, :b]], 1).reshape(b, 2, 6, 1, d)

    cos_t, sin_t = _rope_tables(n_ctx, n_lat, 256)
    h0 = jnp.concatenate([ctx, x], 1)

    rwkv_cols = 3 * gw + RWKV_LORA_W + RWKV_LORA_A + RWKV_LORA_G
    plan0 = ((0, rwkv_cols, False, 1.0), (rwkv_cols, gw, True, q_scale), (rwkv_cols + gw, gw, True, 1.0),
             (rwkv_cols + 2 * gw, gw, False, 1.0))
    p_rwkv, dq, dk, dv = _inproj_call(h0, modsel(0), ab_w_in[0].astype(BF16), cos_t, sin_t, plan0,
                                      (F32, BF16, BF16, BF16), n_ctx_tiles, "inproj_ab")

    zw = jnp.zeros((RWKV_LORA_W, gw), F32)
    lora_w = jnp.concatenate([jnp.concatenate([rwkv_w2[0, 0], rwkv_w2[0, 1], zw, zw], 1),
                              jnp.concatenate([zw, zw, rwkv_a2[0, 0], rwkv_a2[0, 1]], 1)], 0)
    lora_b = jnp.concatenate([rwkv_w0[0, 0], rwkv_w0[0, 1], rwkv_a0[0, 0], rwkv_a0[0, 1]])[None, :]
    head_id = jnp.arange(gw, dtype=jnp.int32) // HEAD_DIM
    bd = (head_id[:, None] == head_id[None, :]).astype(BF16)
    r, v, kk, k, g, bonus, lw, a = _rwkv_prep_call(p_rwkv, rwkv_mu[0][None, :], lora_b, lora_w, rwkv_g2[0],
                                                   rwkv_k_k[0][None, :], rwkv_r_k[0].reshape(1, gw), bd, n_ctx_tiles)
    y = _rwkv_scan_call(r, v, kk, k, lw, a, rwkv_k_a[0][None, :], n_ctx)
    o1 = _rwkv_finish_call(y, bonus, g, rwkv_gn_g[0][None, :], rwkv_gn_b[0][None, :], bd)

    lam_init0 = 0.8 - 0.6 * math.exp(-0.3 * 0)
    lam = (jnp.exp(jnp.sum(diff_lq1[0] * diff_lk1[0])) - jnp.exp(jnp.sum(diff_lq2[0] * diff_lk2[0]))
           + lam_init0).reshape(1).astype(F32)
    o2 = _diff_attn_call(lam, dq, dk, dv, diff_subln_g[0][None, :], n_ctx, lam_init0)

    row = lambda t: t[None, :]
    h1 = _outproj_ffn_call(o1, o2, h0, modsel(0), w_out[0].astype(BF16), ffn_w1[0].astype(BF16),
                           ffn_w2[0].astype(BF16), row(ln_mix_g[0]), row(ln_mix_b[0]), row(ln_ffn_g[0]),
                           row(ln_ffn_b[0]), o_off=0, h_off=0, n_out_tiles=n_ctx_tiles + n_lat_tiles,
                           n_ctx_tiles=n_ctx_tiles, alpha=alpha, name="outproj_ffn_0")

    conv_dim = gw + 2 * SSD_GROUPS * SSD_STATE
    ssd_cols = gw + conv_dim + 2 * SSD_HEADS
    kvw = SWA_KV_HEADS * HEAD_DIM
    wcd = cd_w_in[0]
    w1cat = jnp.concatenate([wcd[:, ssd_cols:], wcd[:, :gw], wcd[:, gw:gw + conv_dim],
                             wcd[:, gw + conv_dim:ssd_cols], jnp.zeros((d, LANES - 2 * SSD_HEADS), F32)], 1)
    c_q, c_k, c_v, c_z, c_x, c_dt = 0, gw, gw + kvw, gw + 2 * kvw, 2 * gw + 2 * kvw, 2 * gw + 2 * kvw + conv_dim
    plan1 = ((c_q, gw, True, q_scale), (c_k, kvw, True, 1.0), (c_v, kvw, False, 1.0), (c_z, gw, False, 1.0),
             (c_x, conv_dim, False, 1.0), (c_dt, LANES, False, 1.0))
    sq, sk, sv, pz, xbc, dt_raw = _inproj_call(h1, modsel(1), w1cat.astype(BF16), cos_t, sin_t, plan1,
                                               (BF16, BF16, BF16, F32, F32, F32), n_ctx_tiles, "inproj_cd")

    pad_l = lambda t: jnp.concatenate([t, jnp.zeros((LANES - t.shape[0],), F32)])[None, :]
    a_neg = -jnp.exp(ssd_a_log[0].astype(F32))
    conv_w_pad = jnp.concatenate([ssd_conv_w[0], jnp.zeros((8 - SSD_CONV, conv_dim), F32)], 0)
    xc, dt_sp, adt = _ssd_prep_call(xbc, dt_raw, conv_w_pad, ssd_conv_b[0][None, :], pad_l(ssd_dt_bias[0].reshape(-1)),
                                    pad_l(a_neg.reshape(-1)), n_ctx_tiles)
    adt_t = jnp.swapaxes(adt[:, :, :2 * SSD_HEADS], 1, 2)
    jj = jnp.arange(LANES, dtype=jnp.int32)[:, None]
    hh = (jnp.arange(gw, dtype=jnp.int32) // HEAD_DIM)[None, :]
    e_sel = jnp.stack([(jj == hh), (jj == hh + SSD_HEADS)]).astype(BF16)
    y_ssd = _ssd_scan_call(xc, dt_sp, adt, adt_t, e_sel, n_ctx)
    o1 = _ssd_finish_call(y_ssd, xc, pz, jnp.repeat(ssd_d[0], HEAD_DIM)[None, :], ssd_norm_g[0][None, :],
                          n_ctx_tiles, n_lat_tiles)
    o2 = _swa_call(swa_sink[0].astype(F32), sq, sk, sv, n_ctx)

    return _outproj_ffn_call(o1, o2, h1, modsel(1), w_out[1].astype(BF16), ffn_w1[1].astype(BF16),
                             ffn_w2[1].astype(BF16), row(ln_mix_g[1]), row(ln_mix_b[1]), row(ln_ffn_g[1]),
                             row(ln_ffn_b[1]), o_off=0, h_off=n_ctx_tiles, n_out_tiles=n_lat_tiles,
                             n_ctx_tiles=n_ctx_tiles, alpha=alpha, name="outproj_ffn_1")
```

```python
import functools
import math

import jax
import jax.numpy as jnp
from jax import lax
from jax.experimental import pallas as pl
from jax.experimental.pallas import tpu as pltpu

F32 = jnp.float32
BF16 = jnp.bfloat16

HEAD_DIM = 64
GRID_W = 64
ROPE_BASE = 10000.0
LN_EPS = 1e-5
RWKV_GN_EPS = 64e-5
RWKV_LORA_W = 64
RWKV_LORA_A = 64
RWKV_LORA_G = 128
DIFF_HEADS = 4
SSD_HEADS = 8
SSD_GROUPS = 2
SSD_STATE = 128
SSD_CONV = 5
SWA_HEADS = 8
SWA_KV_HEADS = 2
SWA_WINDOW = 128

ROW_TILE = 256
HALO = 8
RWKV_CHUNK = 64
SSD_CHUNK = 128
SWA_TILE = 128
LANES = 128
VMEM_LIMIT = 56 * 1024 * 1024


def _cparams(n_axes):
    return pltpu.CompilerParams(dimension_semantics=("arbitrary",) * n_axes, vmem_limit_bytes=VMEM_LIMIT)


def _dot(a, b):
    return jnp.dot(a, b, preferred_element_type=F32)


def _dot_nt(a, b):
    return lax.dot_general(a, b, (((1,), (1,)), ((), ())), preferred_element_type=F32)


def _split3(a):
    hi = a.astype(BF16)
    r1 = a - hi.astype(F32)
    mid = r1.astype(BF16)
    lo = (r1 - mid.astype(F32)).astype(BF16)
    return hi, mid, lo


def _dot_f32_by_exact(a, b_exact):
    hi, mid, lo = _split3(a)
    return _dot(hi, b_exact) + _dot(mid, b_exact) + _dot(lo, b_exact)


def _dot_exact_by_f32(a_exact, b):
    hi, mid, lo = _split3(b)
    return _dot(a_exact, hi) + _dot(a_exact, mid) + _dot(a_exact, lo)


def _dot_hilo(a, b):
    ah = a.astype(BF16)
    al = (a - ah.astype(F32)).astype(BF16)
    bh = b.astype(BF16)
    bl = (b - bh.astype(F32)).astype(BF16)
    return _dot(ah, bh) + _dot(ah, bl) + _dot(al, bh)


def _sigmoid(x):
    return 1.0 / (1.0 + jnp.exp(-x))


def _softplus(x):
    return jnp.maximum(x, 0.0) + jnp.log(1.0 + jnp.exp(-jnp.abs(x)))


def _layer_norm(y, g, b, eps):
    mu = jnp.mean(y, -1, keepdims=True)
    yc = y - mu
    var = jnp.mean(yc * yc, -1, keepdims=True)
    return yc * lax.rsqrt(var + eps) * g + b


def _mod_kernel(c_ref, w_ref, b_ref, o_ref):
    c = c_ref[...]
    o_ref[0] = _dot_hilo(c * _sigmoid(c), w_ref[0]) + b_ref[0]


def _mod_call(c_pad, mod_w, mod_b):
    depth, d, n = mod_w.shape
    rows = c_pad.shape[0]
    tn = 1536
    return pl.pallas_call(
        _mod_kernel,
        grid=(depth, n // tn),
        in_specs=[pl.BlockSpec((rows, d), lambda i, j: (0, 0)),
                  pl.BlockSpec((1, d, tn), lambda i, j: (i, 0, j)),
                  pl.BlockSpec((1, 1, tn), lambda i, j: (i, 0, j))],
        out_specs=pl.BlockSpec((1, rows, tn), lambda i, j: (i, 0, j)),
        out_shape=jax.ShapeDtypeStruct((depth, rows, n), F32),
        compiler_params=_cparams(2),
        name="adaln_mod",
    )(c_pad, mod_w, mod_b.reshape(depth, 1, n))


def _rope(x, cos, sin):
    lane = lax.broadcasted_iota(jnp.int32, (x.shape[0], LANES), 1)
    first = (lane % 32) < 16
    parts = []
    for g in range(0, x.shape[1], LANES):
        xg = x[:, g:g + LANES]
        parts.append(jnp.where(first, pltpu.roll(xg, LANES - 16, 1), pltpu.roll(xg, 16, 1)))
    sw = parts[0] if len(parts) == 1 else jnp.concatenate(parts, 1)
    return x * cos + sw * sin


def _inproj_kernel(x_ref, sh_ref, sc_ref, w_ref, cos_ref, sin_ref, *o_refs, plan):
    xm = (x_ref[0] * (1.0 + sc_ref[0, 0, 0]) + sh_ref[0, 0, 0]).astype(BF16)
    for o_ref, (c0, width, rope, mult) in zip(o_refs, plan):
        for j in range(0, width, 256):
            cw = min(256, width - j)
            acc = _dot(xm, w_ref[:, c0 + j:c0 + j + cw])
            if rope:
                acc = _rope(acc, cos_ref[:, :cw], sin_ref[:, :cw])
            if mult != 1.0:
                acc = acc * mult
            o_ref[0, :, j:j + cw] = acc.astype(o_ref.dtype)


def _inproj_call(h, modsel, w_bf16, cos_t, sin_t, plan, out_dtypes, n_ctx_tiles, name):
    b, l, d = h.shape
    n_tiles = l // ROW_TILE
    seg = lambda bi, t: (bi, (t >= n_ctx_tiles).astype(jnp.int32), 0, 0, 0)
    seg_scale = lambda bi, t: (bi, (t >= n_ctx_tiles).astype(jnp.int32), 1, 0, 0)
    out_shape = [jax.ShapeDtypeStruct((b, l, width), dt) for (_, width, _, _), dt in zip(plan, out_dtypes)]
    out_specs = [pl.BlockSpec((1, ROW_TILE, width), lambda bi, t: (bi, t, 0)) for (_, width, _, _) in plan]
    return pl.pallas_call(
        functools.partial(_inproj_kernel, plan=plan),
        grid=(b, n_tiles),
        in_specs=[pl.BlockSpec((1, ROW_TILE, d), lambda bi, t: (bi, t, 0)),
                  pl.BlockSpec((1, 1, 1, 1, d), seg),
                  pl.BlockSpec((1, 1, 1, 1, d), seg_scale),
                  pl.BlockSpec(w_bf16.shape, lambda bi, t: (0, 0), pipeline_mode=pl.Buffered(1)),
                  pl.BlockSpec((ROW_TILE, cos_t.shape[1]), lambda bi, t: (t, 0)),
                  pl.BlockSpec((ROW_TILE, sin_t.shape[1]), lambda bi, t: (t, 0))],
        out_specs=out_specs,
        out_shape=out_shape,
        compiler_params=_cparams(2),
        name=name,
    )(h, modsel, modsel, w_bf16, cos_t, sin_t)


def _rwkv_prep_kernel(p_ref, hp_ref, hn_ref, mu_ref, bias_ref, lora_ref, g2_ref, kk_ref, rk_ref, bd_ref,
                      r_o, v_o, kk_o, k_o, g_o, bonus_o, lw_o, a_o, *, n_ctx_tiles, n_tiles):
    t = pl.program_id(1)
    gw = r_o.shape[-1]
    p = p_ref[0]
    rows = p.shape[0]
    first = jnp.logical_or(t == 0, t == n_ctx_tiles)
    last = jnp.logical_or(t == n_ctx_tiles - 1, t == n_tiles - 1)
    prev_row = jnp.where(first, 0.0, hp_ref[0, HALO - 1:HALO, :])
    next_row = jnp.where(last, 0.0, hn_ref[0, 0:1, :])
    rowi = lax.broadcasted_iota(jnp.int32, p.shape, 0)
    prev = jnp.where(rowi == 0, prev_row, pltpu.roll(p, 1, 0))
    nxt = jnp.where(rowi == rows - 1, next_row, pltpu.roll(p, rows - 1, 0))
    ps = p + mu_ref[...] * (0.5 * (prev + nxt) - p)

    r = ps[:, :gw]
    k = ps[:, gw:2 * gw]
    v = ps[:, 2 * gw:3 * gw]
    slab = ps[:, 3 * gw:3 * gw + LANES]
    xg = ps[:, 3 * gw + LANES:3 * gw + 2 * LANES]
    lane = lax.broadcasted_iota(jnp.int32, slab.shape, 1)
    lora_in = jnp.where(lane < RWKV_LORA_W, jnp.tanh(slab), slab)
    pre = _dot_hilo(lora_in, lora_ref[...]) + bias_ref[...]
    bd = bd_ref[...]
    for d in range(2):
        w_log = -_softplus(-pre[:, d * gw:(d + 1) * gw]) - 0.5
        lw_o[0, d] = -jnp.exp(w_log)
        a_o[0, d] = _sigmoid(pre[:, (2 + d) * gw:(3 + d) * gw])
    g_o[0] = _dot_hilo(_sigmoid(xg), g2_ref[...])
    kk = k * kk_ref[...]
    ss = _dot_f32_by_exact(kk * kk, bd)
    kk_o[0] = kk * lax.rsqrt(jnp.maximum(ss, 1e-24))
    bonus_o[0] = _dot_f32_by_exact(r * k * rk_ref[...], bd) * v
    r_o[0] = r
    k_o[0] = k
    v_o[0] = v


def _rwkv_prep_call(p, mu, bias, lora_w, g2, k_k, r_k, bd, n_ctx_tiles):
    b, l, cols = p.shape
    gw = k_k.shape[-1]
    n_tiles = l // ROW_TILE
    hb = ROW_TILE // HALO
    n_hb = l // HALO
    full = lambda a: pl.BlockSpec(a.shape, lambda bi, t: (0,) * a.ndim)
    tok = pl.BlockSpec((1, ROW_TILE, gw), lambda bi, t: (bi, t, 0))
    tok2 = pl.BlockSpec((1, 2, ROW_TILE, gw), lambda bi, t: (bi, 0, t, 0))
    s1 = jax.ShapeDtypeStruct((b, l, gw), F32)
    s2 = jax.ShapeDtypeStruct((b, 2, l, gw), F32)
    return pl.pallas_call(
        functools.partial(_rwkv_prep_kernel, n_ctx_tiles=n_ctx_tiles, n_tiles=n_tiles),
        grid=(b, n_tiles),
        in_specs=[pl.BlockSpec((1, ROW_TILE, cols), lambda bi, t: (bi, t, 0)),
                  pl.BlockSpec((1, HALO, cols), lambda bi, t: (bi, jnp.maximum(t * hb - 1, 0), 0)),
                  pl.BlockSpec((1, HALO, cols), lambda bi, t: (bi, jnp.minimum((t + 1) * hb, n_hb - 1), 0)),
                  full(mu), full(bias), full(lora_w), full(g2), full(k_k), full(r_k), full(bd)],
        out_specs=[tok, tok, tok, tok, tok, tok, tok2, tok2],
        out_shape=[s1, s1, s1, s1, s1, s1, s2, s2],
        compiler_params=_cparams(2),
        name="rwkv_prep",
    )(p, p, p, mu, bias, lora_w, g2, k_k, r_k, bd)


def _rwkv_scan_kernel(rf_ref, vf_ref, kkf_ref, kf_ref, lwf_ref, af_ref, rr_ref, vr_ref, kkr_ref, kr_ref, lwr_ref,
                      ar_ref, ka_ref, yf_ref, yr_ref, s_ref):
    i = pl.program_id(1)
    c = rf_ref.shape[1]
    n_pairs = rf_ref.shape[2] // LANES
    c2 = 2 * c

    @pl.when(i == 0)
    def _():
        s_ref[...] = jnp.zeros_like(s_ref)

    row = lax.broadcasted_iota(jnp.int32, (c, c), 0)
    col = lax.broadcasted_iota(jnp.int32, (c, c), 1)
    row2 = lax.broadcasted_iota(jnp.int32, (c2, c2), 0)
    col2 = lax.broadcasted_iota(jnp.int32, (c2, c2), 1)
    same = (row2 // c) == (col2 // c)
    eye = (row2 == col2).astype(F32)
    low = lax.broadcasted_iota(jnp.int32, (c, LANES), 1) < HEAD_DIM

    def stack(x):
        return jnp.concatenate([jnp.where(low, x, 0.0), jnp.where(low, 0.0, x)], 0)

    chains = []
    dirs = ((1, rf_ref, vf_ref, kkf_ref, kf_ref, lwf_ref, af_ref, yf_ref),
            (-1, rr_ref, vr_ref, kkr_ref, kr_ref, lwr_ref, ar_ref, yr_ref))
    for d, (sgn, r_ref, v_ref, kk_ref, k_ref, lw_ref, a_ref, y_ref) in enumerate(dirs):
        tri = ((row - col) * sgn >= 0).astype(BF16)
        lw = lw_ref[0, 0]
        cum = _dot_exact_by_f32(tri, lw)
        total = cum[c - 1:c, :] if sgn > 0 else cum[0:1, :]
        a = a_ref[0, 0]
        kk = kk_ref[0]
        p_inv = jnp.exp(-cum)
        a_t = -kk * jnp.exp(cum - lw)
        b_t = kk * a * p_inv
        k_t = k_ref[0] * (1.0 + (a - 1.0) * ka_ref[...]) * p_inv
        r_t = r_ref[0] * jnp.exp(cum)
        v_all = v_ref[0]
        p_end = jnp.exp(total)
        dt2 = (row2 - col2) * sgn
        strict = jnp.logical_and(same, dt2 > 0)
        incl = jnp.logical_and(same, dt2 >= 0)
        for pr in range(n_pairs):
            sl = slice(pr * LANES, (pr + 1) * LANES)
            ah, bh, kh, rh, vh = (stack(x[:, sl]) for x in (a_t, b_t, k_t, r_t, v_all))
            chains.append(dict(d=d, pr=pr, sl=sl, y_ref=y_ref, strict=strict, incl=incl, p_end=p_end[:, sl], vh=vh,
                               ar=jnp.concatenate([ah, rh], 0).astype(BF16),
                               bk=jnp.concatenate([bh, kh], 0).astype(BF16)))

    for ch in chains:
        gram = _dot_nt(ch["ar"], ch["bk"])
        l_ab = jnp.where(ch["strict"], gram[:c2, :c2], 0.0)
        ch["l_ak"] = jnp.where(ch["strict"], gram[:c2, c2:], 0.0).astype(BF16)
        ch["l_r"] = jnp.concatenate([jnp.where(ch["incl"], gram[c2:, :c2], 0.0),
                                     jnp.where(ch["incl"], gram[c2:, c2:], 0.0)], 1).astype(BF16)
        ch["inv"] = eye + l_ab
        ch["pw"] = l_ab.astype(BF16)
    for _ in range(max(1, int(math.ceil(math.log2(c))) - 1)):
        for ch in chains:
            ch["pw"] = _dot(ch["pw"], ch["pw"]).astype(BF16)
        for ch in chains:
            ch["inv"] = ch["inv"] + _dot(ch["inv"].astype(BF16), ch["pw"])
    for ch in chains:
        ch["s0"] = s_ref[ch["d"], ch["pr"]]
        ch["ars"] = _dot_nt(ch["ar"], ch["s0"].astype(BF16))
        ch["lv"] = _dot(ch["l_ak"], ch["vh"].astype(BF16))
    for ch in chains:
        u = _dot(ch["inv"].astype(BF16), (ch["ars"][:c2] + ch["lv"]).astype(BF16))
        ch["uv"] = jnp.concatenate([u, ch["vh"]], 0)
    for ch in chains:
        y = ch["ars"][c2:] + _dot(ch["l_r"], ch["uv"].astype(BF16))
        ch["y_ref"][0, :, ch["sl"]] = y[:c] + y[c:]
    for ch in chains:
        s_ref[ch["d"], ch["pr"]] = (ch["s0"] + _dot(ch["uv"].T.astype(BF16), ch["bk"])) * ch["p_end"]


def _rwkv_scan_call(r, v, kk, k, lw, a, k_a, n_ctx):
    b, l, gw = r.shape
    c = RWKV_CHUNK
    nc = l // c
    ncc = n_ctx // c

    def rev(i):
        return jnp.where(i < ncc, ncc - 1 - i, nc - 1 - (i - ncc))

    tok_f = pl.BlockSpec((1, c, gw), lambda bi, i: (bi, i, 0))
    tok_r = pl.BlockSpec((1, c, gw), lambda bi, i: (bi, rev(i), 0))
    dir_f = pl.BlockSpec((1, 1, c, gw), lambda bi, i: (bi, 0, i, 0))
    dir_r = pl.BlockSpec((1, 1, c, gw), lambda bi, i: (bi, 1, rev(i), 0))
    ys = jax.ShapeDtypeStruct((b, l, gw), F32)
    return pl.pallas_call(
        _rwkv_scan_kernel,
        grid=(b, nc),
        in_specs=[tok_f, tok_f, tok_f, tok_f, dir_f, dir_f, tok_r, tok_r, tok_r, tok_r, dir_r, dir_r,
                  pl.BlockSpec(k_a.shape, lambda bi, i: (0, 0))],
        out_specs=[tok_f, tok_r],
        out_shape=[ys, ys],
        scratch_shapes=[pltpu.VMEM((2, gw // LANES, LANES, LANES), F32)],
        compiler_params=_cparams(2),
        name="rwkv_scan",
    )(r, v, kk, k, lw, a, r, v, kk, k, lw, a, k_a)


def _rwkv_finish_kernel(yf_ref, yr_ref, bonus_ref, g_ref, gg_ref, gb_ref, bd_ref, o_ref):
    y = yf_ref[0] + yr_ref[0]
    bd = bd_ref[...]
    inv_n = 1.0 / HEAD_DIM
    mu = _dot_f32_by_exact(y, bd) * inv_n
    yc = y - mu
    var = _dot_f32_by_exact(yc * yc, bd) * inv_n
    yn = yc * lax.rsqrt(var + RWKV_GN_EPS) * gg_ref[...] + gb_ref[...]
    o_ref[0] = (yn + bonus_ref[0]) * g_ref[0]


def _rwkv_finish_call(y_f, y_r, bonus, g, gn_g, gn_b, bd):
    b, l, gw = y_f.shape
    tok = pl.BlockSpec((1, ROW_TILE, gw), lambda bi, t: (bi, t, 0))
    full = lambda a: pl.BlockSpec(a.shape, lambda bi, t: (0,) * a.ndim)
    return pl.pallas_call(
        _rwkv_finish_kernel,
        grid=(b, l // ROW_TILE),
        in_specs=[tok, tok, tok, tok, full(gn_g), full(gn_b), full(bd)],
        out_specs=tok,
        out_shape=jax.ShapeDtypeStruct((b, l, gw), F32),
        compiler_params=_cparams(2),
        name="rwkv_finish",
    )(y_f, y_r, bonus, g, gn_g, gn_b, bd)


def _diff_attn_kernel(lam_ref, q_ref, k_ref, v_ref, g_ref, o_ref, *, n_ctx, n_ctx_tiles, out_scale):
    t = pl.program_id(1)
    lam = lam_ref[0]
    vd = 2 * HEAD_DIM

    def run(nk):
        tq = q_ref.shape[1]
        for h in range(DIFF_HEADS):
            es, invs = [], []
            for m in range(2):
                c0 = (2 * h + m) * HEAD_DIM
                s = _dot_nt(q_ref[0, :, c0:c0 + HEAD_DIM], k_ref[0, :nk, c0:c0 + HEAD_DIM])
                e = jnp.exp2(s - jnp.max(s, -1, keepdims=True))
                invs.append(1.0 / jnp.sum(e, -1, keepdims=True))
                es.append(e.astype(BF16))
            pv = _dot(jnp.concatenate(es, 0), v_ref[0, :nk, h * vd:(h + 1) * vd])
            o = pv[:tq] * invs[0] - pv[tq:] * (lam * invs[1])
            o = o * lax.rsqrt(jnp.mean(o * o, -1, keepdims=True) + LN_EPS) * (g_ref[...] * out_scale)
            o_ref[0, :, h * vd:(h + 1) * vd] = o

    @pl.when(t < n_ctx_tiles)
    def _():
        run(n_ctx)

    @pl.when(t >= n_ctx_tiles)
    def _():
        run(k_ref.shape[1])


def _diff_attn_call(lam, q, k, v, subln_g, n_ctx, lam_init):
    b, l, w = q.shape
    kv = pl.BlockSpec((1, l, w), lambda bi, t: (bi, 0, 0))
    tok = pl.BlockSpec((1, ROW_TILE, w), lambda bi, t: (bi, t, 0))
    return pl.pallas_call(
        functools.partial(_diff_attn_kernel, n_ctx=n_ctx, n_ctx_tiles=n_ctx // ROW_TILE, out_scale=1.0 - lam_init),
        grid=(b, l // ROW_TILE),
        in_specs=[pl.BlockSpec(memory_space=pltpu.SMEM), tok, kv, kv,
                  pl.BlockSpec(subln_g.shape, lambda bi, t: (0, 0))],
        out_specs=tok,
        out_shape=jax.ShapeDtypeStruct((b, l, w), F32),
        compiler_params=_cparams(2),
        name="diff_attn",
    )(lam, q, k, v, subln_g)


def _outproj_ffn_kernel(o1_ref, o2_ref, h_ref, gm_ref, sh_ref, sc_ref, gf_ref, wo_ref, w1_ref, w2_ref,
                        lmg_ref, lmb_ref, lfg_ref, lfb_ref, out_ref, *, alpha):
    gw = o1_ref.shape[-1]
    o = _dot(o1_ref[0].astype(BF16), wo_ref[:gw, :]) + _dot(o2_ref[0].astype(BF16), wo_ref[gw:, :])
    h1 = _layer_norm(alpha * h_ref[0] + gm_ref[0, 0, 0] * o, lmg_ref[...], lmb_ref[...], LN_EPS)
    xm = (h1 * (1.0 + sc_ref[0, 0, 0]) + sh_ref[0, 0, 0]).astype(BF16)
    hidden = w1_ref.shape[1]
    step = 512

    def up(j):
        hj = jnp.maximum(_dot(xm, w1_ref[:, j:j + step]), 0.0)
        return (hj * hj).astype(BF16)

    acc = None
    act = up(0)
    for j in range(0, hidden, step):
        nxt = up(j + step) if j + step < hidden else None
        part = _dot(act, w2_ref[j:j + step, :])
        acc = part if acc is None else acc + part
        act = nxt
    out_ref[0] = _layer_norm(alpha * h1 + gf_ref[0, 0, 0] * acc, lfg_ref[...], lfb_ref[...], LN_EPS)


def _outproj_ffn_call(o1, o2, h, modsel, wo, w1, w2, lmg, lmb, lfg, lfb, *, o_off, h_off, n_out_tiles,
                      n_ctx_tiles, alpha, name):
    b, _, gw = o1.shape
    d = h.shape[-1]
    seg = lambda col: (lambda bi, t: (bi, ((t + h_off) >= n_ctx_tiles).astype(jnp.int32), col, 0, 0))
    modspec = lambda col: pl.BlockSpec((1, 1, 1, 1, d), seg(col))
    full = lambda a: pl.BlockSpec(a.shape, lambda bi, t: (0,) * a.ndim, pipeline_mode=pl.Buffered(1))
    otok = pl.BlockSpec((1, ROW_TILE, gw), lambda bi, t: (bi, t + o_off, 0))
    return pl.pallas_call(
        functools.partial(_outproj_ffn_kernel, alpha=alpha),
        grid=(b, n_out_tiles),
        in_specs=[otok, otok, pl.BlockSpec((1, ROW_TILE, d), lambda bi, t: (bi, t + h_off, 0)),
                  modspec(2), modspec(3), modspec(4), modspec(5),
                  full(wo), full(w1), full(w2), full(lmg), full(lmb), full(lfg), full(lfb)],
        out_specs=pl.BlockSpec((1, ROW_TILE, d), lambda bi, t: (bi, t, 0)),
        out_shape=jax.ShapeDtypeStruct((b, n_out_tiles * ROW_TILE, d), F32),
        compiler_params=_cparams(2),
        name=name,
    )(o1, o2, h, modsel, modsel, modsel, modsel, wo, w1, w2, lmg, lmb, lfg, lfb)


def _ssd_prep_kernel(x_ref, hp_ref, hn_ref, dt_ref, cw_ref, cb_ref, dtb_ref, an_ref, xc_o, dt_o, adt_o,
                     *, n_ctx_tiles, n_tiles):
    t = pl.program_id(1)
    x = x_ref[0]
    rows = x.shape[0]
    first = jnp.logical_or(t == 0, t == n_ctx_tiles)
    last = jnp.logical_or(t == n_ctx_tiles - 1, t == n_tiles - 1)
    ext = jnp.concatenate([jnp.where(first, 0.0, hp_ref[0]), x, jnp.where(last, 0.0, hn_ref[0])], 0)
    n_ext = rows + 2 * HALO
    acc = cb_ref[...] + cw_ref[SSD_CONV // 2:SSD_CONV // 2 + 1, :] * x
    for j in range(SSD_CONV):
        s = j - SSD_CONV // 2
        if s == 0:
            continue
        shifted = pltpu.roll(ext, (n_ext - s) % n_ext, 0)[HALO:HALO + rows]
        acc = acc + cw_ref[j:j + 1, :] * shifted
    xc_o[0] = acc * _sigmoid(acc)
    dt = _softplus(dt_ref[0] + dtb_ref[...])
    dt_o[0] = dt
    adt_o[0] = dt * an_ref[...]


def _ssd_prep_call(xbc, dt, conv_w, conv_b, dt_bias, a_neg, n_ctx_tiles):
    b, l, cw = xbc.shape
    n_tiles = l // ROW_TILE
    hb = ROW_TILE // HALO
    n_hb = l // HALO
    full = lambda a: pl.BlockSpec(a.shape, lambda bi, t: (0,) * a.ndim)
    tokd = pl.BlockSpec((1, ROW_TILE, LANES), lambda bi, t: (bi, t, 0))
    sd = jax.ShapeDtypeStruct((b, l, LANES), F32)
    return pl.pallas_call(
        functools.partial(_ssd_prep_kernel, n_ctx_tiles=n_ctx_tiles, n_tiles=n_tiles),
        grid=(b, n_tiles),
        in_specs=[pl.BlockSpec((1, ROW_TILE, cw), lambda bi, t: (bi, t, 0)),
                  pl.BlockSpec((1, HALO, cw), lambda bi, t: (bi, jnp.maximum(t * hb - 1, 0), 0)),
                  pl.BlockSpec((1, HALO, cw), lambda bi, t: (bi, jnp.minimum((t + 1) * hb, n_hb - 1), 0)),
                  tokd, full(conv_w), full(conv_b), full(dt_bias), full(a_neg)],
        out_specs=[pl.BlockSpec((1, ROW_TILE, cw), lambda bi, t: (bi, t, 0)), tokd, tokd],
        out_shape=[jax.ShapeDtypeStruct((b, l, cw), F32), sd, sd],
        compiler_params=_cparams(2),
        name="ssd_prep",
    )(xbc, xbc, xbc, dt, conv_w, conv_b, dt_bias, a_neg)


def _ssd_scan_kernel(xcf_ref, dtf_ref, adtf_ref, adttf_ref, xcr_ref, dtr_ref, adtr_ref, adttr_ref, e_ref,
                     yf_ref, yr_ref, st_ref):
    i = pl.program_id(1)
    q = xcf_ref.shape[1]
    gw = yf_ref.shape[-1]
    n = SSD_STATE
    hpg = SSD_HEADS // SSD_GROUPS
    gcols = hpg * HEAD_DIM

    @pl.when(i == 0)
    def _():
        st_ref[...] = jnp.zeros_like(st_ref)

    row = lax.broadcasted_iota(jnp.int32, (q, q), 0)
    col = lax.broadcasted_iota(jnp.int32, (q, q), 1)
    chains = []
    dirs = ((1, xcf_ref, dtf_ref, adtf_ref, adttf_ref, yf_ref), (-1, xcr_ref, dtr_ref, adtr_ref, adttr_ref, yr_ref))
    for d, (sgn, xc_ref, dt_ref, adt_ref, adtt_ref, y_ref) in enumerate(dirs):
        incl = (row - col) * sgn >= 0
        tri = incl.astype(BF16)
        tri_t = ((col - row) * sgn >= 0).astype(BF16)
        cs = _dot_exact_by_f32(tri, adt_ref[0])
        cs_t = _dot_f32_by_exact(adtt_ref[0], tri_t)
        e = e_ref[d]
        cse = _dot_f32_by_exact(cs, e)
        dte = _dot_f32_by_exact(dt_ref[0], e)
        tot = cse[q - 1:q, :] if sgn > 0 else cse[0:1, :]
        xc = xc_ref[0]
        xdt = xc[:, :gw] * dte
        e_cs = jnp.exp(cse)
        x_end = (xdt * jnp.exp(tot - cse)).astype(BF16)
        e_tot = jnp.exp(tot)
        xdt16 = xdt.astype(BF16)
        for g in range(SSD_GROUPS):
            gsl = slice(g * gcols, (g + 1) * gcols)
            lms = []
            for hh in range(hpg):
                h = g * hpg + hh
                colv = cse[:, h * HEAD_DIM:h * HEAD_DIM + 1]
                rowv = cs_t[d * SSD_HEADS + h:d * SSD_HEADS + h + 1, :]
                lms.append(jnp.where(incl, jnp.exp(colv - rowv), 0.0))
            chains.append(dict(d=d, g=g, gsl=gsl, y_ref=y_ref, lms=lms, e_cs=e_cs[:, gsl], e_tot=e_tot[:, gsl],
                               x_end=x_end[:, gsl], xdt16=xdt16[:, gsl],
                               bg=xc[:, gw + g * n:gw + (g + 1) * n],
                               cg=xc[:, gw + (SSD_GROUPS + g) * n:gw + (SSD_GROUPS + g + 1) * n].astype(BF16)))
    for ch in chains:
        ch["cb"] = _dot_nt(ch["cg"], ch["bg"].astype(BF16))
        ch["st"] = st_ref[ch["d"], ch["g"]]
    for ch in chains:
        ch["y_off"] = _dot(ch["cg"], ch["st"].astype(BF16)) * ch["e_cs"]
    for ch in chains:
        ch["ys"] = [_dot((ch["cb"] * ch["lms"][hh]).astype(BF16), ch["xdt16"][:, hh * HEAD_DIM:(hh + 1) * HEAD_DIM])
                    for hh in range(hpg)]
    for ch in chains:
        ch["y_ref"][0, :, ch["gsl"]] = jnp.concatenate(ch["ys"], 1) + ch["y_off"]
        st_ref[ch["d"], ch["g"]] = ch["st"] * ch["e_tot"] + _dot(ch["bg"].T.astype(BF16), ch["x_end"])


def _ssd_scan_call(xc, dt, adt, adt_t, e_sel, n_ctx):
    b, l, cw = xc.shape
    gw = e_sel.shape[-1]
    q = SSD_CHUNK
    nc = l // q
    ncc = n_ctx // q

    def rev(i):
        return jnp.where(i < ncc, ncc - 1 - i, nc - 1 - (i - ncc))

    def specs(ch):
        tokd = pl.BlockSpec((1, q, LANES), lambda bi, i: (bi, ch(i), 0))
        return [pl.BlockSpec((1, q, cw), lambda bi, i: (bi, ch(i), 0)), tokd, tokd,
                pl.BlockSpec((1, 2 * SSD_HEADS, q), lambda bi, i: (bi, 0, ch(i)))]

    ys = jax.ShapeDtypeStruct((b, l, gw), F32)
    return pl.pallas_call(
        _ssd_scan_kernel,
        grid=(b, nc),
        in_specs=specs(lambda i: i) + specs(rev) + [pl.BlockSpec(e_sel.shape, lambda bi, i: (0, 0, 0))],
        out_specs=[pl.BlockSpec((1, q, gw), lambda bi, i: (bi, i, 0)),
                   pl.BlockSpec((1, q, gw), lambda bi, i: (bi, rev(i), 0))],
        out_shape=[ys, ys],
        scratch_shapes=[pltpu.VMEM((2, SSD_GROUPS, SSD_STATE, gw // SSD_GROUPS), F32)],
        compiler_params=_cparams(2),
        name="ssd_scan",
    )(xc, dt, adt, adt_t, xc, dt, adt, adt_t, e_sel)


def _ssd_finish_kernel(yf_ref, yr_ref, xc_ref, z_ref, dsk_ref, ng_ref, o_ref):
    gw = o_ref.shape[-1]
    z = z_ref[0]
    y = (yf_ref[0] + yr_ref[0] + dsk_ref[...] * xc_ref[0, :, :gw]) * (z * _sigmoid(z))
    gwid = gw // SSD_GROUPS
    for g in range(SSD_GROUPS):
        yg = y[:, g * gwid:(g + 1) * gwid]
        o_ref[0, :, g * gwid:(g + 1) * gwid] = (yg * lax.rsqrt(jnp.mean(yg * yg, -1, keepdims=True) + LN_EPS)
                                                * ng_ref[:, g * gwid:(g + 1) * gwid])


def _ssd_finish_call(y_f, y_r, xc, z, d_exp, norm_g, n_ctx_tiles, n_lat_tiles):
    b, l, gw = y_f.shape
    cw = xc.shape[-1]
    off = n_ctx_tiles
    full = lambda a: pl.BlockSpec(a.shape, lambda bi, t: (0,) * a.ndim)
    return pl.pallas_call(
        _ssd_finish_kernel,
        grid=(b, n_lat_tiles),
        in_specs=[pl.BlockSpec((1, ROW_TILE, gw), lambda bi, t: (bi, t + off, 0)),
                  pl.BlockSpec((1, ROW_TILE, gw), lambda bi, t: (bi, t + off, 0)),
                  pl.BlockSpec((1, ROW_TILE, cw), lambda bi, t: (bi, t + off, 0)),
                  pl.BlockSpec((1, ROW_TILE, gw), lambda bi, t: (bi, t + off, 0)),
                  full(d_exp), full(norm_g)],
        out_specs=pl.BlockSpec((1, ROW_TILE, gw), lambda bi, t: (bi, t, 0)),
        out_shape=jax.ShapeDtypeStruct((b, n_lat_tiles * ROW_TILE, gw), F32),
        compiler_params=_cparams(2),
        name="ssd_finish",
    )(y_f, y_r, xc, z, d_exp, norm_g)


def _swa_kernel(sink_ref, q_ref, k_ref, v_ref, o_ref, *, n_ctx, n_lat):
    i = pl.program_id(1)
    tq = q_ref.shape[1]
    span = tq + 2 * SWA_WINDOW
    nk = n_ctx + span
    group = SWA_HEADS // SWA_KV_HEADS
    start = jnp.clip(i * tq - SWA_WINDOW, 0, n_lat - span)
    w0 = pl.multiple_of(n_ctx + start, LANES)
    kpos = start + lax.broadcasted_iota(jnp.int32, (span, tq), 0)
    qpos = i * tq + lax.broadcasted_iota(jnp.int32, (span, tq), 1)
    bias = jnp.where(jnp.abs(qpos - kpos) <= SWA_WINDOW, 0.0, -1e30)
    bias = jnp.concatenate([jnp.zeros((n_ctx, tq), F32), bias], 0)
    bias = jnp.concatenate([bias] * group, 1)
    kcat = jnp.concatenate([k_ref[0, :n_ctx, :], k_ref[0, pl.ds(w0, span), :]], 0).astype(F32)
    vt = jnp.concatenate([v_ref[0, :n_ctx, :], v_ref[0, pl.ds(w0, span), :]], 0).astype(F32).T
    qs = jnp.concatenate([q_ref[0, :, g * LANES:(g + 1) * LANES] for g in range(group)], 0)
    k_low = lax.broadcasted_iota(jnp.int32, kcat.shape, 1) < HEAD_DIM
    v_low = lax.broadcasted_iota(jnp.int32, vt.shape, 0) < HEAD_DIM
    ss, es, rdens = [], [], []
    for kvh in range(SWA_KV_HEADS):
        km = jnp.where(k_low == (kvh == 0), kcat, 0.0).astype(BF16)
        ss.append(_dot_nt(km, qs) + bias)
    for kvh in range(SWA_KV_HEADS):
        sink = jnp.concatenate([jnp.full((1, tq), sink_ref[kvh * group + g], F32) for g in range(group)], 1)
        m = jnp.maximum(jnp.max(ss[kvh], 0, keepdims=True), sink)
        e = jnp.exp2(ss[kvh] - m)
        rdens.append(1.0 / (jnp.exp2(sink - m) + jnp.sum(e, 0, keepdims=True)))
        es.append(e.astype(BF16))
    ot = (_dot(jnp.where(v_low, vt, 0.0).astype(BF16), es[0]) * rdens[0]
          + _dot(jnp.where(v_low, 0.0, vt).astype(BF16), es[1]) * rdens[1])
    for g in range(group):
        o_ref[0, :, g * LANES:(g + 1) * LANES] = ot[:, g * tq:(g + 1) * tq].T


def _swa_call(sink, q, k, v, n_ctx):
    b, l, qw = q.shape
    kw = k.shape[-1]
    n_lat = l - n_ctx
    off = n_ctx // SWA_TILE
    kv = pl.BlockSpec((1, l, kw), lambda bi, t: (bi, 0, 0))
    return pl.pallas_call(
        functools.partial(_swa_kernel, n_ctx=n_ctx, n_lat=n_lat),
        grid=(b, n_lat // SWA_TILE),
        in_specs=[pl.BlockSpec(memory_space=pltpu.SMEM),
                  pl.BlockSpec((1, SWA_TILE, qw), lambda bi, t: (bi, t + off, 0)), kv, kv],
        out_specs=pl.BlockSpec((1, SWA_TILE, qw), lambda bi, t: (bi, t, 0)),
        out_shape=jax.ShapeDtypeStruct((b, n_lat, qw), F32),
        compiler_params=_cparams(2),
        name="swa",
    )(sink, q, k, v)


def _rope_tables(n_ctx, n_lat, width):
    half = HEAD_DIM // 4
    inv = ROPE_BASE ** (-jnp.arange(half, dtype=F32) / half)
    pos = jnp.arange(n_lat, dtype=jnp.int32)
    rows = (pos // GRID_W).astype(F32)[:, None] * inv
    cols = (pos % GRID_W).astype(F32)[:, None] * inv
    cos = jnp.concatenate([jnp.cos(rows), jnp.cos(rows), jnp.cos(cols), jnp.cos(cols)], -1)
    sin = jnp.concatenate([-jnp.sin(rows), jnp.sin(rows), -jnp.sin(cols), jnp.sin(cols)], -1)
    cos = jnp.concatenate([jnp.ones((n_ctx, HEAD_DIM), F32), cos], 0)
    sin = jnp.concatenate([jnp.zeros((n_ctx, HEAD_DIM), F32), sin], 0)
    reps = width // HEAD_DIM
    return jnp.tile(cos, (1, reps)), jnp.tile(sin, (1, reps))


def kernel(x, c, ctx, c_ctx, mod_w, mod_b, ln_mix_g, ln_mix_b, ln_ffn_g, ln_ffn_b, ffn_w1, ffn_w2, w_out, ab_w_in, rwkv_mu, rwkv_w0, rwkv_w2, rwkv_a0, rwkv_a2, rwkv_g2, rwkv_k_k, rwkv_k_a, rwkv_r_k, rwkv_gn_g, rwkv_gn_b, diff_lq1, diff_lk1, diff_lq2, diff_lk2, diff_subln_g, cd_w_in, ssd_conv_w, ssd_conv_b, ssd_dt_bias, ssd_a_log, ssd_d, ssd_norm_g, swa_sink):
    b, n_lat, d = x.shape
    n_ctx = ctx.shape[1]
    depth = mod_w.shape[0]
    assert depth == 2 and n_ctx % ROW_TILE == 0 and n_lat % ROW_TILE == 0 and n_lat % GRID_W == 0
    assert n_lat >= SWA_TILE + 2 * SWA_WINDOW
    gw = d // 2
    n_ctx_tiles = n_ctx // ROW_TILE
    n_lat_tiles = n_lat // ROW_TILE
    alpha = (2 * depth) ** 0.25
    log2e = math.log2(math.e)
    q_scale = HEAD_DIM ** -0.5 * log2e

    rows_pad = -(-(b + 1) // 8) * 8
    c_pad = jnp.zeros((rows_pad, d), F32).at[:b].set(c).at[b].set(c_ctx)
    m = _mod_call(c_pad, mod_w, mod_b)

    def modsel(i):
        return jnp.stack([jnp.broadcast_to(m[i, b], (b, 6 * d)), m[i, :b]], 1).reshape(b, 2, 6, 1, d)

    cos_t, sin_t = _rope_tables(n_ctx, n_lat, 256)
    h0 = jnp.concatenate([ctx, x], 1)

    rwkv_cols = 3 * gw + RWKV_LORA_W + RWKV_LORA_A + RWKV_LORA_G
    plan0 = ((0, rwkv_cols, False, 1.0), (rwkv_cols, gw, True, q_scale), (rwkv_cols + gw, gw, True, 1.0),
             (rwkv_cols + 2 * gw, gw, False, 1.0))
    p_rwkv, dq, dk, dv = _inproj_call(h0, modsel(0), ab_w_in[0].astype(BF16), cos_t, sin_t, plan0,
                                      (F32, BF16, BF16, BF16), n_ctx_tiles, "inproj_ab")

    zw = jnp.zeros((RWKV_LORA_W, gw), F32)
    lora_w = jnp.concatenate([jnp.concatenate([rwkv_w2[0, 0], rwkv_w2[0, 1], zw, zw], 1),
                              jnp.concatenate([zw, zw, rwkv_a2[0, 0], rwkv_a2[0, 1]], 1)], 0)
    lora_b = jnp.concatenate([rwkv_w0[0, 0], rwkv_w0[0, 1], rwkv_a0[0, 0], rwkv_a0[0, 1]])[None, :]
    head_id = jnp.arange(gw, dtype=jnp.int32) // HEAD_DIM
    bd = (head_id[:, None] == head_id[None, :]).astype(BF16)
    r, v, kk, k, g, bonus, lw, a = _rwkv_prep_call(p_rwkv, rwkv_mu[0][None, :], lora_b, lora_w, rwkv_g2[0],
                                                   rwkv_k_k[0][None, :], rwkv_r_k[0].reshape(1, gw), bd, n_ctx_tiles)
    y_f, y_r = _rwkv_scan_call(r, v, kk, k, lw, a, rwkv_k_a[0][None, :], n_ctx)
    o1 = _rwkv_finish_call(y_f, y_r, bonus, g, rwkv_gn_g[0][None, :], rwkv_gn_b[0][None, :], bd)

    lam_init0 = 0.8 - 0.6 * math.exp(-0.3 * 0)
    lam = (jnp.exp(jnp.sum(diff_lq1[0] * diff_lk1[0])) - jnp.exp(jnp.sum(diff_lq2[0] * diff_lk2[0]))
           + lam_init0).reshape(1).astype(F32)
    o2 = _diff_attn_call(lam, dq, dk, dv, diff_subln_g[0][None, :], n_ctx, lam_init0)

    row = lambda t: t[None, :]
    h1 = _outproj_ffn_call(o1, o2, h0, modsel(0), w_out[0].astype(BF16), ffn_w1[0].astype(BF16),
                           ffn_w2[0].astype(BF16), row(ln_mix_g[0]), row(ln_mix_b[0]), row(ln_ffn_g[0]),
                           row(ln_ffn_b[0]), o_off=0, h_off=0, n_out_tiles=n_ctx_tiles + n_lat_tiles,
                           n_ctx_tiles=n_ctx_tiles, alpha=alpha, name="outproj_ffn_0")

    conv_dim = gw + 2 * SSD_GROUPS * SSD_STATE
    ssd_cols = gw + conv_dim + 2 * SSD_HEADS
    kvw = SWA_KV_HEADS * HEAD_DIM
    wcd = cd_w_in[0]
    group = SWA_HEADS // SWA_KV_HEADS
    q_perm = jnp.arange(gw, dtype=jnp.int32).reshape(SWA_KV_HEADS, group, HEAD_DIM).transpose(1, 0, 2).reshape(-1)
    w1cat = jnp.concatenate([wcd[:, ssd_cols:ssd_cols + gw][:, q_perm], wcd[:, ssd_cols + gw:], wcd[:, :gw],
                             wcd[:, gw:gw + conv_dim], wcd[:, gw + conv_dim:ssd_cols],
                             jnp.zeros((d, LANES - 2 * SSD_HEADS), F32)], 1)
    w_out1 = jnp.concatenate([w_out[1, :gw], w_out[1, gw:][q_perm]], 0)
    c_q, c_k, c_v, c_z, c_x, c_dt = 0, gw, gw + kvw, gw + 2 * kvw, 2 * gw + 2 * kvw, 2 * gw + 2 * kvw + conv_dim
    plan1 = ((c_q, gw, True, q_scale), (c_k, kvw, True, 1.0), (c_v, kvw, False, 1.0), (c_z, gw, False, 1.0),
             (c_x, conv_dim, False, 1.0), (c_dt, LANES, False, 1.0))
    sq, sk, sv, pz, xbc, dt_raw = _inproj_call(h1, modsel(1), w1cat.astype(BF16), cos_t, sin_t, plan1,
                                               (BF16, BF16, BF16, F32, F32, F32), n_ctx_tiles, "inproj_cd")

    pad_l = lambda t: jnp.concatenate([t, jnp.zeros((LANES - t.shape[0],), F32)])[None, :]
    a_neg = -jnp.exp(ssd_a_log[0].astype(F32))
    conv_w_pad = jnp.concatenate([ssd_conv_w[0], jnp.zeros((8 - SSD_CONV, conv_dim), F32)], 0)
    xc, dt_sp, adt = _ssd_prep_call(xbc, dt_raw, conv_w_pad, ssd_conv_b[0][None, :], pad_l(ssd_dt_bias[0].reshape(-1)),
                                    pad_l(a_neg.reshape(-1)), n_ctx_tiles)
    adt_t = jnp.swapaxes(adt[:, :, :2 * SSD_HEADS], 1, 2)
    jj = jnp.arange(LANES, dtype=jnp.int32)[:, None]
    hh = (jnp.arange(gw, dtype=jnp.int32) // HEAD_DIM)[None, :]
    e_sel = jnp.stack([(jj == hh), (jj == hh + SSD_HEADS)]).astype(BF16)
    ys_f, ys_r = _ssd_scan_call(xc, dt_sp, adt, adt_t, e_sel, n_ctx)
    o1 = _ssd_finish_call(ys_f, ys_r, xc, pz, jnp.repeat(ssd_d[0], HEAD_DIM)[None, :], ssd_norm_g[0][None, :],
                          n_ctx_tiles, n_lat_tiles)
    o2 = _swa_call(swa_sink[0].astype(F32) * log2e, sq, sk, sv, n_ctx)

    return _outproj_ffn_call(o1, o2, h1, modsel(1), w_out1.astype(BF16), ffn_w1[1].astype(BF16),
                             ffn_w2[1].astype(BF16), row(ln_mix_g[1]), row(ln_mix_b[1]), row(ln_ffn_g[1]),
                             row(ln_ffn_b[1]), o_off=0, h_off=n_ctx_tiles, n_out_tiles=n_lat_tiles,
                             n_ctx_tiles=n_ctx_tiles, alpha=alpha, name="outproj_ffn_1")
```

```python
import functools
import math

import jax
import jax.numpy as jnp
from jax import lax
from jax.experimental import pallas as pl
from jax.experimental.pallas import tpu as pltpu

F32 = jnp.float32
BF16 = jnp.bfloat16

HEAD_DIM = 64
GRID_W = 64
ROPE_BASE = 10000.0
LN_EPS = 1e-5
RWKV_GN_EPS = 64e-5
RWKV_LORA_W = 64
RWKV_LORA_A = 64
RWKV_LORA_G = 128
DIFF_HEADS = 4
SSD_HEADS = 8
SSD_GROUPS = 2
SSD_STATE = 128
SSD_CONV = 5
SWA_HEADS = 8
SWA_KV_HEADS = 2
SWA_WINDOW = 128

ROW_TILE = 256
HALO = 8
RWKV_CHUNK = 64
SSD_CHUNK = 128
SWA_TILE = 128
LANES = 128
VMEM_LIMIT = 56 * 1024 * 1024


def _cparams(n_axes):
    return pltpu.CompilerParams(dimension_semantics=("arbitrary",) * n_axes, vmem_limit_bytes=VMEM_LIMIT)


def _dot(a, b):
    return jnp.dot(a, b, preferred_element_type=F32)


def _dot_nt(a, b):
    return lax.dot_general(a, b, (((1,), (1,)), ((), ())), preferred_element_type=F32)


def _split3(a):
    hi = a.astype(BF16)
    r1 = a - hi.astype(F32)
    mid = r1.astype(BF16)
    lo = (r1 - mid.astype(F32)).astype(BF16)
    return hi, mid, lo


def _dot_f32_by_exact(a, b_exact):
    hi, mid, lo = _split3(a)
    return _dot(hi, b_exact) + _dot(mid, b_exact) + _dot(lo, b_exact)


def _dot_exact_by_f32(a_exact, b):
    hi, mid, lo = _split3(b)
    return _dot(a_exact, hi) + _dot(a_exact, mid) + _dot(a_exact, lo)


def _dot_hilo(a, b):
    ah = a.astype(BF16)
    al = (a - ah.astype(F32)).astype(BF16)
    bh = b.astype(BF16)
    bl = (b - bh.astype(F32)).astype(BF16)
    return _dot(ah, bh) + _dot(ah, bl) + _dot(al, bh)


def _split_hilo(w):
    hi = w.astype(BF16)
    return jnp.stack([hi, (w - hi.astype(F32)).astype(BF16)])


def _dot_hilo_presplit(a, w_ref):
    ah = a.astype(BF16)
    al = (a - ah.astype(F32)).astype(BF16)
    return _dot(ah, w_ref[0]) + _dot(ah, w_ref[1]) + _dot(al, w_ref[0])


def _sigmoid(x):
    return 1.0 / (1.0 + jnp.exp(-x))


def _softplus(x):
    return jnp.maximum(x, 0.0) + jnp.log(1.0 + jnp.exp(-jnp.abs(x)))


def _layer_norm(y, g, b, eps):
    mu = jnp.mean(y, -1, keepdims=True)
    yc = y - mu
    var = jnp.mean(yc * yc, -1, keepdims=True)
    return yc * lax.rsqrt(var + eps) * g + b


def _mod_kernel(c_ref, w_ref, b_ref, o_ref):
    c = c_ref[...]
    o_ref[0] = _dot_hilo(c * _sigmoid(c), w_ref[0]) + b_ref[0]


def _mod_call(c_pad, mod_w, mod_b):
    depth, d, n = mod_w.shape
    rows = c_pad.shape[0]
    tn = 1536
    return pl.pallas_call(
        _mod_kernel,
        grid=(depth, n // tn),
        in_specs=[pl.BlockSpec((rows, d), lambda i, j: (0, 0)),
                  pl.BlockSpec((1, d, tn), lambda i, j: (i, 0, j)),
                  pl.BlockSpec((1, 1, tn), lambda i, j: (i, 0, j))],
        out_specs=pl.BlockSpec((1, rows, tn), lambda i, j: (i, 0, j)),
        out_shape=jax.ShapeDtypeStruct((depth, rows, n), F32),
        compiler_params=_cparams(2),
        name="adaln_mod",
    )(c_pad, mod_w, mod_b.reshape(depth, 1, n))


def _rope(x, cos, sin):
    lane = lax.broadcasted_iota(jnp.int32, (x.shape[0], LANES), 1)
    first = (lane % 32) < 16
    parts = []
    for g in range(0, x.shape[1], LANES):
        xg = x[:, g:g + LANES]
        parts.append(jnp.where(first, pltpu.roll(xg, LANES - 16, 1), pltpu.roll(xg, 16, 1)))
    sw = parts[0] if len(parts) == 1 else jnp.concatenate(parts, 1)
    return x * cos + sw * sin


def _stream_specs(streams, n_ctx_tiles):
    d = streams[0].shape[-1]
    if len(streams) == 1:
        return [pl.BlockSpec((1, ROW_TILE, d), lambda bi, t: (bi, t, 0))]
    return [pl.BlockSpec((1, ROW_TILE, d), lambda bi, t: (bi, jnp.minimum(t, n_ctx_tiles - 1), 0)),
            pl.BlockSpec((1, ROW_TILE, d), lambda bi, t: (bi, jnp.maximum(t - n_ctx_tiles, 0), 0))]


def _stream_tile(refs, n_ctx_tiles):
    if len(refs) == 1:
        return refs[0][0]
    return jnp.where(pl.program_id(1) < n_ctx_tiles, refs[0][0], refs[1][0])


def _inproj_kernel(*refs, plan, n_streams, n_ctx_tiles):
    x_refs = refs[:n_streams]
    sh_ref, sc_ref, w_ref, cos_ref, sin_ref = refs[n_streams:n_streams + 5]
    o_refs = refs[n_streams + 5:]
    xm = (_stream_tile(x_refs, n_ctx_tiles) * (1.0 + sc_ref[0, 0, 0]) + sh_ref[0, 0, 0]).astype(BF16)
    for o_ref, (c0, width, rope, mult, transpose) in zip(o_refs, plan):
        for j in range(0, width, 256):
            cw = min(256, width - j)
            acc = _dot(xm, w_ref[:, c0 + j:c0 + j + cw])
            if rope:
                acc = _rope(acc, cos_ref[:, :cw], sin_ref[:, :cw])
            if mult != 1.0:
                acc = acc * mult
            if transpose:
                o_ref[0, j:j + cw, :] = acc.T.astype(o_ref.dtype)
            else:
                o_ref[0, :, j:j + cw] = acc.astype(o_ref.dtype)


def _inproj_call(streams, modsel, w_bf16, cos_t, sin_t, plan, out_dtypes, n_ctx_tiles, name):
    b, _, d = streams[0].shape
    l = sum(s.shape[1] for s in streams)
    n_tiles = l // ROW_TILE
    seg = lambda bi, t: (bi, (t >= n_ctx_tiles).astype(jnp.int32), 0, 0, 0)
    seg_scale = lambda bi, t: (bi, (t >= n_ctx_tiles).astype(jnp.int32), 1, 0, 0)
    out_shape, out_specs = [], []
    for (_, width, _, _, transpose), dt in zip(plan, out_dtypes):
        if transpose:
            out_shape.append(jax.ShapeDtypeStruct((b, width, l), dt))
            out_specs.append(pl.BlockSpec((1, width, ROW_TILE), lambda bi, t: (bi, 0, t)))
        else:
            out_shape.append(jax.ShapeDtypeStruct((b, l, width), dt))
            out_specs.append(pl.BlockSpec((1, ROW_TILE, width), lambda bi, t: (bi, t, 0)))
    return pl.pallas_call(
        functools.partial(_inproj_kernel, plan=plan, n_streams=len(streams), n_ctx_tiles=n_ctx_tiles),
        grid=(b, n_tiles),
        in_specs=_stream_specs(streams, n_ctx_tiles) + [
            pl.BlockSpec((1, 1, 1, 1, d), seg),
            pl.BlockSpec((1, 1, 1, 1, d), seg_scale),
            pl.BlockSpec(w_bf16.shape, lambda bi, t: (0, 0), pipeline_mode=pl.Buffered(1)),
            pl.BlockSpec((ROW_TILE, cos_t.shape[1]), lambda bi, t: (t, 0)),
            pl.BlockSpec((ROW_TILE, sin_t.shape[1]), lambda bi, t: (t, 0))],
        out_specs=out_specs,
        out_shape=out_shape,
        compiler_params=_cparams(2),
        name=name,
    )(*streams, modsel, modsel, w_bf16, cos_t, sin_t)


def _rwkv_prep_kernel(p_ref, hp_ref, hn_ref, mu_ref, bias_ref, lora_ref, g2_ref, kk_ref, ka_ref, rk_ref, bd_ref,
                      at_o, bt_o, kt_o, rt_o, pe_o, v_o, g_o, bonus_o, *, n_ctx_tiles, n_tiles, chunk):
    t = pl.program_id(1)
    gw = v_o.shape[-1]
    p = p_ref[0]
    rows = p.shape[0]
    first = jnp.logical_or(t == 0, t == n_ctx_tiles)
    last = jnp.logical_or(t == n_ctx_tiles - 1, t == n_tiles - 1)
    prev_row = jnp.where(first, 0.0, hp_ref[0, HALO - 1:HALO, :])
    next_row = jnp.where(last, 0.0, hn_ref[0, 0:1, :])
    rowi = lax.broadcasted_iota(jnp.int32, p.shape, 0)
    prev = jnp.where(rowi == 0, prev_row, pltpu.roll(p, 1, 0))
    nxt = jnp.where(rowi == rows - 1, next_row, pltpu.roll(p, rows - 1, 0))
    ps = p + mu_ref[...] * (0.5 * (prev + nxt) - p)

    r = ps[:, :gw]
    k = ps[:, gw:2 * gw]
    v = ps[:, 2 * gw:3 * gw]
    slab = ps[:, 3 * gw:3 * gw + LANES]
    xg = ps[:, 3 * gw + LANES:3 * gw + 2 * LANES]
    lane = lax.broadcasted_iota(jnp.int32, slab.shape, 1)
    lora_in = jnp.where(lane < RWKV_LORA_W, jnp.tanh(slab), slab)
    pre = _dot_hilo_presplit(lora_in, lora_ref) + bias_ref[...]
    bd = bd_ref[...]
    g_o[0] = _dot_hilo_presplit(_sigmoid(xg), g2_ref)
    kk = k * kk_ref[...]
    ss = _dot_f32_by_exact(kk * kk, bd)
    kk = kk * lax.rsqrt(jnp.maximum(ss, 1e-24))
    bonus_o[0] = _dot_f32_by_exact(r * k * rk_ref[...], bd) * v
    v_o[0] = v.astype(v_o.dtype)
    ri = lax.broadcasted_iota(jnp.int32, (rows, rows), 0)
    ci = lax.broadcasted_iota(jnp.int32, (rows, rows), 1)
    same_chunk = (ri // chunk) == (ci // chunk)
    for d in range(2):
        w_log = -_softplus(-pre[:, d * gw:(d + 1) * gw]) - 0.5
        lw = -jnp.exp(w_log)
        a = _sigmoid(pre[:, (2 + d) * gw:(3 + d) * gw])
        before = (ci <= ri) if d == 0 else (ci >= ri)
        tri = jnp.logical_and(same_chunk, before).astype(BF16)
        cum = _dot_exact_by_f32(tri, lw)
        p_inv = jnp.exp(-cum)
        at_o[0, d] = (-kk * jnp.exp(cum - lw)).astype(at_o.dtype)
        bt_o[0, d] = (kk * a * p_inv).astype(bt_o.dtype)
        kt_o[0, d] = (k * (1.0 + (a - 1.0) * ka_ref[...]) * p_inv).astype(kt_o.dtype)
        rt_o[0, d] = (r * jnp.exp(cum)).astype(rt_o.dtype)
        for j in range(rows // chunk):
            last_row = j * chunk + (chunk - 1 if d == 0 else 0)
            pe_o[0, d, j] = jnp.exp(cum[last_row:last_row + 1, :])


def _rwkv_prep_call(p, mu, bias, lora_w, g2, k_k, k_a, r_k, bd, n_ctx_tiles):
    b, l, cols = p.shape
    gw = k_k.shape[-1]
    n_tiles = l // ROW_TILE
    hb = ROW_TILE // HALO
    n_hb = l // HALO
    cpt = ROW_TILE // RWKV_CHUNK
    full = lambda a: pl.BlockSpec(a.shape, lambda bi, t: (0,) * a.ndim)
    tok = pl.BlockSpec((1, ROW_TILE, gw), lambda bi, t: (bi, t, 0))
    tok2 = pl.BlockSpec((1, 2, ROW_TILE, gw), lambda bi, t: (bi, 0, t, 0))
    s1 = jax.ShapeDtypeStruct((b, l, gw), F32)
    s2 = jax.ShapeDtypeStruct((b, 2, l, gw), BF16)
    return pl.pallas_call(
        functools.partial(_rwkv_prep_kernel, n_ctx_tiles=n_ctx_tiles, n_tiles=n_tiles, chunk=RWKV_CHUNK),
        grid=(b, n_tiles),
        in_specs=[pl.BlockSpec((1, ROW_TILE, cols), lambda bi, t: (bi, t, 0)),
                  pl.BlockSpec((1, HALO, cols), lambda bi, t: (bi, jnp.maximum(t * hb - 1, 0), 0)),
                  pl.BlockSpec((1, HALO, cols), lambda bi, t: (bi, jnp.minimum((t + 1) * hb, n_hb - 1), 0)),
                  full(mu), full(bias), full(lora_w), full(g2), full(k_k), full(k_a), full(r_k), full(bd)],
        out_specs=[tok2, tok2, tok2, tok2,
                   pl.BlockSpec((1, 2, cpt, 1, gw), lambda bi, t: (bi, 0, t, 0, 0)), tok, tok, tok],
        out_shape=[s2, s2, s2, s2, jax.ShapeDtypeStruct((b, 2, l // RWKV_CHUNK, 1, gw), F32),
                   jax.ShapeDtypeStruct((b, l, gw), BF16), s1, s1],
        compiler_params=_cparams(2),
        name="rwkv_prep",
    )(p, p, p, mu, bias, lora_w, g2, k_k, k_a, r_k, bd)


def _rwkv_scan_kernel(atf_ref, btf_ref, ktf_ref, rtf_ref, pef_ref, vf_ref, atr_ref, btr_ref, ktr_ref, rtr_ref, per_ref,
                      vr_ref, yf_ref, yr_ref, s_ref):
    i = pl.program_id(1)
    c = vf_ref.shape[1]
    n_pairs = vf_ref.shape[2] // LANES
    c2 = 2 * c

    @pl.when(i == 0)
    def _():
        s_ref[...] = jnp.zeros_like(s_ref)

    row2 = lax.broadcasted_iota(jnp.int32, (c2, c2), 0)
    col2 = lax.broadcasted_iota(jnp.int32, (c2, c2), 1)
    same = (row2 // c) == (col2 // c)
    eye = (row2 == col2).astype(F32)
    low = lax.broadcasted_iota(jnp.int32, (c, LANES), 1) < HEAD_DIM

    def stack(x):
        xf = x.astype(F32)
        return jnp.concatenate([jnp.where(low, xf, 0.0), jnp.where(low, 0.0, xf)], 0)

    chains = []
    dirs = ((1, atf_ref, btf_ref, ktf_ref, rtf_ref, pef_ref, vf_ref, yf_ref),
            (-1, atr_ref, btr_ref, ktr_ref, rtr_ref, per_ref, vr_ref, yr_ref))
    for d, (sgn, at_ref, bt_ref, kt_ref, rt_ref, pe_ref, v_ref, y_ref) in enumerate(dirs):
        dt2 = (row2 - col2) * sgn
        strict = jnp.logical_and(same, dt2 > 0)
        incl = jnp.logical_and(same, dt2 >= 0)
        for pr in range(n_pairs):
            sl = slice(pr * LANES, (pr + 1) * LANES)
            b_s = bt_ref[0, 0, :, sl]
            k_s = kt_ref[0, 0, :, sl]
            chains.append(dict(d=d, pr=pr, sl=sl, y_ref=y_ref, strict=strict, incl=incl, p_end=pe_ref[0, 0, 0, :, sl],
                               vh=stack(v_ref[0, :, sl]),
                               ar=jnp.concatenate([stack(at_ref[0, 0, :, sl]), stack(rt_ref[0, 0, :, sl])],
                                                  0).astype(BF16),
                               bk=jnp.concatenate([b_s, b_s, k_s, k_s], 0)))

    for ch in chains:
        gram = _dot_nt(ch["ar"], ch["bk"])
        l_ab = jnp.where(ch["strict"], gram[:c2, :c2], 0.0)
        ch["l_ak"] = jnp.where(ch["strict"], gram[:c2, c2:], 0.0).astype(BF16)
        ch["l_r"] = jnp.concatenate([jnp.where(ch["incl"], gram[c2:, :c2], 0.0),
                                     jnp.where(ch["incl"], gram[c2:, c2:], 0.0)], 1).astype(BF16)
        ch["inv"] = eye + l_ab
        ch["pw"] = l_ab.astype(BF16)
    for ch in chains:
        ch["pw"] = _dot(ch["pw"], ch["pw"]).astype(BF16)
    for _ in range(max(0, int(math.ceil(math.log2(c))) - 2)):
        for ch in chains:
            both = _dot(jnp.concatenate([ch["inv"].astype(BF16), ch["pw"]], 0), ch["pw"])
            ch["inv"] = ch["inv"] + both[:c2]
            ch["pw"] = both[c2:].astype(BF16)
    for ch in chains:
        ch["inv"] = (ch["inv"] + _dot(ch["inv"].astype(BF16), ch["pw"])).astype(BF16)
    for ch in chains:
        ch["s0"] = s_ref[ch["d"], ch["pr"]]
        ch["ars"] = _dot_nt(ch["ar"], ch["s0"].astype(BF16))
        ch["lv"] = _dot(ch["l_ak"], ch["vh"].astype(BF16))
    for ch in chains:
        u = _dot(ch["inv"], (ch["ars"][:c2] + ch["lv"]).astype(BF16))
        ch["uv"] = jnp.concatenate([u, ch["vh"]], 0)
    for ch in chains:
        y = ch["ars"][c2:] + _dot(ch["l_r"], ch["uv"].astype(BF16))
        ch["y_ref"][0, :, ch["sl"]] = y[:c] + y[c:]
    for ch in chains:
        upd = (ch["s0"] + _dot(ch["uv"].T.astype(BF16), ch["bk"])) * ch["p_end"]
        s_ref[ch["d"], ch["pr"]] = jnp.where(same, upd, 0.0)


def _rwkv_scan_call(at, bt, kt, rt, pe, v, n_ctx):
    b, l, gw = v.shape
    c = RWKV_CHUNK
    nc = l // c
    ncc = n_ctx // c

    def rev(i):
        return jnp.where(i < ncc, ncc - 1 - i, nc - 1 - (i - ncc))

    def specs(d, ch):
        tokd = pl.BlockSpec((1, 1, c, gw), lambda bi, i: (bi, d, ch(i), 0))
        return [tokd, tokd, tokd, tokd, pl.BlockSpec((1, 1, 1, 1, gw), lambda bi, i: (bi, d, ch(i), 0, 0)),
                pl.BlockSpec((1, c, gw), lambda bi, i: (bi, ch(i), 0))]

    ys = jax.ShapeDtypeStruct((b, l, gw), F32)
    return pl.pallas_call(
        _rwkv_scan_kernel,
        grid=(b, nc),
        in_specs=specs(0, lambda i: i) + specs(1, rev),
        out_specs=[pl.BlockSpec((1, c, gw), lambda bi, i: (bi, i, 0)),
                   pl.BlockSpec((1, c, gw), lambda bi, i: (bi, rev(i), 0))],
        out_shape=[ys, ys],
        scratch_shapes=[pltpu.VMEM((2, gw // LANES, LANES, LANES), F32)],
        compiler_params=_cparams(2),
        name="rwkv_scan",
    )(at, bt, kt, rt, pe, v, at, bt, kt, rt, pe, v)


def _rwkv_finish_kernel(yf_ref, yr_ref, bonus_ref, g_ref, gg_ref, gb_ref, bd_ref, o_ref):
    y = yf_ref[0] + yr_ref[0]
    bd = bd_ref[...]
    inv_n = 1.0 / HEAD_DIM
    mu = _dot_f32_by_exact(y, bd) * inv_n
    yc = y - mu
    var = _dot_f32_by_exact(yc * yc, bd) * inv_n
    yn = yc * lax.rsqrt(var + RWKV_GN_EPS) * gg_ref[...] + gb_ref[...]
    o_ref[0] = (yn + bonus_ref[0]) * g_ref[0]


def _rwkv_finish_call(y_f, y_r, bonus, g, gn_g, gn_b, bd):
    b, l, gw = y_f.shape
    tok = pl.BlockSpec((1, ROW_TILE, gw), lambda bi, t: (bi, t, 0))
    full = lambda a: pl.BlockSpec(a.shape, lambda bi, t: (0,) * a.ndim)
    return pl.pallas_call(
        _rwkv_finish_kernel,
        grid=(b, l // ROW_TILE),
        in_specs=[tok, tok, tok, tok, full(gn_g), full(gn_b), full(bd)],
        out_specs=tok,
        out_shape=jax.ShapeDtypeStruct((b, l, gw), F32),
        compiler_params=_cparams(2),
        name="rwkv_finish",
    )(y_f, y_r, bonus, g, gn_g, gn_b, bd)


def _diff_attn_kernel(lam_ref, q_ref, k_ref, vt_ref, g_ref, o_ref, *, n_ctx, n_ctx_tiles, out_scale):
    t = pl.program_id(1)
    lam = lam_ref[0]
    vd = 2 * HEAD_DIM
    tq = q_ref.shape[1]
    low = lax.broadcasted_iota(jnp.int32, (tq, vd), 1) < HEAD_DIM

    def run(nk):
        def logits(u):
            h, m = divmod(u, 2)
            qf = q_ref[0, :, h * vd:(h + 1) * vd].astype(F32)
            qm = jnp.where(low == (m == 0), qf, 0.0).astype(BF16)
            return _dot_nt(k_ref[0, :nk, h * vd:(h + 1) * vd], qm)

        n_units = 2 * DIFF_HEADS
        s_next = logits(0)
        es, invs = [], []
        for u in range(n_units):
            s = s_next
            if u + 1 < n_units:
                s_next = logits(u + 1)
            e = jnp.exp2(s - jnp.max(s, 0, keepdims=True))
            invs.append(1.0 / jnp.sum(e, 0, keepdims=True))
            es.append(e.astype(BF16))
            if u % 2 == 1:
                h = u // 2
                pv = _dot(vt_ref[0, h * vd:(h + 1) * vd, :nk], jnp.concatenate(es, 1))
                o = pv[:, :tq] * invs[0] - pv[:, tq:] * (lam * invs[1])
                o = o * lax.rsqrt(jnp.mean(o * o, 0, keepdims=True) + LN_EPS) * (g_ref[...] * out_scale)
                o_ref[0, :, h * vd:(h + 1) * vd] = o.T
                es, invs = [], []

    @pl.when(t < n_ctx_tiles)
    def _():
        run(n_ctx)

    @pl.when(t >= n_ctx_tiles)
    def _():
        run(k_ref.shape[1])


def _diff_attn_call(lam, q, k, vt, subln_g_col, n_ctx, lam_init):
    b, l, w = q.shape
    tok = pl.BlockSpec((1, ROW_TILE, w), lambda bi, t: (bi, t, 0))
    return pl.pallas_call(
        functools.partial(_diff_attn_kernel, n_ctx=n_ctx, n_ctx_tiles=n_ctx // ROW_TILE, out_scale=1.0 - lam_init),
        grid=(b, l // ROW_TILE),
        in_specs=[pl.BlockSpec(memory_space=pltpu.SMEM), tok,
                  pl.BlockSpec((1, l, w), lambda bi, t: (bi, 0, 0)),
                  pl.BlockSpec((1, w, l), lambda bi, t: (bi, 0, 0)),
                  pl.BlockSpec(subln_g_col.shape, lambda bi, t: (0, 0))],
        out_specs=tok,
        out_shape=jax.ShapeDtypeStruct((b, l, w), F32),
        compiler_params=_cparams(2),
        name="diff_attn",
    )(lam, q, k, vt, subln_g_col)


def _outproj_ffn_kernel(*refs, alpha, n_streams, n_ctx_tiles):
    o1_ref, o2_ref = refs[:2]
    h_refs = refs[2:2 + n_streams]
    (gm_ref, sh_ref, sc_ref, gf_ref, wo_ref, w1_ref, w2_ref, lmg_ref, lmb_ref, lfg_ref, lfb_ref,
     out_ref) = refs[2 + n_streams:]
    gw = o1_ref.shape[-1]
    o = _dot(o1_ref[0].astype(BF16), wo_ref[:gw, :]) + _dot(o2_ref[0].astype(BF16), wo_ref[gw:, :])
    h_in = _stream_tile(h_refs, n_ctx_tiles)
    h1 = _layer_norm(alpha * h_in + gm_ref[0, 0, 0] * o, lmg_ref[...], lmb_ref[...], LN_EPS)
    xm = (h1 * (1.0 + sc_ref[0, 0, 0]) + sh_ref[0, 0, 0]).astype(BF16)
    hidden = w1_ref.shape[1]
    step = 512

    def up(j):
        hj = jnp.maximum(_dot(xm, w1_ref[:, j:j + step]), 0.0)
        return (hj * hj).astype(BF16)

    acc = None
    act = up(0)
    for j in range(0, hidden, step):
        nxt = up(j + step) if j + step < hidden else None
        part = _dot(act, w2_ref[j:j + step, :])
        acc = part if acc is None else acc + part
        act = nxt
    out_ref[0] = _layer_norm(alpha * h1 + gf_ref[0, 0, 0] * acc, lfg_ref[...], lfb_ref[...], LN_EPS)


def _outproj_ffn_call(o1, o2, streams, modsel, wo, w1, w2, lmg, lmb, lfg, lfb, *, h_off, n_out_tiles,
                      n_ctx_tiles, alpha, name):
    b, _, gw = o1.shape
    d = streams[0].shape[-1]
    assert len(streams) == 1 or h_off == 0
    seg = lambda col: (lambda bi, t: (bi, ((t + h_off) >= n_ctx_tiles).astype(jnp.int32), col, 0, 0))
    modspec = lambda col: pl.BlockSpec((1, 1, 1, 1, d), seg(col))
    full = lambda a: pl.BlockSpec(a.shape, lambda bi, t: (0,) * a.ndim, pipeline_mode=pl.Buffered(1))
    otok = pl.BlockSpec((1, ROW_TILE, gw), lambda bi, t: (bi, t, 0))
    if len(streams) == 1:
        h_specs = [pl.BlockSpec((1, ROW_TILE, d), lambda bi, t: (bi, t + h_off, 0))]
    else:
        h_specs = _stream_specs(streams, n_ctx_tiles)
    return pl.pallas_call(
        functools.partial(_outproj_ffn_kernel, alpha=alpha, n_streams=len(streams), n_ctx_tiles=n_ctx_tiles),
        grid=(b, n_out_tiles),
        in_specs=[otok, otok] + h_specs + [
            modspec(2), modspec(3), modspec(4), modspec(5),
            full(wo), full(w1), full(w2), full(lmg), full(lmb), full(lfg), full(lfb)],
        out_specs=pl.BlockSpec((1, ROW_TILE, d), lambda bi, t: (bi, t, 0)),
        out_shape=jax.ShapeDtypeStruct((b, n_out_tiles * ROW_TILE, d), F32),
        compiler_params=_cparams(2),
        name=name,
    )(o1, o2, *streams, modsel, modsel, modsel, modsel, wo, w1, w2, lmg, lmb, lfg, lfb)


def _ssd_prep_kernel(x_ref, hp_ref, hn_ref, dt_ref, cw_ref, cb_ref, dtb_ref, an_ref, xc_o, dt_o, adt_o,
                     *, n_ctx_tiles, n_tiles):
    t = pl.program_id(1)
    x = x_ref[0]
    rows = x.shape[0]
    first = jnp.logical_or(t == 0, t == n_ctx_tiles)
    last = jnp.logical_or(t == n_ctx_tiles - 1, t == n_tiles - 1)
    ext = jnp.concatenate([jnp.where(first, 0.0, hp_ref[0]), x, jnp.where(last, 0.0, hn_ref[0])], 0)
    n_ext = rows + 2 * HALO
    acc = cb_ref[...] + cw_ref[SSD_CONV // 2:SSD_CONV // 2 + 1, :] * x
    for j in range(SSD_CONV):
        s = j - SSD_CONV // 2
        if s == 0:
            continue
        shifted = pltpu.roll(ext, (n_ext - s) % n_ext, 0)[HALO:HALO + rows]
        acc = acc + cw_ref[j:j + 1, :] * shifted
    xc_o[0] = acc * _sigmoid(acc)
    dt = _softplus(dt_ref[0] + dtb_ref[...])
    dt_o[0] = dt
    adt_o[0] = dt * an_ref[...]


def _ssd_prep_call(xbc, dt, conv_w, conv_b, dt_bias, a_neg, n_ctx_tiles):
    b, l, cw = xbc.shape
    n_tiles = l // ROW_TILE
    hb = ROW_TILE // HALO
    n_hb = l // HALO
    full = lambda a: pl.BlockSpec(a.shape, lambda bi, t: (0,) * a.ndim)
    tokd = pl.BlockSpec((1, ROW_TILE, LANES), lambda bi, t: (bi, t, 0))
    sd = jax.ShapeDtypeStruct((b, l, LANES), F32)
    return pl.pallas_call(
        functools.partial(_ssd_prep_kernel, n_ctx_tiles=n_ctx_tiles, n_tiles=n_tiles),
        grid=(b, n_tiles),
        in_specs=[pl.BlockSpec((1, ROW_TILE, cw), lambda bi, t: (bi, t, 0)),
                  pl.BlockSpec((1, HALO, cw), lambda bi, t: (bi, jnp.maximum(t * hb - 1, 0), 0)),
                  pl.BlockSpec((1, HALO, cw), lambda bi, t: (bi, jnp.minimum((t + 1) * hb, n_hb - 1), 0)),
                  tokd, full(conv_w), full(conv_b), full(dt_bias), full(a_neg)],
        out_specs=[pl.BlockSpec((1, ROW_TILE, cw), lambda bi, t: (bi, t, 0)), tokd, tokd],
        out_shape=[jax.ShapeDtypeStruct((b, l, cw), F32), sd, sd],
        compiler_params=_cparams(2),
        name="ssd_prep",
    )(xbc, xbc, xbc, dt, conv_w, conv_b, dt_bias, a_neg)


def _ssd_scan_kernel(xcf_ref, dtf_ref, adtf_ref, adttf_ref, xcr_ref, dtr_ref, adtr_ref, adttr_ref, e_ref,
                     yf_ref, yr_ref, st_ref):
    i = pl.program_id(1)
    q = xcf_ref.shape[1]
    gw = yf_ref.shape[-1]
    n = SSD_STATE
    hpg = SSD_HEADS // SSD_GROUPS
    gcols = hpg * HEAD_DIM

    @pl.when(i == 0)
    def _():
        st_ref[...] = jnp.zeros_like(st_ref)

    row = lax.broadcasted_iota(jnp.int32, (q, q), 0)
    col = lax.broadcasted_iota(jnp.int32, (q, q), 1)
    chains = []
    dirs = ((1, xcf_ref, dtf_ref, adtf_ref, adttf_ref, yf_ref), (-1, xcr_ref, dtr_ref, adtr_ref, adttr_ref, yr_ref))
    for d, (sgn, xc_ref, dt_ref, adt_ref, adtt_ref, y_ref) in enumerate(dirs):
        incl = (row - col) * sgn >= 0
        tri = incl.astype(BF16)
        tri_t = ((col - row) * sgn >= 0).astype(BF16)
        cs = _dot_exact_by_f32(tri, adt_ref[0])
        cs_t = _dot_f32_by_exact(adtt_ref[0], tri_t)
        e = e_ref[d]
        cse = _dot_f32_by_exact(cs, e)
        dte = _dot_f32_by_exact(dt_ref[0], e)
        tot = cse[q - 1:q, :] if sgn > 0 else cse[0:1, :]
        xc = xc_ref[0]
        xdt = xc[:, :gw] * dte
        e_cs = jnp.exp(cse)
        x_end = (xdt * jnp.exp(tot - cse)).astype(BF16)
        e_tot = jnp.exp(tot)
        xdt16 = xdt.astype(BF16)
        for g in range(SSD_GROUPS):
            gsl = slice(g * gcols, (g + 1) * gcols)
            lms = []
            for hh in range(hpg):
                h = g * hpg + hh
                colv = cse[:, h * HEAD_DIM:h * HEAD_DIM + 1]
                rowv = cs_t[d * SSD_HEADS + h:d * SSD_HEADS + h + 1, :]
                lms.append(jnp.where(incl, jnp.exp(colv - rowv), 0.0))
            chains.append(dict(d=d, g=g, gsl=gsl, y_ref=y_ref, lms=lms, e_cs=e_cs[:, gsl], e_tot=e_tot[:, gsl],
                               x_end=x_end[:, gsl], xdt16=xdt16[:, gsl],
                               bg=xc[:, gw + g * n:gw + (g + 1) * n],
                               cg=xc[:, gw + (SSD_GROUPS + g) * n:gw + (SSD_GROUPS + g + 1) * n].astype(BF16)))
    for ch in chains:
        ch["cb"] = _dot_nt(ch["cg"], ch["bg"].astype(BF16))
        ch["st"] = st_ref[ch["d"], ch["g"]]
    for ch in chains:
        ch["y_off"] = _dot(ch["cg"], ch["st"].astype(BF16)) * ch["e_cs"]
    for ch in chains:
        ch["ys"] = [_dot((ch["cb"] * ch["lms"][hh]).astype(BF16), ch["xdt16"][:, hh * HEAD_DIM:(hh + 1) * HEAD_DIM])
                    for hh in range(hpg)]
    for ch in chains:
        ch["y_ref"][0, :, ch["gsl"]] = jnp.concatenate(ch["ys"], 1) + ch["y_off"]
        st_ref[ch["d"], ch["g"]] = ch["st"] * ch["e_tot"] + _dot(ch["bg"].T.astype(BF16), ch["x_end"])


def _ssd_scan_call(xc, dt, adt, adt_t, e_sel, n_ctx):
    b, l, cw = xc.shape
    gw = e_sel.shape[-1]
    q = SSD_CHUNK
    nc = l // q
    ncc = n_ctx // q

    def rev(i):
        return jnp.where(i < ncc, ncc - 1 - i, nc - 1 - (i - ncc))

    def specs(ch):
        tokd = pl.BlockSpec((1, q, LANES), lambda bi, i: (bi, ch(i), 0))
        return [pl.BlockSpec((1, q, cw), lambda bi, i: (bi, ch(i), 0)), tokd, tokd,
                pl.BlockSpec((1, 2 * SSD_HEADS, q), lambda bi, i: (bi, 0, ch(i)))]

    ys = jax.ShapeDtypeStruct((b, l, gw), F32)
    return pl.pallas_call(
        _ssd_scan_kernel,
        grid=(b, nc),
        in_specs=specs(lambda i: i) + specs(rev) + [pl.BlockSpec(e_sel.shape, lambda bi, i: (0, 0, 0))],
        out_specs=[pl.BlockSpec((1, q, gw), lambda bi, i: (bi, i, 0)),
                   pl.BlockSpec((1, q, gw), lambda bi, i: (bi, rev(i), 0))],
        out_shape=[ys, ys],
        scratch_shapes=[pltpu.VMEM((2, SSD_GROUPS, SSD_STATE, gw // SSD_GROUPS), F32)],
        compiler_params=_cparams(2),
        name="ssd_scan",
    )(xc, dt, adt, adt_t, xc, dt, adt, adt_t, e_sel)


def _ssd_finish_kernel(yf_ref, yr_ref, xc_ref, z_ref, dsk_ref, ng_ref, o_ref):
    gw = o_ref.shape[-1]
    z = z_ref[0]
    y = (yf_ref[0] + yr_ref[0] + dsk_ref[...] * xc_ref[0, :, :gw]) * (z * _sigmoid(z))
    gwid = gw // SSD_GROUPS
    for g in range(SSD_GROUPS):
        yg = y[:, g * gwid:(g + 1) * gwid]
        o_ref[0, :, g * gwid:(g + 1) * gwid] = (yg * lax.rsqrt(jnp.mean(yg * yg, -1, keepdims=True) + LN_EPS)
                                                * ng_ref[:, g * gwid:(g + 1) * gwid])


def _ssd_finish_call(y_f, y_r, xc, z, d_exp, norm_g, n_ctx_tiles, n_lat_tiles):
    b, l, gw = y_f.shape
    cw = xc.shape[-1]
    off = n_ctx_tiles
    full = lambda a: pl.BlockSpec(a.shape, lambda bi, t: (0,) * a.ndim)
    return pl.pallas_call(
        _ssd_finish_kernel,
        grid=(b, n_lat_tiles),
        in_specs=[pl.BlockSpec((1, ROW_TILE, gw), lambda bi, t: (bi, t + off, 0)),
                  pl.BlockSpec((1, ROW_TILE, gw), lambda bi, t: (bi, t + off, 0)),
                  pl.BlockSpec((1, ROW_TILE, cw), lambda bi, t: (bi, t + off, 0)),
                  pl.BlockSpec((1, ROW_TILE, gw), lambda bi, t: (bi, t + off, 0)),
                  full(d_exp), full(norm_g)],
        out_specs=pl.BlockSpec((1, ROW_TILE, gw), lambda bi, t: (bi, t, 0)),
        out_shape=jax.ShapeDtypeStruct((b, n_lat_tiles * ROW_TILE, gw), F32),
        compiler_params=_cparams(2),
        name="ssd_finish",
    )(y_f, y_r, xc, z, d_exp, norm_g)


def _swa_kernel(sink_ref, q_ref, k_ref, v_ref, o_ref, *, n_ctx, n_lat):
    i = pl.program_id(1)
    tq = q_ref.shape[1]
    span = tq + 2 * SWA_WINDOW
    nk = n_ctx + span
    group = SWA_HEADS // SWA_KV_HEADS
    start = jnp.clip(i * tq - SWA_WINDOW, 0, n_lat - span)
    w0 = pl.multiple_of(n_ctx + start, LANES)
    kpos = start + lax.broadcasted_iota(jnp.int32, (span, tq), 0)
    qpos = i * tq + lax.broadcasted_iota(jnp.int32, (span, tq), 1)
    bias = jnp.where(jnp.abs(qpos - kpos) <= SWA_WINDOW, 0.0, -1e30)
    bias = jnp.concatenate([jnp.zeros((n_ctx, tq), F32), bias], 0)
    bias = jnp.concatenate([bias] * group, 1)
    kcat = jnp.concatenate([k_ref[0, :n_ctx, :], k_ref[0, pl.ds(w0, span), :]], 0).astype(F32)
    vt = jnp.concatenate([v_ref[0, :n_ctx, :], v_ref[0, pl.ds(w0, span), :]], 0).astype(F32).T
    qs = jnp.concatenate([q_ref[0, :, g * LANES:(g + 1) * LANES] for g in range(group)], 0)
    k_low = lax.broadcasted_iota(jnp.int32, kcat.shape, 1) < HEAD_DIM
    v_low = lax.broadcasted_iota(jnp.int32, vt.shape, 0) < HEAD_DIM
    ss, es, rdens = [], [], []
    for kvh in range(SWA_KV_HEADS):
        km = jnp.where(k_low == (kvh == 0), kcat, 0.0).astype(BF16)
        ss.append(_dot_nt(km, qs) + bias)
    for kvh in range(SWA_KV_HEADS):
        sink = jnp.concatenate([jnp.full((1, tq), sink_ref[kvh * group + g], F32) for g in range(group)], 1)
        m = jnp.maximum(jnp.max(ss[kvh], 0, keepdims=True), sink)
        e = jnp.exp2(ss[kvh] - m)
        rdens.append(1.0 / (jnp.exp2(sink - m) + jnp.sum(e, 0, keepdims=True)))
        es.append(e.astype(BF16))
    ot = (_dot(jnp.where(v_low, vt, 0.0).astype(BF16), es[0]) * rdens[0]
          + _dot(jnp.where(v_low, 0.0, vt).astype(BF16), es[1]) * rdens[1])
    for g in range(group):
        o_ref[0, :, g * LANES:(g + 1) * LANES] = ot[:, g * tq:(g + 1) * tq].T


def _swa_call(sink, q, k, v, n_ctx):
    b, l, qw = q.shape
    kw = k.shape[-1]
    n_lat = l - n_ctx
    off = n_ctx // SWA_TILE
    kv = pl.BlockSpec((1, l, kw), lambda bi, t: (bi, 0, 0))
    return pl.pallas_call(
        functools.partial(_swa_kernel, n_ctx=n_ctx, n_lat=n_lat),
        grid=(b, n_lat // SWA_TILE),
        in_specs=[pl.BlockSpec(memory_space=pltpu.SMEM),
                  pl.BlockSpec((1, SWA_TILE, qw), lambda bi, t: (bi, t + off, 0)), kv, kv],
        out_specs=pl.BlockSpec((1, SWA_TILE, qw), lambda bi, t: (bi, t, 0)),
        out_shape=jax.ShapeDtypeStruct((b, n_lat, qw), F32),
        compiler_params=_cparams(2),
        name="swa",
    )(sink, q, k, v)


def _rope_tables(n_ctx, n_lat, width):
    half = HEAD_DIM // 4
    inv = ROPE_BASE ** (-jnp.arange(half, dtype=F32) / half)
    pos = jnp.arange(n_lat, dtype=jnp.int32)
    rows = (pos // GRID_W).astype(F32)[:, None] * inv
    cols = (pos % GRID_W).astype(F32)[:, None] * inv
    cos = jnp.concatenate([jnp.cos(rows), jnp.cos(rows), jnp.cos(cols), jnp.cos(cols)], -1)
    sin = jnp.concatenate([-jnp.sin(rows), jnp.sin(rows), -jnp.sin(cols), jnp.sin(cols)], -1)
    cos = jnp.concatenate([jnp.ones((n_ctx, HEAD_DIM), F32), cos], 0)
    sin = jnp.concatenate([jnp.zeros((n_ctx, HEAD_DIM), F32), sin], 0)
    reps = width // HEAD_DIM
    return jnp.tile(cos, (1, reps)), jnp.tile(sin, (1, reps))


def kernel(x, c, ctx, c_ctx, mod_w, mod_b, ln_mix_g, ln_mix_b, ln_ffn_g, ln_ffn_b, ffn_w1, ffn_w2, w_out, ab_w_in, rwkv_mu, rwkv_w0, rwkv_w2, rwkv_a0, rwkv_a2, rwkv_g2, rwkv_k_k, rwkv_k_a, rwkv_r_k, rwkv_gn_g, rwkv_gn_b, diff_lq1, diff_lk1, diff_lq2, diff_lk2, diff_subln_g, cd_w_in, ssd_conv_w, ssd_conv_b, ssd_dt_bias, ssd_a_log, ssd_d, ssd_norm_g, swa_sink):
    b, n_lat, d = x.shape
    n_ctx = ctx.shape[1]
    depth = mod_w.shape[0]
    assert depth == 2 and n_ctx % ROW_TILE == 0 and n_lat % ROW_TILE == 0 and n_lat % GRID_W == 0
    assert n_lat >= SWA_TILE + 2 * SWA_WINDOW
    gw = d // 2
    n_ctx_tiles = n_ctx // ROW_TILE
    n_lat_tiles = n_lat // ROW_TILE
    alpha = (2 * depth) ** 0.25
    log2e = math.log2(math.e)
    q_scale = HEAD_DIM ** -0.5 * log2e

    rows_pad = -(-(b + 1) // 8) * 8
    c_pad = jnp.zeros((rows_pad, d), F32).at[:b].set(c).at[b].set(c_ctx)
    m = _mod_call(c_pad, mod_w, mod_b)

    def modsel(i):
        return jnp.stack([jnp.broadcast_to(m[i, b], (b, 6 * d)), m[i, :b]], 1).reshape(b, 2, 6, 1, d)

    cos_t, sin_t = _rope_tables(n_ctx, n_lat, 256)
    h0 = (ctx, x)

    rwkv_cols = 3 * gw + RWKV_LORA_W + RWKV_LORA_A + RWKV_LORA_G
    plan0 = ((0, rwkv_cols, False, 1.0, False), (rwkv_cols, gw, True, q_scale, False),
             (rwkv_cols + gw, gw, True, 1.0, False), (rwkv_cols + 2 * gw, gw, False, 1.0, True))
    p_rwkv, dq, dk, dvt = _inproj_call(h0, modsel(0), ab_w_in[0].astype(BF16), cos_t, sin_t, plan0,
                                       (F32, BF16, BF16, BF16), n_ctx_tiles, "inproj_ab")

    zw = jnp.zeros((RWKV_LORA_W, gw), F32)
    lora_w = jnp.concatenate([jnp.concatenate([rwkv_w2[0, 0], rwkv_w2[0, 1], zw, zw], 1),
                              jnp.concatenate([zw, zw, rwkv_a2[0, 0], rwkv_a2[0, 1]], 1)], 0)
    lora_b = jnp.concatenate([rwkv_w0[0, 0], rwkv_w0[0, 1], rwkv_a0[0, 0], rwkv_a0[0, 1]])[None, :]
    head_id = jnp.arange(gw, dtype=jnp.int32) // HEAD_DIM
    bd = (head_id[:, None] == head_id[None, :]).astype(BF16)
    at, bt, kt, rt, pe, rv, g, bonus = _rwkv_prep_call(p_rwkv, rwkv_mu[0][None, :], lora_b, _split_hilo(lora_w), _split_hilo(rwkv_g2[0]),
                                                       rwkv_k_k[0][None, :], rwkv_k_a[0][None, :],
                                                       rwkv_r_k[0].reshape(1, gw), bd, n_ctx_tiles)
    y_f, y_r = _rwkv_scan_call(at, bt, kt, rt, pe, rv, n_ctx)
    o1 = _rwkv_finish_call(y_f, y_r, bonus, g, rwkv_gn_g[0][None, :], rwkv_gn_b[0][None, :], bd)

    lam_init0 = 0.8 - 0.6 * math.exp(-0.3 * 0)
    lam = (jnp.exp(jnp.sum(diff_lq1[0] * diff_lk1[0])) - jnp.exp(jnp.sum(diff_lq2[0] * diff_lk2[0]))
           + lam_init0).reshape(1).astype(F32)
    o2 = _diff_attn_call(lam, dq, dk, dvt, diff_subln_g[0][:, None], n_ctx, lam_init0)

    row = lambda t: t[None, :]
    h1 = _outproj_ffn_call(o1, o2, h0, modsel(0), w_out[0].astype(BF16), ffn_w1[0].astype(BF16),
                           ffn_w2[0].astype(BF16), row(ln_mix_g[0]), row(ln_mix_b[0]), row(ln_ffn_g[0]),
                           row(ln_ffn_b[0]), h_off=0, n_out_tiles=n_ctx_tiles + n_lat_tiles,
                           n_ctx_tiles=n_ctx_tiles, alpha=alpha, name="outproj_ffn_0")

    conv_dim = gw + 2 * SSD_GROUPS * SSD_STATE
    ssd_cols = gw + conv_dim + 2 * SSD_HEADS
    kvw = SWA_KV_HEADS * HEAD_DIM
    wcd = cd_w_in[0]
    group = SWA_HEADS // SWA_KV_HEADS
    q_perm = jnp.arange(gw, dtype=jnp.int32).reshape(SWA_KV_HEADS, group, HEAD_DIM).transpose(1, 0, 2).reshape(-1)
    w1cat = jnp.concatenate([wcd[:, ssd_cols:ssd_cols + gw][:, q_perm], wcd[:, ssd_cols + gw:], wcd[:, :gw],
                             wcd[:, gw:gw + conv_dim], wcd[:, gw + conv_dim:ssd_cols],
                             jnp.zeros((d, LANES - 2 * SSD_HEADS), F32)], 1)
    w_out1 = jnp.concatenate([w_out[1, :gw], w_out[1, gw:][q_perm]], 0)
    c_q, c_k, c_v, c_z, c_x, c_dt = 0, gw, gw + kvw, gw + 2 * kvw, 2 * gw + 2 * kvw, 2 * gw + 2 * kvw + conv_dim
    plan1 = ((c_q, gw, True, q_scale, False), (c_k, kvw, True, 1.0, False), (c_v, kvw, False, 1.0, False),
             (c_z, gw, False, 1.0, False), (c_x, conv_dim, False, 1.0, False), (c_dt, LANES, False, 1.0, False))
    sq, sk, sv, pz, xbc, dt_raw = _inproj_call((h1,), modsel(1), w1cat.astype(BF16), cos_t, sin_t, plan1,
                                               (BF16, BF16, BF16, F32, F32, F32), n_ctx_tiles, "inproj_cd")

    pad_l = lambda t: jnp.concatenate([t, jnp.zeros((LANES - t.shape[0],), F32)])[None, :]
    a_neg = -jnp.exp(ssd_a_log[0].astype(F32))
    conv_w_pad = jnp.concatenate([ssd_conv_w[0], jnp.zeros((8 - SSD_CONV, conv_dim), F32)], 0)
    xc, dt_sp, adt = _ssd_prep_call(xbc, dt_raw, conv_w_pad, ssd_conv_b[0][None, :], pad_l(ssd_dt_bias[0].reshape(-1)),
                                    pad_l(a_neg.reshape(-1)), n_ctx_tiles)
    adt_t = jnp.swapaxes(adt[:, :, :2 * SSD_HEADS], 1, 2)
    jj = jnp.arange(LANES, dtype=jnp.int32)[:, None]
    hh = (jnp.arange(gw, dtype=jnp.int32) // HEAD_DIM)[None, :]
    e_sel = jnp.stack([(jj == hh), (jj == hh + SSD_HEADS)]).astype(BF16)
    ys_f, ys_r = _ssd_scan_call(xc, dt_sp, adt, adt_t, e_sel, n_ctx)
    o1 = _ssd_finish_call(ys_f, ys_r, xc, pz, jnp.repeat(ssd_d[0], HEAD_DIM)[None, :], ssd_norm_g[0][None, :],
                          n_ctx_tiles, n_lat_tiles)
    o2 = _swa_call(swa_sink[0].astype(F32) * log2e, sq, sk, sv, n_ctx)

    return _outproj_ffn_call(o1, o2, (h1,), modsel(1), w_out1.astype(BF16), ffn_w1[1].astype(BF16),
                             ffn_w2[1].astype(BF16), row(ln_mix_g[1]), row(ln_mix_b[1]), row(ln_ffn_g[1]),
                             row(ln_ffn_b[1]), h_off=n_ctx_tiles, n_out_tiles=n_lat_tiles,
                             n_ctx_tiles=n_ctx_tiles, alpha=alpha, name="outproj_ffn_1")
```

```python
import functools
import math

import jax
import jax.numpy as jnp
from jax import lax
from jax.experimental import pallas as pl
from jax.experimental.pallas import tpu as pltpu

F32 = jnp.float32
BF16 = jnp.bfloat16

HEAD_DIM = 64
GRID_W = 64
ROPE_BASE = 10000.0
LN_EPS = 1e-5
RWKV_GN_EPS = 64e-5
RWKV_LORA_W = 64
RWKV_LORA_A = 64
RWKV_LORA_G = 128
DIFF_HEADS = 4
SSD_HEADS = 8
SSD_GROUPS = 2
SSD_STATE = 128
SSD_CONV = 5
SWA_HEADS = 8
SWA_KV_HEADS = 2
SWA_WINDOW = 128

ROW_TILE = 256
BATCH_PACK = 2
HALO = 8
RWKV_CHUNK = 64
SSD_CHUNK = 128
SWA_TILE = 128
LANES = 128
VMEM_LIMIT = 56 * 1024 * 1024


def _cparams(n_axes):
    return pltpu.CompilerParams(dimension_semantics=("arbitrary",) * n_axes, vmem_limit_bytes=VMEM_LIMIT)


def _dot(a, b):
    return jnp.dot(a, b, preferred_element_type=F32)


def _dot_nt(a, b):
    return lax.dot_general(a, b, (((1,), (1,)), ((), ())), preferred_element_type=F32)


def _split3(a):
    hi = a.astype(BF16)
    r1 = a - hi.astype(F32)
    mid = r1.astype(BF16)
    lo = (r1 - mid.astype(F32)).astype(BF16)
    return hi, mid, lo


def _dot_f32_by_exact(a, b_exact):
    hi, mid, lo = _split3(a)
    return _dot(hi, b_exact) + _dot(mid, b_exact) + _dot(lo, b_exact)


def _dot_exact_by_f32(a_exact, b):
    hi, mid, lo = _split3(b)
    return _dot(a_exact, hi) + _dot(a_exact, mid) + _dot(a_exact, lo)


def _dot_hilo(a, b):
    ah = a.astype(BF16)
    al = (a - ah.astype(F32)).astype(BF16)
    bh = b.astype(BF16)
    bl = (b - bh.astype(F32)).astype(BF16)
    return _dot(ah, bh) + _dot(ah, bl) + _dot(al, bh)


def _split_hilo(w):
    hi = w.astype(BF16)
    return jnp.stack([hi, (w - hi.astype(F32)).astype(BF16)])


def _dot_hilo_presplit(a, w_ref):
    ah = a.astype(BF16)
    al = (a - ah.astype(F32)).astype(BF16)
    return _dot(ah, w_ref[0]) + _dot(ah, w_ref[1]) + _dot(al, w_ref[0])


def _sigmoid(x):
    return 1.0 / (1.0 + jnp.exp(-x))


def _softplus(x):
    return jnp.maximum(x, 0.0) + jnp.log(1.0 + jnp.exp(-jnp.abs(x)))


def _layer_norm(y, g, b, eps):
    mu = jnp.mean(y, -1, keepdims=True)
    yc = y - mu
    var = jnp.mean(yc * yc, -1, keepdims=True)
    return yc * lax.rsqrt(var + eps) * g + b


def _mod_kernel(c_ref, w_ref, b_ref, o_ref):
    c = c_ref[...]
    o_ref[0] = _dot_hilo(c * _sigmoid(c), w_ref[0]) + b_ref[0]


def _mod_call(c_pad, mod_w, mod_b):
    depth, d, n = mod_w.shape
    rows = c_pad.shape[0]
    tn = 1536
    return pl.pallas_call(
        _mod_kernel,
        grid=(depth, n // tn),
        in_specs=[pl.BlockSpec((rows, d), lambda i, j: (0, 0)),
                  pl.BlockSpec((1, d, tn), lambda i, j: (i, 0, j)),
                  pl.BlockSpec((1, 1, tn), lambda i, j: (i, 0, j))],
        out_specs=pl.BlockSpec((1, rows, tn), lambda i, j: (i, 0, j)),
        out_shape=jax.ShapeDtypeStruct((depth, rows, n), F32),
        compiler_params=_cparams(2),
        name="adaln_mod",
    )(c_pad, mod_w, mod_b.reshape(depth, 1, n))


def _rope(x, cos, sin):
    lane = lax.broadcasted_iota(jnp.int32, (x.shape[0], LANES), 1)
    first = (lane % 32) < 16
    parts = []
    for g in range(0, x.shape[1], LANES):
        xg = x[:, g:g + LANES]
        parts.append(jnp.where(first, pltpu.roll(xg, LANES - 16, 1), pltpu.roll(xg, 16, 1)))
    sw = parts[0] if len(parts) == 1 else jnp.concatenate(parts, 1)
    return x * cos + sw * sin


def _stream_specs(streams, n_ctx_tiles):
    d = streams[0].shape[-1]
    bp = BATCH_PACK
    if len(streams) == 1:
        return [pl.BlockSpec((bp, ROW_TILE, d), lambda bi, t: (bi, t, 0))]
    return [pl.BlockSpec((bp, ROW_TILE, d), lambda bi, t: (bi, jnp.minimum(t, n_ctx_tiles - 1), 0)),
            pl.BlockSpec((bp, ROW_TILE, d), lambda bi, t: (bi, jnp.maximum(t - n_ctx_tiles, 0), 0))]


def _stream_tile(refs, n_ctx_tiles):
    if len(refs) == 1:
        return refs[0][...]
    return jnp.where(pl.program_id(1) < n_ctx_tiles, refs[0][...], refs[1][...])


def _inproj_kernel(*refs, plan, n_streams, n_ctx_tiles):
    x_refs = refs[:n_streams]
    sh_ref, sc_ref, w_ref, cos_ref, sin_ref = refs[n_streams:n_streams + 5]
    o_refs = refs[n_streams + 5:]
    x = _stream_tile(x_refs, n_ctx_tiles)
    bp, rows, d = x.shape
    xm = (x * (1.0 + sc_ref[:, 0, 0]) + sh_ref[:, 0, 0]).reshape(bp * rows, d).astype(BF16)
    cos = jnp.concatenate([cos_ref[...]] * bp, 0)
    sin = jnp.concatenate([sin_ref[...]] * bp, 0)
    for o_ref, (c0, width, rope, mult, transpose) in zip(o_refs, plan):
        for j in range(0, width, 256):
            cw = min(256, width - j)
            acc = _dot(xm, w_ref[:, c0 + j:c0 + j + cw])
            if rope:
                acc = _rope(acc, cos[:, :cw], sin[:, :cw])
            if mult != 1.0:
                acc = acc * mult
            if transpose:
                for p in range(bp):
                    o_ref[p, j:j + cw, :] = acc[p * rows:(p + 1) * rows].T.astype(o_ref.dtype)
            else:
                o_ref[:, :, j:j + cw] = acc.reshape(bp, rows, cw).astype(o_ref.dtype)


def _inproj_call(streams, modsel, w_bf16, cos_t, sin_t, plan, out_dtypes, n_ctx_tiles, name):
    b, _, d = streams[0].shape
    l = sum(s.shape[1] for s in streams)
    n_tiles = l // ROW_TILE
    seg = lambda bi, t: (bi, (t >= n_ctx_tiles).astype(jnp.int32), 0, 0, 0)
    seg_scale = lambda bi, t: (bi, (t >= n_ctx_tiles).astype(jnp.int32), 1, 0, 0)
    out_shape, out_specs = [], []
    bp = BATCH_PACK
    for (_, width, _, _, transpose), dt in zip(plan, out_dtypes):
        if transpose:
            out_shape.append(jax.ShapeDtypeStruct((b, width, l), dt))
            out_specs.append(pl.BlockSpec((bp, width, ROW_TILE), lambda bi, t: (bi, 0, t)))
        else:
            out_shape.append(jax.ShapeDtypeStruct((b, l, width), dt))
            out_specs.append(pl.BlockSpec((bp, ROW_TILE, width), lambda bi, t: (bi, t, 0)))
    return pl.pallas_call(
        functools.partial(_inproj_kernel, plan=plan, n_streams=len(streams), n_ctx_tiles=n_ctx_tiles),
        grid=(b // bp, n_tiles),
        in_specs=_stream_specs(streams, n_ctx_tiles) + [
            pl.BlockSpec((bp, 1, 1, 1, d), seg),
            pl.BlockSpec((bp, 1, 1, 1, d), seg_scale),
            pl.BlockSpec(w_bf16.shape, lambda bi, t: (0, 0), pipeline_mode=pl.Buffered(1)),
            pl.BlockSpec((ROW_TILE, cos_t.shape[1]), lambda bi, t: (t, 0)),
            pl.BlockSpec((ROW_TILE, sin_t.shape[1]), lambda bi, t: (t, 0))],
        out_specs=out_specs,
        out_shape=out_shape,
        compiler_params=_cparams(2),
        name=name,
    )(*streams, modsel, modsel, w_bf16, cos_t, sin_t)


def _rwkv_prep_kernel(p_ref, hp_ref, hn_ref, mu_ref, bias_ref, lora_ref, g2_ref, kk_ref, ka_ref, rk_ref, bd_ref,
                      at_o, bt_o, kt_o, rt_o, pe_o, v_o, g_o, bonus_o, *, n_ctx_tiles, n_tiles, chunk):
    t = pl.program_id(1)
    gw = v_o.shape[-1]
    p = p_ref[0]
    rows = p.shape[0]
    first = jnp.logical_or(t == 0, t == n_ctx_tiles)
    last = jnp.logical_or(t == n_ctx_tiles - 1, t == n_tiles - 1)
    prev_row = jnp.where(first, 0.0, hp_ref[0, HALO - 1:HALO, :])
    next_row = jnp.where(last, 0.0, hn_ref[0, 0:1, :])
    rowi = lax.broadcasted_iota(jnp.int32, p.shape, 0)
    prev = jnp.where(rowi == 0, prev_row, pltpu.roll(p, 1, 0))
    nxt = jnp.where(rowi == rows - 1, next_row, pltpu.roll(p, rows - 1, 0))
    ps = p + mu_ref[...] * (0.5 * (prev + nxt) - p)

    r = ps[:, :gw]
    k = ps[:, gw:2 * gw]
    v = ps[:, 2 * gw:3 * gw]
    slab = ps[:, 3 * gw:3 * gw + LANES]
    xg = ps[:, 3 * gw + LANES:3 * gw + 2 * LANES]
    lane = lax.broadcasted_iota(jnp.int32, slab.shape, 1)
    lora_in = jnp.where(lane < RWKV_LORA_W, jnp.tanh(slab), slab)
    pre = _dot_hilo_presplit(lora_in, lora_ref) + bias_ref[...]
    bd = bd_ref[...]
    g_o[0] = _dot_hilo_presplit(_sigmoid(xg), g2_ref)
    kk = k * kk_ref[...]
    ss = _dot_f32_by_exact(kk * kk, bd)
    kk = kk * lax.rsqrt(jnp.maximum(ss, 1e-24))
    bonus_o[0] = _dot_f32_by_exact(r * k * rk_ref[...], bd) * v
    v_o[0] = v.astype(v_o.dtype)
    ri = lax.broadcasted_iota(jnp.int32, (rows, rows), 0)
    ci = lax.broadcasted_iota(jnp.int32, (rows, rows), 1)
    same_chunk = (ri // chunk) == (ci // chunk)
    for d in range(2):
        w_log = -_softplus(-pre[:, d * gw:(d + 1) * gw]) - 0.5
        lw = -jnp.exp(w_log)
        a = _sigmoid(pre[:, (2 + d) * gw:(3 + d) * gw])
        before = (ci <= ri) if d == 0 else (ci >= ri)
        tri = jnp.logical_and(same_chunk, before).astype(BF16)
        cum = _dot_exact_by_f32(tri, lw)
        p_inv = jnp.exp(-cum)
        at_o[0, d] = (-kk * jnp.exp(cum - lw)).astype(at_o.dtype)
        bt_o[0, d] = (kk * a * p_inv).astype(bt_o.dtype)
        kt_o[0, d] = (k * (1.0 + (a - 1.0) * ka_ref[...]) * p_inv).astype(kt_o.dtype)
        rt_o[0, d] = (r * jnp.exp(cum)).astype(rt_o.dtype)
        for j in range(rows // chunk):
            last_row = j * chunk + (chunk - 1 if d == 0 else 0)
            pe_o[0, d, j] = jnp.exp(cum[last_row:last_row + 1, :])


def _rwkv_prep_call(p, mu, bias, lora_w, g2, k_k, k_a, r_k, bd, n_ctx_tiles):
    b, l, cols = p.shape
    gw = k_k.shape[-1]
    n_tiles = l // ROW_TILE
    hb = ROW_TILE // HALO
    n_hb = l // HALO
    cpt = ROW_TILE // RWKV_CHUNK
    full = lambda a: pl.BlockSpec(a.shape, lambda bi, t: (0,) * a.ndim)
    tok = pl.BlockSpec((1, ROW_TILE, gw), lambda bi, t: (bi, t, 0))
    tok2 = pl.BlockSpec((1, 2, ROW_TILE, gw), lambda bi, t: (bi, 0, t, 0))
    s1 = jax.ShapeDtypeStruct((b, l, gw), F32)
    s2 = jax.ShapeDtypeStruct((b, 2, l, gw), BF16)
    return pl.pallas_call(
        functools.partial(_rwkv_prep_kernel, n_ctx_tiles=n_ctx_tiles, n_tiles=n_tiles, chunk=RWKV_CHUNK),
        grid=(b, n_tiles),
        in_specs=[pl.BlockSpec((1, ROW_TILE, cols), lambda bi, t: (bi, t, 0)),
                  pl.BlockSpec((1, HALO, cols), lambda bi, t: (bi, jnp.maximum(t * hb - 1, 0), 0)),
                  pl.BlockSpec((1, HALO, cols), lambda bi, t: (bi, jnp.minimum((t + 1) * hb, n_hb - 1), 0)),
                  full(mu), full(bias), full(lora_w), full(g2), full(k_k), full(k_a), full(r_k), full(bd)],
        out_specs=[tok2, tok2, tok2, tok2,
                   pl.BlockSpec((1, 2, cpt, 1, gw), lambda bi, t: (bi, 0, t, 0, 0)), tok, tok, tok],
        out_shape=[s2, s2, s2, s2, jax.ShapeDtypeStruct((b, 2, l // RWKV_CHUNK, 1, gw), F32),
                   jax.ShapeDtypeStruct((b, l, gw), BF16), s1, s1],
        compiler_params=_cparams(2),
        name="rwkv_prep",
    )(p, p, p, mu, bias, lora_w, g2, k_k, k_a, r_k, bd)


def _rwkv_scan_kernel(atf_ref, btf_ref, ktf_ref, rtf_ref, pef_ref, vf_ref, atr_ref, btr_ref, ktr_ref, rtr_ref, per_ref,
                      vr_ref, yf_ref, yr_ref, s_ref):
    i = pl.program_id(1)
    c = vf_ref.shape[1]
    n_pairs = vf_ref.shape[2] // LANES
    c2 = 2 * c

    @pl.when(i == 0)
    def _():
        s_ref[...] = jnp.zeros_like(s_ref)

    row2 = lax.broadcasted_iota(jnp.int32, (c2, c2), 0)
    col2 = lax.broadcasted_iota(jnp.int32, (c2, c2), 1)
    same = (row2 // c) == (col2 // c)
    eye = (row2 == col2).astype(F32)
    low = lax.broadcasted_iota(jnp.int32, (c, LANES), 1) < HEAD_DIM

    def stack(x):
        xf = x.astype(F32)
        return jnp.concatenate([jnp.where(low, xf, 0.0), jnp.where(low, 0.0, xf)], 0)

    chains = []
    dirs = ((1, atf_ref, btf_ref, ktf_ref, rtf_ref, pef_ref, vf_ref, yf_ref),
            (-1, atr_ref, btr_ref, ktr_ref, rtr_ref, per_ref, vr_ref, yr_ref))
    for d, (sgn, at_ref, bt_ref, kt_ref, rt_ref, pe_ref, v_ref, y_ref) in enumerate(dirs):
        dt2 = (row2 - col2) * sgn
        strict = jnp.logical_and(same, dt2 > 0)
        incl = jnp.logical_and(same, dt2 >= 0)
        for pr in range(n_pairs):
            sl = slice(pr * LANES, (pr + 1) * LANES)
            b_s = bt_ref[0, 0, :, sl]
            k_s = kt_ref[0, 0, :, sl]
            chains.append(dict(d=d, pr=pr, sl=sl, y_ref=y_ref, strict=strict, incl=incl, p_end=pe_ref[0, 0, 0, :, sl],
                               vh=stack(v_ref[0, :, sl]),
                               ar=jnp.concatenate([stack(at_ref[0, 0, :, sl]), stack(rt_ref[0, 0, :, sl])],
                                                  0).astype(BF16),
                               bk=jnp.concatenate([b_s, b_s, k_s, k_s], 0)))

    for ch in chains:
        gram = _dot_nt(ch["ar"], ch["bk"])
        l_ab = jnp.where(ch["strict"], gram[:c2, :c2], 0.0)
        ch["l_ak"] = jnp.where(ch["strict"], gram[:c2, c2:], 0.0).astype(BF16)
        ch["l_r"] = jnp.concatenate([jnp.where(ch["incl"], gram[c2:, :c2], 0.0),
                                     jnp.where(ch["incl"], gram[c2:, c2:], 0.0)], 1).astype(BF16)
        ch["inv"] = eye + l_ab
        ch["pw"] = l_ab.astype(BF16)
    for ch in chains:
        ch["pw"] = _dot(ch["pw"], ch["pw"]).astype(BF16)
    for _ in range(max(0, int(math.ceil(math.log2(c))) - 2)):
        for ch in chains:
            both = _dot(ch["pw"], jnp.concatenate([ch["inv"].astype(BF16), ch["pw"]], 1))
            ch["inv"] = ch["inv"] + both[:, :c2]
            ch["pw"] = both[:, c2:].astype(BF16)
    for ch in chains:
        ch["inv"] = (ch["inv"] + _dot(ch["pw"], ch["inv"].astype(BF16))).astype(BF16)
    for ch in chains:
        ch["s0"] = s_ref[ch["d"], ch["pr"]]
        ch["ars"] = _dot_nt(ch["ar"], ch["s0"].astype(BF16))
        ch["lv"] = _dot(ch["l_ak"], ch["vh"].astype(BF16))
    for ch in chains:
        u = _dot(ch["inv"], (ch["ars"][:c2] + ch["lv"]).astype(BF16))
        ch["uv"] = jnp.concatenate([u, ch["vh"]], 0)
    for ch in chains:
        y = ch["ars"][c2:] + _dot(ch["l_r"], ch["uv"].astype(BF16))
        ch["y_ref"][0, :, ch["sl"]] = y[:c] + y[c:]
    for ch in chains:
        upd = (ch["s0"] + _dot(ch["uv"].T.astype(BF16), ch["bk"])) * ch["p_end"]
        s_ref[ch["d"], ch["pr"]] = jnp.where(same, upd, 0.0)


def _rwkv_scan_call(at, bt, kt, rt, pe, v, n_ctx):
    b, l, gw = v.shape
    c = RWKV_CHUNK
    nc = l // c
    ncc = n_ctx // c

    def rev(i):
        return jnp.where(i < ncc, ncc - 1 - i, nc - 1 - (i - ncc))

    def specs(d, ch):
        tokd = pl.BlockSpec((1, 1, c, gw), lambda bi, i: (bi, d, ch(i), 0))
        return [tokd, tokd, tokd, tokd, pl.BlockSpec((1, 1, 1, 1, gw), lambda bi, i: (bi, d, ch(i), 0, 0)),
                pl.BlockSpec((1, c, gw), lambda bi, i: (bi, ch(i), 0))]

    ys = jax.ShapeDtypeStruct((b, l, gw), F32)
    return pl.pallas_call(
        _rwkv_scan_kernel,
        grid=(b, nc),
        in_specs=specs(0, lambda i: i) + specs(1, rev),
        out_specs=[pl.BlockSpec((1, c, gw), lambda bi, i: (bi, i, 0)),
                   pl.BlockSpec((1, c, gw), lambda bi, i: (bi, rev(i), 0))],
        out_shape=[ys, ys],
        scratch_shapes=[pltpu.VMEM((2, gw // LANES, LANES, LANES), F32)],
        compiler_params=_cparams(2),
        name="rwkv_scan",
    )(at, bt, kt, rt, pe, v, at, bt, kt, rt, pe, v)


def _rwkv_mixer_out(yf_ref, yr_ref, bonus_ref, g_ref, gg_ref, gb_ref):
    flat = lambda r: r[...].reshape(-1, r.shape[-1])
    y = flat(yf_ref) + flat(yr_ref)
    low = lax.broadcasted_iota(jnp.int32, (y.shape[0], LANES), 1) < HEAD_DIM

    def head_mean(t):
        tot = jnp.sum(t, -1, keepdims=True)
        lo = jnp.sum(jnp.where(low, t, 0.0), -1, keepdims=True)
        return jnp.where(low, lo, tot - lo) * (1.0 / HEAD_DIM)

    outs = []
    for s in range(0, y.shape[1], LANES):
        ys = y[:, s:s + LANES]
        yc = ys - head_mean(ys)
        outs.append(yc * lax.rsqrt(head_mean(yc * yc) + RWKV_GN_EPS))
    yn = jnp.concatenate(outs, 1) * gg_ref[...] + gb_ref[...]
    return (yn + flat(bonus_ref)) * flat(g_ref)


def _diff_attn_kernel(lam_ref, q_ref, k_ref, vt_ref, g_ref, o_ref, *, n_ctx, n_ctx_tiles, out_scale):
    t = pl.program_id(1)
    lam = lam_ref[0]
    vd = 2 * HEAD_DIM
    tq = q_ref.shape[1]
    low = lax.broadcasted_iota(jnp.int32, (tq, vd), 1) < HEAD_DIM

    def run(nk):
        def logits(u):
            h, m = divmod(u, 2)
            qf = q_ref[0, :, h * vd:(h + 1) * vd].astype(F32)
            qm = jnp.where(low == (m == 0), qf, 0.0).astype(BF16)
            return _dot_nt(k_ref[0, :nk, h * vd:(h + 1) * vd], qm)

        n_units = 2 * DIFF_HEADS
        s_next = logits(0)
        es, invs = [], []
        for u in range(n_units):
            s = s_next
            if u + 1 < n_units:
                s_next = logits(u + 1)
            e = jnp.exp2(s - jnp.max(s, 0, keepdims=True))
            invs.append(1.0 / jnp.sum(e, 0, keepdims=True))
            es.append(e.astype(BF16))
            if u % 2 == 1:
                h = u // 2
                pv = _dot(vt_ref[0, h * vd:(h + 1) * vd, :nk], jnp.concatenate(es, 1))
                o = pv[:, :tq] * invs[0] - pv[:, tq:] * (lam * invs[1])
                o = o * lax.rsqrt(jnp.mean(o * o, 0, keepdims=True) + LN_EPS) * (g_ref[...] * out_scale)
                o_ref[0, :, h * vd:(h + 1) * vd] = o.T
                es, invs = [], []

    @pl.when(t < n_ctx_tiles)
    def _():
        run(n_ctx)

    @pl.when(t >= n_ctx_tiles)
    def _():
        run(k_ref.shape[1])


def _diff_attn_call(lam, q, k, vt, subln_g_col, n_ctx, lam_init):
    b, l, w = q.shape
    tok = pl.BlockSpec((1, ROW_TILE, w), lambda bi, t: (bi, t, 0))
    return pl.pallas_call(
        functools.partial(_diff_attn_kernel, n_ctx=n_ctx, n_ctx_tiles=n_ctx // ROW_TILE, out_scale=1.0 - lam_init),
        grid=(b, l // ROW_TILE),
        in_specs=[pl.BlockSpec(memory_space=pltpu.SMEM), tok,
                  pl.BlockSpec((1, l, w), lambda bi, t: (bi, 0, 0)),
                  pl.BlockSpec((1, w, l), lambda bi, t: (bi, 0, 0)),
                  pl.BlockSpec(subln_g_col.shape, lambda bi, t: (0, 0))],
        out_specs=tok,
        out_shape=jax.ShapeDtypeStruct((b, l, w), F32),
        compiler_params=_cparams(2),
        name="diff_attn",
    )(lam, q, k, vt, subln_g_col)


def _outproj_ffn_kernel(*refs, alpha, n_streams, n_ctx_tiles, mixer_out, n_mix):
    o2_ref = refs[n_mix]
    h_refs = refs[n_mix + 1:n_mix + 1 + n_streams]
    (gm_ref, sh_ref, sc_ref, gf_ref, wo_ref, w1_ref, w2_ref, lmg_ref, lmb_ref, lfg_ref, lfb_ref,
     out_ref) = refs[n_mix + 1 + n_streams:]
    bp, rows, gw = o2_ref.shape
    flat = lambda t: t.reshape(bp * rows, t.shape[-1])
    per_b = lambda t: t.reshape(bp, rows, t.shape[-1])
    o1 = mixer_out(*refs[:n_mix])
    o = _dot(o1.astype(BF16), wo_ref[:gw, :]) + _dot(flat(o2_ref[...]).astype(BF16), wo_ref[gw:, :])
    h_in = _stream_tile(h_refs, n_ctx_tiles)
    h1 = _layer_norm(flat(alpha * h_in + gm_ref[:, 0, 0] * per_b(o)), lmg_ref[...], lmb_ref[...], LN_EPS)
    xm = flat(per_b(h1) * (1.0 + sc_ref[:, 0, 0]) + sh_ref[:, 0, 0]).astype(BF16)
    hidden = w1_ref.shape[1]
    step = 512

    def up(j):
        hj = jnp.maximum(_dot(xm, w1_ref[:, j:j + step]), 0.0)
        return (hj * hj).astype(BF16)

    acc = None
    act = up(0)
    for j in range(0, hidden, step):
        nxt = up(j + step) if j + step < hidden else None
        part = _dot(act, w2_ref[j:j + step, :])
        acc = part if acc is None else acc + part
        act = nxt
    out = _layer_norm(alpha * h1 + flat(gf_ref[:, 0, 0] * per_b(acc)), lfg_ref[...], lfb_ref[...], LN_EPS)
    out_ref[...] = per_b(out)


def _outproj_ffn_call(mixer_out, mix_tok, mix_par, o2, streams, modsel, wo, w1, w2, lmg, lmb, lfg, lfb, *, h_off,
                      n_out_tiles, n_ctx_tiles, alpha, name):
    b, _, gw = o2.shape
    d = streams[0].shape[-1]
    bp = BATCH_PACK
    assert len(streams) == 1 or h_off == 0
    seg = lambda col: (lambda bi, t: (bi, ((t + h_off) >= n_ctx_tiles).astype(jnp.int32), col, 0, 0))
    modspec = lambda col: pl.BlockSpec((bp, 1, 1, 1, d), seg(col))
    full = lambda a: pl.BlockSpec(a.shape, lambda bi, t: (0,) * a.ndim, pipeline_mode=pl.Buffered(1))
    mtok = pl.BlockSpec((bp, ROW_TILE, gw), lambda bi, t: (bi, t + h_off, 0))
    otok = pl.BlockSpec((bp, ROW_TILE, gw), lambda bi, t: (bi, t, 0))
    if len(streams) == 1:
        h_specs = [pl.BlockSpec((bp, ROW_TILE, d), lambda bi, t: (bi, t + h_off, 0))]
    else:
        h_specs = _stream_specs(streams, n_ctx_tiles)
    n_mix = len(mix_tok) + len(mix_par)
    return pl.pallas_call(
        functools.partial(_outproj_ffn_kernel, alpha=alpha, n_streams=len(streams), n_ctx_tiles=n_ctx_tiles,
                          mixer_out=mixer_out, n_mix=n_mix),
        grid=(b // bp, n_out_tiles),
        in_specs=[mtok] * len(mix_tok) + [full(p) for p in mix_par] + [otok] + h_specs + [
            modspec(2), modspec(3), modspec(4), modspec(5),
            full(wo), full(w1), full(w2), full(lmg), full(lmb), full(lfg), full(lfb)],
        out_specs=pl.BlockSpec((bp, ROW_TILE, d), lambda bi, t: (bi, t, 0)),
        out_shape=jax.ShapeDtypeStruct((b, n_out_tiles * ROW_TILE, d), F32),
        compiler_params=_cparams(2),
        name=name,
    )(*mix_tok, *mix_par, o2, *streams, modsel, modsel, modsel, modsel, wo, w1, w2, lmg, lmb, lfg, lfb)


def _ssd_prep_kernel(x_ref, hp_ref, hn_ref, dt_ref, cw_ref, cb_ref, dtb_ref, an_ref, xc_o, dt_o, adt_o,
                     *, n_ctx_tiles, n_tiles):
    t = pl.program_id(1)
    x = x_ref[0]
    rows = x.shape[0]
    first = jnp.logical_or(t == 0, t == n_ctx_tiles)
    last = jnp.logical_or(t == n_ctx_tiles - 1, t == n_tiles - 1)
    ext = jnp.concatenate([jnp.where(first, 0.0, hp_ref[0]), x, jnp.where(last, 0.0, hn_ref[0])], 0)
    n_ext = rows + 2 * HALO
    acc = cb_ref[...] + cw_ref[SSD_CONV // 2:SSD_CONV // 2 + 1, :] * x
    for j in range(SSD_CONV):
        s = j - SSD_CONV // 2
        if s == 0:
            continue
        shifted = pltpu.roll(ext, (n_ext - s) % n_ext, 0)[HALO:HALO + rows]
        acc = acc + cw_ref[j:j + 1, :] * shifted
    xc_o[0] = acc * _sigmoid(acc)
    dt = _softplus(dt_ref[0] + dtb_ref[...])
    dt_o[0] = dt
    adt_o[0] = dt * an_ref[...]


def _ssd_prep_call(xbc, dt, conv_w, conv_b, dt_bias, a_neg, n_ctx_tiles):
    b, l, cw = xbc.shape
    n_tiles = l // ROW_TILE
    hb = ROW_TILE // HALO
    n_hb = l // HALO
    full = lambda a: pl.BlockSpec(a.shape, lambda bi, t: (0,) * a.ndim)
    tokd = pl.BlockSpec((1, ROW_TILE, LANES), lambda bi, t: (bi, t, 0))
    sd = jax.ShapeDtypeStruct((b, l, LANES), F32)
    return pl.pallas_call(
        functools.partial(_ssd_prep_kernel, n_ctx_tiles=n_ctx_tiles, n_tiles=n_tiles),
        grid=(b, n_tiles),
        in_specs=[pl.BlockSpec((1, ROW_TILE, cw), lambda bi, t: (bi, t, 0)),
                  pl.BlockSpec((1, HALO, cw), lambda bi, t: (bi, jnp.maximum(t * hb - 1, 0), 0)),
                  pl.BlockSpec((1, HALO, cw), lambda bi, t: (bi, jnp.minimum((t + 1) * hb, n_hb - 1), 0)),
                  tokd, full(conv_w), full(conv_b), full(dt_bias), full(a_neg)],
        out_specs=[pl.BlockSpec((1, ROW_TILE, cw), lambda bi, t: (bi, t, 0)), tokd, tokd],
        out_shape=[jax.ShapeDtypeStruct((b, l, cw), F32), sd, sd],
        compiler_params=_cparams(2),
        name="ssd_prep",
    )(xbc, xbc, xbc, dt, conv_w, conv_b, dt_bias, a_neg)


def _ssd_scan_kernel(xcf_ref, dtf_ref, adtf_ref, adttf_ref, xcr_ref, dtr_ref, adtr_ref, adttr_ref, e_ref,
                     yf_ref, yr_ref, st_ref):
    i = pl.program_id(1)
    q = xcf_ref.shape[1]
    gw = yf_ref.shape[-1]
    n = SSD_STATE
    hpg = SSD_HEADS // SSD_GROUPS
    gcols = hpg * HEAD_DIM

    @pl.when(i == 0)
    def _():
        st_ref[...] = jnp.zeros_like(st_ref)

    row = lax.broadcasted_iota(jnp.int32, (q, q), 0)
    col = lax.broadcasted_iota(jnp.int32, (q, q), 1)
    chains = []
    dirs = ((1, xcf_ref, dtf_ref, adtf_ref, adttf_ref, yf_ref), (-1, xcr_ref, dtr_ref, adtr_ref, adttr_ref, yr_ref))
    for d, (sgn, xc_ref, dt_ref, adt_ref, adtt_ref, y_ref) in enumerate(dirs):
        incl = (row - col) * sgn >= 0
        tri = incl.astype(BF16)
        tri_t = ((col - row) * sgn >= 0).astype(BF16)
        cs = _dot_exact_by_f32(tri, adt_ref[0])
        cs_t = _dot_f32_by_exact(adtt_ref[0], tri_t)
        e = e_ref[d]
        cse = _dot_f32_by_exact(cs, e)
        dte = _dot_f32_by_exact(dt_ref[0], e)
        tot = cse[q - 1:q, :] if sgn > 0 else cse[0:1, :]
        xc = xc_ref[0]
        xdt = xc[:, :gw] * dte
        e_cs = jnp.exp(cse)
        x_end = (xdt * jnp.exp(tot - cse)).astype(BF16)
        e_tot = jnp.exp(tot)
        xdt16 = xdt.astype(BF16)
        for g in range(SSD_GROUPS):
            gsl = slice(g * gcols, (g + 1) * gcols)
            lms = []
            for hh in range(hpg):
                h = g * hpg + hh
                colv = cse[:, h * HEAD_DIM:h * HEAD_DIM + 1]
                rowv = cs_t[d * SSD_HEADS + h:d * SSD_HEADS + h + 1, :]
                lms.append(jnp.where(incl, jnp.exp(colv - rowv), 0.0))
            chains.append(dict(d=d, g=g, gsl=gsl, y_ref=y_ref, lms=lms, e_cs=e_cs[:, gsl], e_tot=e_tot[:, gsl],
                               x_end=x_end[:, gsl], xdt16=xdt16[:, gsl],
                               bg=xc[:, gw + g * n:gw + (g + 1) * n],
                               cg=xc[:, gw + (SSD_GROUPS + g) * n:gw + (SSD_GROUPS + g + 1) * n].astype(BF16)))
    for ch in chains:
        ch["cb"] = _dot_nt(ch["cg"], ch["bg"].astype(BF16))
        ch["st"] = st_ref[ch["d"], ch["g"]]
    for ch in chains:
        ch["y_off"] = _dot(ch["cg"], ch["st"].astype(BF16)) * ch["e_cs"]
    for ch in chains:
        ch["ys"] = [_dot((ch["cb"] * ch["lms"][hh]).astype(BF16), ch["xdt16"][:, hh * HEAD_DIM:(hh + 1) * HEAD_DIM])
                    for hh in range(hpg)]
    for ch in chains:
        ch["y_ref"][0, :, ch["gsl"]] = jnp.concatenate(ch["ys"], 1) + ch["y_off"]
        st_ref[ch["d"], ch["g"]] = ch["st"] * ch["e_tot"] + _dot(ch["bg"].T.astype(BF16), ch["x_end"])


def _ssd_scan_call(xc, dt, adt, adt_t, e_sel, n_ctx):
    b, l, cw = xc.shape
    gw = e_sel.shape[-1]
    q = SSD_CHUNK
    nc = l // q
    ncc = n_ctx // q

    def rev(i):
        return jnp.where(i < ncc, ncc - 1 - i, nc - 1 - (i - ncc))

    def specs(ch):
        tokd = pl.BlockSpec((1, q, LANES), lambda bi, i: (bi, ch(i), 0))
        return [pl.BlockSpec((1, q, cw), lambda bi, i: (bi, ch(i), 0)), tokd, tokd,
                pl.BlockSpec((1, 2 * SSD_HEADS, q), lambda bi, i: (bi, 0, ch(i)))]

    ys = jax.ShapeDtypeStruct((b, l, gw), F32)
    return pl.pallas_call(
        _ssd_scan_kernel,
        grid=(b, nc),
        in_specs=specs(lambda i: i) + specs(rev) + [pl.BlockSpec(e_sel.shape, lambda bi, i: (0, 0, 0))],
        out_specs=[pl.BlockSpec((1, q, gw), lambda bi, i: (bi, i, 0)),
                   pl.BlockSpec((1, q, gw), lambda bi, i: (bi, rev(i), 0))],
        out_shape=[ys, ys],
        scratch_shapes=[pltpu.VMEM((2, SSD_GROUPS, SSD_STATE, gw // SSD_GROUPS), F32)],
        compiler_params=_cparams(2),
        name="ssd_scan",
    )(xc, dt, adt, adt_t, xc, dt, adt, adt_t, e_sel)


def _ssd_mixer_out(yf_ref, yr_ref, xs_ref, z_ref, dsk_ref, ng_ref):
    flat = lambda r: r[...].reshape(-1, r.shape[-1])
    z = flat(z_ref)
    y = (flat(yf_ref) + flat(yr_ref) + dsk_ref[...] * flat(xs_ref)) * (z * _sigmoid(z))
    gwid = y.shape[1] // SSD_GROUPS
    outs = []
    for g in range(SSD_GROUPS):
        yg = y[:, g * gwid:(g + 1) * gwid]
        outs.append(yg * lax.rsqrt(jnp.mean(yg * yg, -1, keepdims=True) + LN_EPS))
    return jnp.concatenate(outs, 1) * ng_ref[...]


def _swa_kernel(sink_ref, q_ref, k_ref, v_ref, o_ref, *, n_ctx, n_lat):
    i = pl.program_id(1)
    tq = q_ref.shape[1]
    span = tq + 2 * SWA_WINDOW
    nk = n_ctx + span
    group = SWA_HEADS // SWA_KV_HEADS
    start = jnp.clip(i * tq - SWA_WINDOW, 0, n_lat - span)
    w0 = pl.multiple_of(n_ctx + start, LANES)
    kpos = start + lax.broadcasted_iota(jnp.int32, (span, tq), 0)
    qpos = i * tq + lax.broadcasted_iota(jnp.int32, (span, tq), 1)
    bias = jnp.where(jnp.abs(qpos - kpos) <= SWA_WINDOW, 0.0, -1e30)
    bias = jnp.concatenate([jnp.zeros((n_ctx, tq), F32), bias], 0)
    bias = jnp.concatenate([bias] * group, 1)
    kcat = jnp.concatenate([k_ref[0, :n_ctx, :], k_ref[0, pl.ds(w0, span), :]], 0).astype(F32)
    vt = jnp.concatenate([v_ref[0, :n_ctx, :], v_ref[0, pl.ds(w0, span), :]], 0).astype(F32).T
    qs = jnp.concatenate([q_ref[0, :, g * LANES:(g + 1) * LANES] for g in range(group)], 0)
    k_low = lax.broadcasted_iota(jnp.int32, kcat.shape, 1) < HEAD_DIM
    v_low = lax.broadcasted_iota(jnp.int32, vt.shape, 0) < HEAD_DIM
    ss, es, rdens = [], [], []
    for kvh in range(SWA_KV_HEADS):
        km = jnp.where(k_low == (kvh == 0), kcat, 0.0).astype(BF16)
        ss.append(_dot_nt(km, qs) + bias)
    for kvh in range(SWA_KV_HEADS):
        sink = jnp.concatenate([jnp.full((1, tq), sink_ref[kvh * group + g], F32) for g in range(group)], 1)
        m = jnp.maximum(jnp.max(ss[kvh], 0, keepdims=True), sink)
        e = jnp.exp2(ss[kvh] - m)
        rdens.append(1.0 / (jnp.exp2(sink - m) + jnp.sum(e, 0, keepdims=True)))
        es.append(e.astype(BF16))
    ot = (_dot(jnp.where(v_low, vt, 0.0).astype(BF16), es[0]) * rdens[0]
          + _dot(jnp.where(v_low, 0.0, vt).astype(BF16), es[1]) * rdens[1])
    for g in range(group):
        o_ref[0, :, g * LANES:(g + 1) * LANES] = ot[:, g * tq:(g + 1) * tq].T


def _swa_call(sink, q, k, v, n_ctx):
    b, l, qw = q.shape
    kw = k.shape[-1]
    n_lat = l - n_ctx
    off = n_ctx // SWA_TILE
    kv = pl.BlockSpec((1, l, kw), lambda bi, t: (bi, 0, 0))
    return pl.pallas_call(
        functools.partial(_swa_kernel, n_ctx=n_ctx, n_lat=n_lat),
        grid=(b, n_lat // SWA_TILE),
        in_specs=[pl.BlockSpec(memory_space=pltpu.SMEM),
                  pl.BlockSpec((1, SWA_TILE, qw), lambda bi, t: (bi, t + off, 0)), kv, kv],
        out_specs=pl.BlockSpec((1, SWA_TILE, qw), lambda bi, t: (bi, t, 0)),
        out_shape=jax.ShapeDtypeStruct((b, n_lat, qw), F32),
        compiler_params=_cparams(2),
        name="swa",
    )(sink, q, k, v)


def _rope_tables(n_ctx, n_lat, width):
    half = HEAD_DIM // 4
    inv = ROPE_BASE ** (-jnp.arange(half, dtype=F32) / half)
    pos = jnp.arange(n_lat, dtype=jnp.int32)
    rows = (pos // GRID_W).astype(F32)[:, None] * inv
    cols = (pos % GRID_W).astype(F32)[:, None] * inv
    cos = jnp.concatenate([jnp.cos(rows), jnp.cos(rows), jnp.cos(cols), jnp.cos(cols)], -1)
    sin = jnp.concatenate([-jnp.sin(rows), jnp.sin(rows), -jnp.sin(cols), jnp.sin(cols)], -1)
    cos = jnp.concatenate([jnp.ones((n_ctx, HEAD_DIM), F32), cos], 0)
    sin = jnp.concatenate([jnp.zeros((n_ctx, HEAD_DIM), F32), sin], 0)
    reps = width // HEAD_DIM
    return jnp.tile(cos, (1, reps)), jnp.tile(sin, (1, reps))


def kernel(x, c, ctx, c_ctx, mod_w, mod_b, ln_mix_g, ln_mix_b, ln_ffn_g, ln_ffn_b, ffn_w1, ffn_w2, w_out, ab_w_in, rwkv_mu, rwkv_w0, rwkv_w2, rwkv_a0, rwkv_a2, rwkv_g2, rwkv_k_k, rwkv_k_a, rwkv_r_k, rwkv_gn_g, rwkv_gn_b, diff_lq1, diff_lk1, diff_lq2, diff_lk2, diff_subln_g, cd_w_in, ssd_conv_w, ssd_conv_b, ssd_dt_bias, ssd_a_log, ssd_d, ssd_norm_g, swa_sink):
    b, n_lat, d = x.shape
    n_ctx = ctx.shape[1]
    depth = mod_w.shape[0]
    assert depth == 2 and n_ctx % ROW_TILE == 0 and n_lat % ROW_TILE == 0 and n_lat % GRID_W == 0
    assert n_lat >= SWA_TILE + 2 * SWA_WINDOW and b % BATCH_PACK == 0
    gw = d // 2
    n_ctx_tiles = n_ctx // ROW_TILE
    n_lat_tiles = n_lat // ROW_TILE
    alpha = (2 * depth) ** 0.25
    log2e = math.log2(math.e)
    q_scale = HEAD_DIM ** -0.5 * log2e

    rows_pad = -(-(b + 1) // 8) * 8
    c_pad = jnp.zeros((rows_pad, d), F32).at[:b].set(c).at[b].set(c_ctx)
    m = _mod_call(c_pad, mod_w, mod_b)

    def modsel(i):
        return jnp.stack([jnp.broadcast_to(m[i, b], (b, 6 * d)), m[i, :b]], 1).reshape(b, 2, 6, 1, d)

    cos_t, sin_t = _rope_tables(n_ctx, n_lat, 256)
    h0 = (ctx, x)

    rwkv_cols = 3 * gw + RWKV_LORA_W + RWKV_LORA_A + RWKV_LORA_G
    plan0 = ((0, rwkv_cols, False, 1.0, False), (rwkv_cols, gw, True, q_scale, False),
             (rwkv_cols + gw, gw, True, 1.0, False), (rwkv_cols + 2 * gw, gw, False, 1.0, True))
    p_rwkv, dq, dk, dvt = _inproj_call(h0, modsel(0), ab_w_in[0].astype(BF16), cos_t, sin_t, plan0,
                                       (F32, BF16, BF16, BF16), n_ctx_tiles, "inproj_ab")

    zw = jnp.zeros((RWKV_LORA_W, gw), F32)
    lora_w = jnp.concatenate([jnp.concatenate([rwkv_w2[0, 0], rwkv_w2[0, 1], zw, zw], 1),
                              jnp.concatenate([zw, zw, rwkv_a2[0, 0], rwkv_a2[0, 1]], 1)], 0)
    lora_b = jnp.concatenate([rwkv_w0[0, 0], rwkv_w0[0, 1], rwkv_a0[0, 0], rwkv_a0[0, 1]])[None, :]
    head_id = jnp.arange(gw, dtype=jnp.int32) // HEAD_DIM
    bd = (head_id[:, None] == head_id[None, :]).astype(BF16)
    at, bt, kt, rt, pe, rv, g, bonus = _rwkv_prep_call(p_rwkv, rwkv_mu[0][None, :], lora_b, _split_hilo(lora_w), _split_hilo(rwkv_g2[0]),
                                                       rwkv_k_k[0][None, :], rwkv_k_a[0][None, :],
                                                       rwkv_r_k[0].reshape(1, gw), bd, n_ctx_tiles)
    y_f, y_r = _rwkv_scan_call(at, bt, kt, rt, pe, rv, n_ctx)

    lam_init0 = 0.8 - 0.6 * math.exp(-0.3 * 0)
    lam = (jnp.exp(jnp.sum(diff_lq1[0] * diff_lk1[0])) - jnp.exp(jnp.sum(diff_lq2[0] * diff_lk2[0]))
           + lam_init0).reshape(1).astype(F32)
    o2 = _diff_attn_call(lam, dq, dk, dvt, diff_subln_g[0][:, None], n_ctx, lam_init0)

    row = lambda t: t[None, :]
    h1 = _outproj_ffn_call(_rwkv_mixer_out, (y_f, y_r, bonus, g), (row(rwkv_gn_g[0]), row(rwkv_gn_b[0])), o2, h0,
                           modsel(0), w_out[0].astype(BF16), ffn_w1[0].astype(BF16),
                           ffn_w2[0].astype(BF16), row(ln_mix_g[0]), row(ln_mix_b[0]), row(ln_ffn_g[0]),
                           row(ln_ffn_b[0]), h_off=0, n_out_tiles=n_ctx_tiles + n_lat_tiles,
                           n_ctx_tiles=n_ctx_tiles, alpha=alpha, name="outproj_ffn_0")

    conv_dim = gw + 2 * SSD_GROUPS * SSD_STATE
    ssd_cols = gw + conv_dim + 2 * SSD_HEADS
    kvw = SWA_KV_HEADS * HEAD_DIM
    wcd = cd_w_in[0]
    group = SWA_HEADS // SWA_KV_HEADS
    q_perm = jnp.arange(gw, dtype=jnp.int32).reshape(SWA_KV_HEADS, group, HEAD_DIM).transpose(1, 0, 2).reshape(-1)
    w1cat = jnp.concatenate([wcd[:, ssd_cols:ssd_cols + gw][:, q_perm], wcd[:, ssd_cols + gw:], wcd[:, :gw],
                             wcd[:, gw:gw + conv_dim], wcd[:, gw + conv_dim:ssd_cols],
                             jnp.zeros((d, LANES - 2 * SSD_HEADS), F32)], 1)
    w_out1 = jnp.concatenate([w_out[1, :gw], w_out[1, gw:][q_perm]], 0)
    c_q, c_k, c_v, c_z, c_x, c_dt = 0, gw, gw + kvw, gw + 2 * kvw, 2 * gw + 2 * kvw, 2 * gw + 2 * kvw + conv_dim
    plan1 = ((c_q, gw, True, q_scale, False), (c_k, kvw, True, 1.0, False), (c_v, kvw, False, 1.0, False),
             (c_z, gw, False, 1.0, False), (c_x, conv_dim, False, 1.0, False), (c_dt, LANES, False, 1.0, False))
    sq, sk, sv, pz, xbc, dt_raw = _inproj_call((h1,), modsel(1), w1cat.astype(BF16), cos_t, sin_t, plan1,
                                               (BF16, BF16, BF16, F32, F32, F32), n_ctx_tiles, "inproj_cd")

    pad_l = lambda t: jnp.concatenate([t, jnp.zeros((LANES - t.shape[0],), F32)])[None, :]
    a_neg = -jnp.exp(ssd_a_log[0].astype(F32))
    conv_w_pad = jnp.concatenate([ssd_conv_w[0], jnp.zeros((8 - SSD_CONV, conv_dim), F32)], 0)
    xc, dt_sp, adt = _ssd_prep_call(xbc, dt_raw, conv_w_pad, ssd_conv_b[0][None, :], pad_l(ssd_dt_bias[0].reshape(-1)),
                                    pad_l(a_neg.reshape(-1)), n_ctx_tiles)
    adt_t = jnp.swapaxes(adt[:, :, :2 * SSD_HEADS], 1, 2)
    jj = jnp.arange(LANES, dtype=jnp.int32)[:, None]
    hh = (jnp.arange(gw, dtype=jnp.int32) // HEAD_DIM)[None, :]
    e_sel = jnp.stack([(jj == hh), (jj == hh + SSD_HEADS)]).astype(BF16)
    ys_f, ys_r = _ssd_scan_call(xc, dt_sp, adt, adt_t, e_sel, n_ctx)
    o2 = _swa_call(swa_sink[0].astype(F32) * log2e, sq, sk, sv, n_ctx)

    return _outproj_ffn_call(_ssd_mixer_out, (ys_f, ys_r, xc, pz),
                             (row(jnp.repeat(ssd_d[0], HEAD_DIM)), row(ssd_norm_g[0])), o2, (h1,),
                             modsel(1), w_out1.astype(BF16), ffn_w1[1].astype(BF16),
                             ffn_w2[1].astype(BF16), row(ln_mix_g[1]), row(ln_mix_b[1]), row(ln_ffn_g[1]),
                             row(ln_ffn_b[1]), h_off=n_ctx_tiles, n_out_tiles=n_lat_tiles,
                             n_ctx_tiles=n_ctx_tiles, alpha=alpha, name="outproj_ffn_1")
```

```python
import functools
import math

import jax
import jax.numpy as jnp
from jax import lax
from jax.experimental import pallas as pl
from jax.experimental.pallas import tpu as pltpu

F32 = jnp.float32
BF16 = jnp.bfloat16

HEAD_DIM = 64
GRID_W = 64
ROPE_BASE = 10000.0
LN_EPS = 1e-5
RWKV_GN_EPS = 64e-5
RWKV_LORA_W = 64
RWKV_LORA_A = 64
RWKV_LORA_G = 128
DIFF_HEADS = 4
SSD_HEADS = 8
SSD_GROUPS = 2
SSD_STATE = 128
SSD_CONV = 5
SWA_HEADS = 8
SWA_KV_HEADS = 2
SWA_WINDOW = 128

ROW_TILE = 256
BATCH_PACK = 2
HALO = 8
RWKV_CHUNK = 64
SSD_CHUNK = 128
SWA_TILE = 128
LANES = 128
VMEM_LIMIT = 56 * 1024 * 1024


def _cparams(n_axes):
    return pltpu.CompilerParams(dimension_semantics=("arbitrary",) * n_axes, vmem_limit_bytes=VMEM_LIMIT)


def _dot(a, b):
    return jnp.dot(a, b, preferred_element_type=F32)


def _dot_nt(a, b):
    return lax.dot_general(a, b, (((1,), (1,)), ((), ())), preferred_element_type=F32)


def _split_bf16(a, pieces):
    out = []
    for i in range(pieces):
        part = a.astype(BF16)
        out.append(part)
        if i + 1 < pieces:
            a = a - part.astype(F32)
    return out


def _dot_f32_by_exact(a, b_exact, pieces=3):
    acc = None
    for part in _split_bf16(a, pieces):
        t = _dot(part, b_exact)
        acc = t if acc is None else acc + t
    return acc


def _dot_exact_by_f32(a_exact, b, pieces=3):
    acc = None
    for part in _split_bf16(b, pieces):
        t = _dot(a_exact, part)
        acc = t if acc is None else acc + t
    return acc


def _dot_hilo(a, b):
    ah = a.astype(BF16)
    al = (a - ah.astype(F32)).astype(BF16)
    bh = b.astype(BF16)
    bl = (b - bh.astype(F32)).astype(BF16)
    return _dot(ah, bh) + _dot(ah, bl) + _dot(al, bh)


def _split_hilo(w):
    hi = w.astype(BF16)
    return jnp.stack([hi, (w - hi.astype(F32)).astype(BF16)])


def _dot_hilo_presplit(a, w_ref):
    ah = a.astype(BF16)
    al = (a - ah.astype(F32)).astype(BF16)
    return _dot(ah, w_ref[0]) + _dot(ah, w_ref[1]) + _dot(al, w_ref[0])


def _sigmoid(x):
    return 1.0 / (1.0 + jnp.exp(-x))


def _softplus(x):
    return jnp.maximum(x, 0.0) + jnp.log(1.0 + jnp.exp(-jnp.abs(x)))


def _layer_norm(y, g, b, eps):
    mu = jnp.mean(y, -1, keepdims=True)
    yc = y - mu
    var = jnp.mean(yc * yc, -1, keepdims=True)
    return yc * lax.rsqrt(var + eps) * g + b


def _mod_kernel(c_ref, w_ref, b_ref, o_ref):
    c = c_ref[...]
    o_ref[0] = _dot_hilo(c * _sigmoid(c), w_ref[0]) + b_ref[0]


def _mod_call(c_pad, mod_w, mod_b):
    depth, d, n = mod_w.shape
    rows = c_pad.shape[0]
    tn = 1536
    return pl.pallas_call(
        _mod_kernel,
        grid=(depth, n // tn),
        in_specs=[pl.BlockSpec((rows, d), lambda i, j: (0, 0)),
                  pl.BlockSpec((1, d, tn), lambda i, j: (i, 0, j)),
                  pl.BlockSpec((1, 1, tn), lambda i, j: (i, 0, j))],
        out_specs=pl.BlockSpec((1, rows, tn), lambda i, j: (i, 0, j)),
        out_shape=jax.ShapeDtypeStruct((depth, rows, n), F32),
        compiler_params=_cparams(2),
        name="adaln_mod",
    )(c_pad, mod_w, mod_b.reshape(depth, 1, n))


def _rope(x, cos, sin):
    lane = lax.broadcasted_iota(jnp.int32, (x.shape[0], LANES), 1)
    first = (lane % 32) < 16
    parts = []
    for g in range(0, x.shape[1], LANES):
        xg = x[:, g:g + LANES]
        parts.append(jnp.where(first, pltpu.roll(xg, LANES - 16, 1), pltpu.roll(xg, 16, 1)))
    sw = parts[0] if len(parts) == 1 else jnp.concatenate(parts, 1)
    return x * cos + sw * sin


def _stream_specs(streams, n_ctx_tiles):
    d = streams[0].shape[-1]
    bp = BATCH_PACK
    if len(streams) == 1:
        return [pl.BlockSpec((bp, ROW_TILE, d), lambda bi, t: (bi, t, 0))]
    return [pl.BlockSpec((bp, ROW_TILE, d), lambda bi, t: (bi, jnp.minimum(t, n_ctx_tiles - 1), 0)),
            pl.BlockSpec((bp, ROW_TILE, d), lambda bi, t: (bi, jnp.maximum(t - n_ctx_tiles, 0), 0))]


def _stream_tile(refs, n_ctx_tiles):
    if len(refs) == 1:
        return refs[0][...]
    return jnp.where(pl.program_id(1) < n_ctx_tiles, refs[0][...], refs[1][...])


def _inproj_kernel(*refs, plan, n_streams, n_ctx_tiles):
    x_refs = refs[:n_streams]
    sh_ref, sc_ref, w_ref, cos_ref, sin_ref = refs[n_streams:n_streams + 5]
    o_refs = refs[n_streams + 5:]
    x = _stream_tile(x_refs, n_ctx_tiles)
    bp, rows, d = x.shape
    xm = (x * (1.0 + sc_ref[:, 0, 0]) + sh_ref[:, 0, 0]).reshape(bp * rows, d).astype(BF16)
    cos = jnp.concatenate([cos_ref[...]] * bp, 0)
    sin = jnp.concatenate([sin_ref[...]] * bp, 0)
    for o_ref, (c0, width, rope, mult, transpose) in zip(o_refs, plan):
        for j in range(0, width, 256):
            cw = min(256, width - j)
            acc = _dot(xm, w_ref[:, c0 + j:c0 + j + cw])
            if rope:
                acc = _rope(acc, cos[:, :cw], sin[:, :cw])
            if mult != 1.0:
                acc = acc * mult
            if transpose:
                for p in range(bp):
                    o_ref[p, j:j + cw, :] = acc[p * rows:(p + 1) * rows].T.astype(o_ref.dtype)
            else:
                o_ref[:, :, j:j + cw] = acc.reshape(bp, rows, cw).astype(o_ref.dtype)


def _inproj_call(streams, modsel, w_bf16, cos_t, sin_t, plan, out_dtypes, n_ctx_tiles, name):
    b, _, d = streams[0].shape
    l = sum(s.shape[1] for s in streams)
    n_tiles = l // ROW_TILE
    seg = lambda bi, t: (bi, (t >= n_ctx_tiles).astype(jnp.int32), 0, 0, 0)
    seg_scale = lambda bi, t: (bi, (t >= n_ctx_tiles).astype(jnp.int32), 1, 0, 0)
    out_shape, out_specs = [], []
    bp = BATCH_PACK
    for (_, width, _, _, transpose), dt in zip(plan, out_dtypes):
        if transpose:
            out_shape.append(jax.ShapeDtypeStruct((b, width, l), dt))
            out_specs.append(pl.BlockSpec((bp, width, ROW_TILE), lambda bi, t: (bi, 0, t)))
        else:
            out_shape.append(jax.ShapeDtypeStruct((b, l, width), dt))
            out_specs.append(pl.BlockSpec((bp, ROW_TILE, width), lambda bi, t: (bi, t, 0)))
    return pl.pallas_call(
        functools.partial(_inproj_kernel, plan=plan, n_streams=len(streams), n_ctx_tiles=n_ctx_tiles),
        grid=(b // bp, n_tiles),
        in_specs=_stream_specs(streams, n_ctx_tiles) + [
            pl.BlockSpec((bp, 1, 1, 1, d), seg),
            pl.BlockSpec((bp, 1, 1, 1, d), seg_scale),
            pl.BlockSpec(w_bf16.shape, lambda bi, t: (0, 0), pipeline_mode=pl.Buffered(1)),
            pl.BlockSpec((ROW_TILE, cos_t.shape[1]), lambda bi, t: (t, 0)),
            pl.BlockSpec((ROW_TILE, sin_t.shape[1]), lambda bi, t: (t, 0))],
        out_specs=out_specs,
        out_shape=out_shape,
        compiler_params=_cparams(2),
        name=name,
    )(*streams, modsel, modsel, w_bf16, cos_t, sin_t)


def _rwkv_prep_kernel(p_ref, hp_ref, hn_ref, mu_ref, bias_ref, lora_ref, g2_ref, kk_ref, ka_ref, rk_ref, bd_ref,
                      at_o, bt_o, kt_o, rt_o, pe_o, v_o, g_o, bonus_o, *, n_ctx_tiles, n_tiles, chunk):
    t = pl.program_id(1)
    gw = v_o.shape[-1]
    p = p_ref[0]
    rows = p.shape[0]
    first = jnp.logical_or(t == 0, t == n_ctx_tiles)
    last = jnp.logical_or(t == n_ctx_tiles - 1, t == n_tiles - 1)
    prev_row = jnp.where(first, 0.0, hp_ref[0, HALO - 1:HALO, :])
    next_row = jnp.where(last, 0.0, hn_ref[0, 0:1, :])
    rowi = lax.broadcasted_iota(jnp.int32, p.shape, 0)
    prev = jnp.where(rowi == 0, prev_row, pltpu.roll(p, 1, 0))
    nxt = jnp.where(rowi == rows - 1, next_row, pltpu.roll(p, rows - 1, 0))
    ps = p + mu_ref[...] * (0.5 * (prev + nxt) - p)

    r = ps[:, :gw]
    k = ps[:, gw:2 * gw]
    v = ps[:, 2 * gw:3 * gw]
    slab = ps[:, 3 * gw:3 * gw + LANES]
    xg = ps[:, 3 * gw + LANES:3 * gw + 2 * LANES]
    lane = lax.broadcasted_iota(jnp.int32, slab.shape, 1)
    lora_in = jnp.where(lane < RWKV_LORA_W, jnp.tanh(slab), slab)
    pre = _dot_hilo_presplit(lora_in, lora_ref) + bias_ref[...]
    bd = bd_ref[...]
    g_o[0] = _dot_hilo_presplit(_sigmoid(xg), g2_ref)
    kk = k * kk_ref[...]
    ss = _dot_f32_by_exact(kk * kk, bd, 2)
    kk = kk * lax.rsqrt(jnp.maximum(ss, 1e-24))
    bonus_o[0] = _dot_f32_by_exact(r * k * rk_ref[...], bd, 2) * v
    v_o[0] = v.astype(v_o.dtype)
    ri = lax.broadcasted_iota(jnp.int32, (rows, rows), 0)
    ci = lax.broadcasted_iota(jnp.int32, (rows, rows), 1)
    same_chunk = (ri // chunk) == (ci // chunk)
    for d in range(2):
        w_log = -_softplus(-pre[:, d * gw:(d + 1) * gw]) - 0.5
        lw = -jnp.exp(w_log)
        a = _sigmoid(pre[:, (2 + d) * gw:(3 + d) * gw])
        before = (ci <= ri) if d == 0 else (ci >= ri)
        tri = jnp.logical_and(same_chunk, before).astype(BF16)
        cum = _dot_exact_by_f32(tri, lw, 2)
        p_inv = jnp.exp(-cum)
        at_o[0, d] = (-kk * jnp.exp(cum - lw)).astype(at_o.dtype)
        bt_o[0, d] = (kk * a * p_inv).astype(bt_o.dtype)
        kt_o[0, d] = (k * (1.0 + (a - 1.0) * ka_ref[...]) * p_inv).astype(kt_o.dtype)
        rt_o[0, d] = (r * jnp.exp(cum)).astype(rt_o.dtype)
        for j in range(rows // chunk):
            last_row = j * chunk + (chunk - 1 if d == 0 else 0)
            pe_o[0, d, j] = jnp.exp(cum[last_row:last_row + 1, :])


def _rwkv_prep_call(p, mu, bias, lora_w, g2, k_k, k_a, r_k, bd, n_ctx_tiles):
    b, l, cols = p.shape
    gw = k_k.shape[-1]
    n_tiles = l // ROW_TILE
    hb = ROW_TILE // HALO
    n_hb = l // HALO
    cpt = ROW_TILE // RWKV_CHUNK
    full = lambda a: pl.BlockSpec(a.shape, lambda bi, t: (0,) * a.ndim)
    tok = pl.BlockSpec((1, ROW_TILE, gw), lambda bi, t: (bi, t, 0))
    tok2 = pl.BlockSpec((1, 2, ROW_TILE, gw), lambda bi, t: (bi, 0, t, 0))
    s1 = jax.ShapeDtypeStruct((b, l, gw), F32)
    s2 = jax.ShapeDtypeStruct((b, 2, l, gw), BF16)
    return pl.pallas_call(
        functools.partial(_rwkv_prep_kernel, n_ctx_tiles=n_ctx_tiles, n_tiles=n_tiles, chunk=RWKV_CHUNK),
        grid=(b, n_tiles),
        in_specs=[pl.BlockSpec((1, ROW_TILE, cols), lambda bi, t: (bi, t, 0)),
                  pl.BlockSpec((1, HALO, cols), lambda bi, t: (bi, jnp.maximum(t * hb - 1, 0), 0)),
                  pl.BlockSpec((1, HALO, cols), lambda bi, t: (bi, jnp.minimum((t + 1) * hb, n_hb - 1), 0)),
                  full(mu), full(bias), full(lora_w), full(g2), full(k_k), full(k_a), full(r_k), full(bd)],
        out_specs=[tok2, tok2, tok2, tok2,
                   pl.BlockSpec((1, 2, cpt, 1, gw), lambda bi, t: (bi, 0, t, 0, 0)), tok, tok, tok],
        out_shape=[s2, s2, s2, s2, jax.ShapeDtypeStruct((b, 2, l // RWKV_CHUNK, 1, gw), F32),
                   jax.ShapeDtypeStruct((b, l, gw), BF16), s1, s1],
        compiler_params=_cparams(2),
        name="rwkv_prep",
    )(p, p, p, mu, bias, lora_w, g2, k_k, k_a, r_k, bd)


def _rwkv_scan_kernel(atf_ref, btf_ref, ktf_ref, rtf_ref, pef_ref, vf_ref, atr_ref, btr_ref, ktr_ref, rtr_ref, per_ref,
                      vr_ref, yf_ref, yr_ref, s_ref):
    i = pl.program_id(1)
    c = vf_ref.shape[1]
    n_pairs = vf_ref.shape[2] // LANES
    c2 = 2 * c

    @pl.when(i == 0)
    def _():
        s_ref[...] = jnp.zeros_like(s_ref)

    row2 = lax.broadcasted_iota(jnp.int32, (c2, c2), 0)
    col2 = lax.broadcasted_iota(jnp.int32, (c2, c2), 1)
    same = (row2 // c) == (col2 // c)
    eye = (row2 == col2).astype(F32)
    low = lax.broadcasted_iota(jnp.int32, (c, LANES), 1) < HEAD_DIM

    def stack(x):
        xf = x.astype(F32)
        return jnp.concatenate([jnp.where(low, xf, 0.0), jnp.where(low, 0.0, xf)], 0)

    chains = []
    dirs = ((1, atf_ref, btf_ref, ktf_ref, rtf_ref, pef_ref, vf_ref, yf_ref),
            (-1, atr_ref, btr_ref, ktr_ref, rtr_ref, per_ref, vr_ref, yr_ref))
    for d, (sgn, at_ref, bt_ref, kt_ref, rt_ref, pe_ref, v_ref, y_ref) in enumerate(dirs):
        dt2 = (row2 - col2) * sgn
        strict = jnp.logical_and(same, dt2 > 0)
        incl = jnp.logical_and(same, dt2 >= 0)
        for p in range(vf_ref.shape[0]):
            for pr in range(n_pairs):
                sl = slice(pr * LANES, (pr + 1) * LANES)
                b_s = bt_ref[p, 0, :, sl]
                k_s = kt_ref[p, 0, :, sl]
                chains.append(dict(st=(p, d, pr), p=p, sl=sl, y_ref=y_ref, strict=strict, incl=incl,
                                   p_end=pe_ref[p, 0, 0, :, sl], vh=stack(v_ref[p, :, sl]),
                                   ar=jnp.concatenate([stack(at_ref[p, 0, :, sl]), stack(rt_ref[p, 0, :, sl])],
                                                      0).astype(BF16),
                                   bk=jnp.concatenate([b_s, b_s, k_s, k_s], 0)))

    for ch in chains:
        gram = _dot_nt(ch["ar"], ch["bk"])
        l_ab = jnp.where(ch["strict"], gram[:c2, :c2], 0.0)
        ch["l_ak"] = jnp.where(ch["strict"], gram[:c2, c2:], 0.0).astype(BF16)
        ch["l_r"] = jnp.concatenate([jnp.where(ch["incl"], gram[c2:, :c2], 0.0),
                                     jnp.where(ch["incl"], gram[c2:, c2:], 0.0)], 1).astype(BF16)
        ch["inv"] = eye + l_ab
        ch["pw"] = l_ab.astype(BF16)
    for ch in chains:
        ch["pw"] = _dot(ch["pw"], ch["pw"]).astype(BF16)
    for _ in range(max(0, int(math.ceil(math.log2(c))) - 2)):
        for ch in chains:
            both = _dot(ch["pw"], jnp.concatenate([ch["inv"].astype(BF16), ch["pw"]], 1))
            ch["inv"] = ch["inv"] + both[:, :c2]
            ch["pw"] = both[:, c2:].astype(BF16)
    for ch in chains:
        ch["inv"] = (ch["inv"] + _dot(ch["pw"], ch["inv"].astype(BF16))).astype(BF16)
    for ch in chains:
        ch["s0"] = s_ref[ch["st"]]
        ch["ars"] = _dot_nt(ch["ar"], ch["s0"].astype(BF16))
        ch["lv"] = _dot(ch["l_ak"], ch["vh"].astype(BF16))
    for ch in chains:
        u = _dot(ch["inv"], (ch["ars"][:c2] + ch["lv"]).astype(BF16))
        ch["uv"] = jnp.concatenate([u, ch["vh"]], 0)
    for ch in chains:
        y = ch["ars"][c2:] + _dot(ch["l_r"], ch["uv"].astype(BF16))
        ch["y_ref"][ch["p"], :, ch["sl"]] = y[:c] + y[c:]
    for ch in chains:
        upd = (ch["s0"] + _dot(ch["uv"].T.astype(BF16), ch["bk"])) * ch["p_end"]
        s_ref[ch["st"]] = jnp.where(same, upd, 0.0)


def _rwkv_scan_call(at, bt, kt, rt, pe, v, n_ctx):
    b, l, gw = v.shape
    c = RWKV_CHUNK
    nc = l // c
    ncc = n_ctx // c

    def rev(i):
        return jnp.where(i < ncc, ncc - 1 - i, nc - 1 - (i - ncc))

    bp = BATCH_PACK

    def specs(d, ch):
        tokd = pl.BlockSpec((bp, 1, c, gw), lambda bi, i: (bi, d, ch(i), 0))
        return [tokd, tokd, tokd, tokd, pl.BlockSpec((bp, 1, 1, 1, gw), lambda bi, i: (bi, d, ch(i), 0, 0)),
                pl.BlockSpec((bp, c, gw), lambda bi, i: (bi, ch(i), 0))]

    ys = jax.ShapeDtypeStruct((b, l, gw), F32)
    return pl.pallas_call(
        _rwkv_scan_kernel,
        grid=(b // bp, nc),
        in_specs=specs(0, lambda i: i) + specs(1, rev),
        out_specs=[pl.BlockSpec((bp, c, gw), lambda bi, i: (bi, i, 0)),
                   pl.BlockSpec((bp, c, gw), lambda bi, i: (bi, rev(i), 0))],
        out_shape=[ys, ys],
        scratch_shapes=[pltpu.VMEM((bp, 2, gw // LANES, LANES, LANES), F32)],
        compiler_params=_cparams(2),
        name="rwkv_scan",
    )(at, bt, kt, rt, pe, v, at, bt, kt, rt, pe, v)


def _rwkv_mixer_out(yf_ref, yr_ref, bonus_ref, g_ref, gg_ref, gb_ref):
    flat = lambda r: r[...].reshape(-1, r.shape[-1])
    y = flat(yf_ref) + flat(yr_ref)
    low = lax.broadcasted_iota(jnp.int32, (y.shape[0], LANES), 1) < HEAD_DIM

    def head_mean(t):
        tot = jnp.sum(t, -1, keepdims=True)
        lo = jnp.sum(jnp.where(low, t, 0.0), -1, keepdims=True)
        return jnp.where(low, lo, tot - lo) * (1.0 / HEAD_DIM)

    outs = []
    for s in range(0, y.shape[1], LANES):
        ys = y[:, s:s + LANES]
        yc = ys - head_mean(ys)
        outs.append(yc * lax.rsqrt(head_mean(yc * yc) + RWKV_GN_EPS))
    yn = jnp.concatenate(outs, 1) * gg_ref[...] + gb_ref[...]
    return (yn + flat(bonus_ref)) * flat(g_ref)


def _diff_attn_kernel(lam_ref, q_ref, k_ref, vt_ref, g_ref, o_ref, *, n_ctx, n_ctx_tiles, out_scale):
    t = pl.program_id(1)
    lam = lam_ref[0]
    vd = 2 * HEAD_DIM
    tq = q_ref.shape[1]
    low = lax.broadcasted_iota(jnp.int32, (tq, vd), 1) < HEAD_DIM

    def run(nk):
        def logits(u):
            p, hm = divmod(u, 2 * DIFF_HEADS)
            h, m = divmod(hm, 2)
            qf = q_ref[p, :, h * vd:(h + 1) * vd].astype(F32)
            qm = jnp.where(low == (m == 0), qf, 0.0).astype(BF16)
            return _dot_nt(k_ref[p, :nk, h * vd:(h + 1) * vd], qm)

        n_units = 2 * DIFF_HEADS * q_ref.shape[0]
        s_next = logits(0)
        es, invs = [], []
        for u in range(n_units):
            s = s_next
            if u + 1 < n_units:
                s_next = logits(u + 1)
            e = jnp.exp2(s - jnp.max(s, 0, keepdims=True))
            invs.append(1.0 / jnp.sum(e, 0, keepdims=True))
            es.append(e.astype(BF16))
            if u % 2 == 1:
                p, hm = divmod(u, 2 * DIFF_HEADS)
                h = hm // 2
                pv = _dot(vt_ref[p, h * vd:(h + 1) * vd, :nk], jnp.concatenate(es, 1))
                o = pv[:, :tq] * invs[0] - pv[:, tq:] * (lam * invs[1])
                o = o * lax.rsqrt(jnp.mean(o * o, 0, keepdims=True) + LN_EPS) * (g_ref[...] * out_scale)
                o_ref[p, :, h * vd:(h + 1) * vd] = o.T
                es, invs = [], []

    @pl.when(t < n_ctx_tiles)
    def _():
        run(n_ctx)

    @pl.when(t >= n_ctx_tiles)
    def _():
        run(k_ref.shape[1])


def _diff_attn_call(lam, q, k, vt, subln_g_col, n_ctx, lam_init):
    b, l, w = q.shape
    bp = BATCH_PACK
    tok = pl.BlockSpec((bp, ROW_TILE, w), lambda bi, t: (bi, t, 0))
    return pl.pallas_call(
        functools.partial(_diff_attn_kernel, n_ctx=n_ctx, n_ctx_tiles=n_ctx // ROW_TILE, out_scale=1.0 - lam_init),
        grid=(b // bp, l // ROW_TILE),
        in_specs=[pl.BlockSpec(memory_space=pltpu.SMEM), tok,
                  pl.BlockSpec((bp, l, w), lambda bi, t: (bi, 0, 0)),
                  pl.BlockSpec((bp, w, l), lambda bi, t: (bi, 0, 0)),
                  pl.BlockSpec(subln_g_col.shape, lambda bi, t: (0, 0))],
        out_specs=tok,
        out_shape=jax.ShapeDtypeStruct((b, l, w), F32),
        compiler_params=_cparams(2),
        name="diff_attn",
    )(lam, q, k, vt, subln_g_col)


def _outproj_ffn_kernel(*refs, alpha, n_streams, n_ctx_tiles, mixer_out, n_mix):
    o2_ref = refs[n_mix]
    h_refs = refs[n_mix + 1:n_mix + 1 + n_streams]
    (gm_ref, sh_ref, sc_ref, gf_ref, wo_ref, w1_ref, w2_ref, lmg_ref, lmb_ref, lfg_ref, lfb_ref,
     out_ref) = refs[n_mix + 1 + n_streams:]
    bp, rows, gw = o2_ref.shape
    flat = lambda t: t.reshape(bp * rows, t.shape[-1])
    per_b = lambda t: t.reshape(bp, rows, t.shape[-1])
    o1 = mixer_out(*refs[:n_mix])
    o = _dot(o1.astype(BF16), wo_ref[:gw, :]) + _dot(flat(o2_ref[...]).astype(BF16), wo_ref[gw:, :])
    h_in = _stream_tile(h_refs, n_ctx_tiles)
    h1 = _layer_norm(flat(alpha * h_in + gm_ref[:, 0, 0] * per_b(o)), lmg_ref[...], lmb_ref[...], LN_EPS)
    xm = flat(per_b(h1) * (1.0 + sc_ref[:, 0, 0]) + sh_ref[:, 0, 0]).astype(BF16)
    hidden = w1_ref.shape[1]
    step = 512

    def up(j):
        hj = jnp.maximum(_dot(xm, w1_ref[:, j:j + step]), 0.0)
        return (hj * hj).astype(BF16)

    acc = None
    act = up(0)
    for j in range(0, hidden, step):
        nxt = up(j + step) if j + step < hidden else None
        part = _dot(act, w2_ref[j:j + step, :])
        acc = part if acc is None else acc + part
        act = nxt
    out = _layer_norm(alpha * h1 + flat(gf_ref[:, 0, 0] * per_b(acc)), lfg_ref[...], lfb_ref[...], LN_EPS)
    out_ref[...] = per_b(out)


def _outproj_ffn_call(mixer_out, mix_tok, mix_par, o2, streams, modsel, wo, w1, w2, lmg, lmb, lfg, lfb, *, h_off,
                      n_out_tiles, n_ctx_tiles, alpha, name):
    b, _, gw = o2.shape
    d = streams[0].shape[-1]
    bp = BATCH_PACK
    assert len(streams) == 1 or h_off == 0
    seg = lambda col: (lambda bi, t: (bi, ((t + h_off) >= n_ctx_tiles).astype(jnp.int32), col, 0, 0))
    modspec = lambda col: pl.BlockSpec((bp, 1, 1, 1, d), seg(col))
    full = lambda a: pl.BlockSpec(a.shape, lambda bi, t: (0,) * a.ndim, pipeline_mode=pl.Buffered(1))
    mtok = pl.BlockSpec((bp, ROW_TILE, gw), lambda bi, t: (bi, t + h_off, 0))
    otok = pl.BlockSpec((bp, ROW_TILE, gw), lambda bi, t: (bi, t, 0))
    if len(streams) == 1:
        h_specs = [pl.BlockSpec((bp, ROW_TILE, d), lambda bi, t: (bi, t + h_off, 0))]
    else:
        h_specs = _stream_specs(streams, n_ctx_tiles)
    n_mix = len(mix_tok) + len(mix_par)
    return pl.pallas_call(
        functools.partial(_outproj_ffn_kernel, alpha=alpha, n_streams=len(streams), n_ctx_tiles=n_ctx_tiles,
                          mixer_out=mixer_out, n_mix=n_mix),
        grid=(b // bp, n_out_tiles),
        in_specs=[mtok] * len(mix_tok) + [full(p) for p in mix_par] + [otok] + h_specs + [
            modspec(2), modspec(3), modspec(4), modspec(5),
            full(wo), full(w1), full(w2), full(lmg), full(lmb), full(lfg), full(lfb)],
        out_specs=pl.BlockSpec((bp, ROW_TILE, d), lambda bi, t: (bi, t, 0)),
        out_shape=jax.ShapeDtypeStruct((b, n_out_tiles * ROW_TILE, d), F32),
        compiler_params=_cparams(2),
        name=name,
    )(*mix_tok, *mix_par, o2, *streams, modsel, modsel, modsel, modsel, wo, w1, w2, lmg, lmb, lfg, lfb)


def _ssd_prep_kernel(x_ref, hp_ref, hn_ref, dt_ref, cw_ref, cb_ref, dtb_ref, an_ref, xc_o, dt_o, adt_o,
                     *, n_ctx_tiles, n_tiles):
    t = pl.program_id(1)
    x = x_ref[0]
    rows = x.shape[0]
    first = jnp.logical_or(t == 0, t == n_ctx_tiles)
    last = jnp.logical_or(t == n_ctx_tiles - 1, t == n_tiles - 1)
    ext = jnp.concatenate([jnp.where(first, 0.0, hp_ref[0]), x, jnp.where(last, 0.0, hn_ref[0])], 0)
    n_ext = rows + 2 * HALO
    acc = cb_ref[...] + cw_ref[SSD_CONV // 2:SSD_CONV // 2 + 1, :] * x
    for j in range(SSD_CONV):
        s = j - SSD_CONV // 2
        if s == 0:
            continue
        shifted = pltpu.roll(ext, (n_ext - s) % n_ext, 0)[HALO:HALO + rows]
        acc = acc + cw_ref[j:j + 1, :] * shifted
    xc_o[0] = acc * _sigmoid(acc)
    dt = _softplus(dt_ref[0] + dtb_ref[...])
    dt_o[0] = dt
    adt_o[0] = dt * an_ref[...]


def _ssd_prep_call(xbc, dt, conv_w, conv_b, dt_bias, a_neg, n_ctx_tiles):
    b, l, cw = xbc.shape
    n_tiles = l // ROW_TILE
    hb = ROW_TILE // HALO
    n_hb = l // HALO
    full = lambda a: pl.BlockSpec(a.shape, lambda bi, t: (0,) * a.ndim)
    tokd = pl.BlockSpec((1, ROW_TILE, LANES), lambda bi, t: (bi, t, 0))
    sd = jax.ShapeDtypeStruct((b, l, LANES), F32)
    return pl.pallas_call(
        functools.partial(_ssd_prep_kernel, n_ctx_tiles=n_ctx_tiles, n_tiles=n_tiles),
        grid=(b, n_tiles),
        in_specs=[pl.BlockSpec((1, ROW_TILE, cw), lambda bi, t: (bi, t, 0)),
                  pl.BlockSpec((1, HALO, cw), lambda bi, t: (bi, jnp.maximum(t * hb - 1, 0), 0)),
                  pl.BlockSpec((1, HALO, cw), lambda bi, t: (bi, jnp.minimum((t + 1) * hb, n_hb - 1), 0)),
                  tokd, full(conv_w), full(conv_b), full(dt_bias), full(a_neg)],
        out_specs=[pl.BlockSpec((1, ROW_TILE, cw), lambda bi, t: (bi, t, 0)), tokd, tokd],
        out_shape=[jax.ShapeDtypeStruct((b, l, cw), F32), sd, sd],
        compiler_params=_cparams(2),
        name="ssd_prep",
    )(xbc, xbc, xbc, dt, conv_w, conv_b, dt_bias, a_neg)


def _ssd_scan_kernel(xcf_ref, dtf_ref, adtf_ref, adttf_ref, xcr_ref, dtr_ref, adtr_ref, adttr_ref, e_ref,
                     yf_ref, yr_ref, st_ref):
    i = pl.program_id(1)
    q = xcf_ref.shape[1]
    gw = yf_ref.shape[-1]
    n = SSD_STATE
    hpg = SSD_HEADS // SSD_GROUPS
    gcols = hpg * HEAD_DIM

    @pl.when(i == 0)
    def _():
        st_ref[...] = jnp.zeros_like(st_ref)

    row = lax.broadcasted_iota(jnp.int32, (q, q), 0)
    col = lax.broadcasted_iota(jnp.int32, (q, q), 1)
    chains = []
    dirs = ((1, xcf_ref, dtf_ref, adtf_ref, adttf_ref, yf_ref), (-1, xcr_ref, dtr_ref, adtr_ref, adttr_ref, yr_ref))
    for p, d in [(p, d) for p in range(xcf_ref.shape[0]) for d in range(2)]:
        sgn, xc_ref, dt_ref, adt_ref, adtt_ref, y_ref = dirs[d]
        incl = (row - col) * sgn >= 0
        tri = incl.astype(BF16)
        tri_t = ((col - row) * sgn >= 0).astype(BF16)
        cs = _dot_exact_by_f32(tri, adt_ref[p])
        cs_t = _dot_f32_by_exact(adtt_ref[p], tri_t)
        e = e_ref[d]
        cse = _dot_f32_by_exact(cs, e)
        dte = _dot_f32_by_exact(dt_ref[p], e, 2)
        tot = cse[q - 1:q, :] if sgn > 0 else cse[0:1, :]
        xc = xc_ref[p]
        xdt = xc[:, :gw] * dte
        e_cs = jnp.exp(cse)
        x_end = (xdt * jnp.exp(tot - cse)).astype(BF16)
        e_tot = jnp.exp(tot)
        xdt16 = xdt.astype(BF16)
        for g in range(SSD_GROUPS):
            gsl = slice(g * gcols, (g + 1) * gcols)
            lms = []
            for hh in range(hpg):
                h = g * hpg + hh
                colv = cse[:, h * HEAD_DIM:h * HEAD_DIM + 1]
                rowv = cs_t[d * SSD_HEADS + h:d * SSD_HEADS + h + 1, :]
                lms.append(jnp.where(incl, jnp.exp(colv - rowv), 0.0))
            chains.append(dict(st=(p, d, g), p=p, gsl=gsl, y_ref=y_ref, lms=lms, e_cs=e_cs[:, gsl], e_tot=e_tot[:, gsl],
                               x_end=x_end[:, gsl], xdt16=xdt16[:, gsl],
                               bg=xc[:, gw + g * n:gw + (g + 1) * n],
                               cg=xc[:, gw + (SSD_GROUPS + g) * n:gw + (SSD_GROUPS + g + 1) * n].astype(BF16)))
    for ch in chains:
        ch["cb"] = _dot_nt(ch["cg"], ch["bg"].astype(BF16))
        ch["s0"] = st_ref[ch["st"]]
    for ch in chains:
        ch["y_off"] = _dot(ch["cg"], ch["s0"].astype(BF16)) * ch["e_cs"]
    for ch in chains:
        ch["ys"] = [_dot((ch["cb"] * ch["lms"][hh]).astype(BF16), ch["xdt16"][:, hh * HEAD_DIM:(hh + 1) * HEAD_DIM])
                    for hh in range(hpg)]
    for ch in chains:
        ch["y_ref"][ch["p"], :, ch["gsl"]] = jnp.concatenate(ch["ys"], 1) + ch["y_off"]
        st_ref[ch["st"]] = ch["s0"] * ch["e_tot"] + _dot(ch["bg"].T.astype(BF16), ch["x_end"])


def _ssd_scan_call(xc, dt, adt, adt_t, e_sel, n_ctx):
    b, l, cw = xc.shape
    gw = e_sel.shape[-1]
    q = SSD_CHUNK
    nc = l // q
    ncc = n_ctx // q

    def rev(i):
        return jnp.where(i < ncc, ncc - 1 - i, nc - 1 - (i - ncc))

    bp = BATCH_PACK

    def specs(ch):
        tokd = pl.BlockSpec((bp, q, LANES), lambda bi, i: (bi, ch(i), 0))
        return [pl.BlockSpec((bp, q, cw), lambda bi, i: (bi, ch(i), 0)), tokd, tokd,
                pl.BlockSpec((bp, 2 * SSD_HEADS, q), lambda bi, i: (bi, 0, ch(i)))]

    ys = jax.ShapeDtypeStruct((b, l, gw), F32)
    return pl.pallas_call(
        _ssd_scan_kernel,
        grid=(b // bp, nc),
        in_specs=specs(lambda i: i) + specs(rev) + [pl.BlockSpec(e_sel.shape, lambda bi, i: (0, 0, 0))],
        out_specs=[pl.BlockSpec((bp, q, gw), lambda bi, i: (bi, i, 0)),
                   pl.BlockSpec((bp, q, gw), lambda bi, i: (bi, rev(i), 0))],
        out_shape=[ys, ys],
        scratch_shapes=[pltpu.VMEM((bp, 2, SSD_GROUPS, SSD_STATE, gw // SSD_GROUPS), F32)],
        compiler_params=_cparams(2),
        name="ssd_scan",
    )(xc, dt, adt, adt_t, xc, dt, adt, adt_t, e_sel)


def _ssd_mixer_out(yf_ref, yr_ref, xs_ref, z_ref, dsk_ref, ng_ref):
    flat = lambda r: r[...].reshape(-1, r.shape[-1])
    z = flat(z_ref)
    y = (flat(yf_ref) + flat(yr_ref) + dsk_ref[...] * flat(xs_ref)) * (z * _sigmoid(z))
    gwid = y.shape[1] // SSD_GROUPS
    outs = []
    for g in range(SSD_GROUPS):
        yg = y[:, g * gwid:(g + 1) * gwid]
        outs.append(yg * lax.rsqrt(jnp.mean(yg * yg, -1, keepdims=True) + LN_EPS))
    return jnp.concatenate(outs, 1) * ng_ref[...]


def _swa_kernel(sink_ref, q_ref, k_ref, v_ref, o_ref, *, n_ctx, n_lat):
    i = pl.program_id(1)
    tq = q_ref.shape[1]
    span = tq + 2 * SWA_WINDOW
    nk = n_ctx + span
    group = SWA_HEADS // SWA_KV_HEADS
    start = jnp.clip(i * tq - SWA_WINDOW, 0, n_lat - span)
    w0 = pl.multiple_of(n_ctx + start, LANES)
    kpos = start + lax.broadcasted_iota(jnp.int32, (span, tq), 0)
    qpos = i * tq + lax.broadcasted_iota(jnp.int32, (span, tq), 1)
    bias = jnp.where(jnp.abs(qpos - kpos) <= SWA_WINDOW, 0.0, -1e30)
    bias = jnp.concatenate([jnp.zeros((n_ctx, tq), F32), bias], 0)
    bias = jnp.concatenate([bias] * group, 1)
    k_low = lax.broadcasted_iota(jnp.int32, (nk, LANES), 1) < HEAD_DIM
    v_low = lax.broadcasted_iota(jnp.int32, (LANES, nk), 0) < HEAD_DIM
    sinks = [jnp.concatenate([jnp.full((1, tq), sink_ref[kvh * group + g], F32) for g in range(group)], 1)
             for kvh in range(SWA_KV_HEADS)]
    bps = range(q_ref.shape[0])
    vts, ss, es, rdens = [], {}, {}, {}
    for p in bps:
        kcat = jnp.concatenate([k_ref[p, :n_ctx, :], k_ref[p, pl.ds(w0, span), :]], 0).astype(F32)
        vts.append(jnp.concatenate([v_ref[p, :n_ctx, :], v_ref[p, pl.ds(w0, span), :]], 0).astype(F32).T)
        qs = jnp.concatenate([q_ref[p, :, g * LANES:(g + 1) * LANES] for g in range(group)], 0)
        for kvh in range(SWA_KV_HEADS):
            km = jnp.where(k_low == (kvh == 0), kcat, 0.0).astype(BF16)
            ss[p, kvh] = _dot_nt(km, qs) + bias
    for p in bps:
        for kvh in range(SWA_KV_HEADS):
            m = jnp.maximum(jnp.max(ss[p, kvh], 0, keepdims=True), sinks[kvh])
            e = jnp.exp2(ss[p, kvh] - m)
            rdens[p, kvh] = 1.0 / (jnp.exp2(sinks[kvh] - m) + jnp.sum(e, 0, keepdims=True))
            es[p, kvh] = e.astype(BF16)
    for p in bps:
        ot = (_dot(jnp.where(v_low, vts[p], 0.0).astype(BF16), es[p, 0]) * rdens[p, 0]
              + _dot(jnp.where(v_low, 0.0, vts[p]).astype(BF16), es[p, 1]) * rdens[p, 1])
        for g in range(group):
            o_ref[p, :, g * LANES:(g + 1) * LANES] = ot[:, g * tq:(g + 1) * tq].T


def _swa_call(sink, q, k, v, n_ctx):
    b, l, qw = q.shape
    kw = k.shape[-1]
    n_lat = l - n_ctx
    off = n_ctx // SWA_TILE
    bp = BATCH_PACK
    kv = pl.BlockSpec((bp, l, kw), lambda bi, t: (bi, 0, 0))
    return pl.pallas_call(
        functools.partial(_swa_kernel, n_ctx=n_ctx, n_lat=n_lat),
        grid=(b // bp, n_lat // SWA_TILE),
        in_specs=[pl.BlockSpec(memory_space=pltpu.SMEM),
                  pl.BlockSpec((bp, SWA_TILE, qw), lambda bi, t: (bi, t + off, 0)), kv, kv],
        out_specs=pl.BlockSpec((bp, SWA_TILE, qw), lambda bi, t: (bi, t, 0)),
        out_shape=jax.ShapeDtypeStruct((b, n_lat, qw), F32),
        compiler_params=_cparams(2),
        name="swa",
    )(sink, q, k, v)


def _rope_tables(n_ctx, n_lat, width):
    half = HEAD_DIM // 4
    inv = ROPE_BASE ** (-jnp.arange(half, dtype=F32) / half)
    pos = jnp.arange(n_lat, dtype=jnp.int32)
    rows = (pos // GRID_W).astype(F32)[:, None] * inv
    cols = (pos % GRID_W).astype(F32)[:, None] * inv
    cos = jnp.concatenate([jnp.cos(rows), jnp.cos(rows), jnp.cos(cols), jnp.cos(cols)], -1)
    sin = jnp.concatenate([-jnp.sin(rows), jnp.sin(rows), -jnp.sin(cols), jnp.sin(cols)], -1)
    cos = jnp.concatenate([jnp.ones((n_ctx, HEAD_DIM), F32), cos], 0)
    sin = jnp.concatenate([jnp.zeros((n_ctx, HEAD_DIM), F32), sin], 0)
    reps = width // HEAD_DIM
    return jnp.tile(cos, (1, reps)), jnp.tile(sin, (1, reps))


def kernel(x, c, ctx, c_ctx, mod_w, mod_b, ln_mix_g, ln_mix_b, ln_ffn_g, ln_ffn_b, ffn_w1, ffn_w2, w_out, ab_w_in, rwkv_mu, rwkv_w0, rwkv_w2, rwkv_a0, rwkv_a2, rwkv_g2, rwkv_k_k, rwkv_k_a, rwkv_r_k, rwkv_gn_g, rwkv_gn_b, diff_lq1, diff_lk1, diff_lq2, diff_lk2, diff_subln_g, cd_w_in, ssd_conv_w, ssd_conv_b, ssd_dt_bias, ssd_a_log, ssd_d, ssd_norm_g, swa_sink):
    b, n_lat, d = x.shape
    n_ctx = ctx.shape[1]
    depth = mod_w.shape[0]
    assert depth == 2 and n_ctx % ROW_TILE == 0 and n_lat % ROW_TILE == 0 and n_lat % GRID_W == 0
    assert n_lat >= SWA_TILE + 2 * SWA_WINDOW and b % BATCH_PACK == 0
    gw = d // 2
    n_ctx_tiles = n_ctx // ROW_TILE
    n_lat_tiles = n_lat // ROW_TILE
    alpha = (2 * depth) ** 0.25
    log2e = math.log2(math.e)
    q_scale = HEAD_DIM ** -0.5 * log2e

    rows_pad = -(-(b + 1) // 8) * 8
    c_pad = jnp.zeros((rows_pad, d), F32).at[:b].set(c).at[b].set(c_ctx)
    m = _mod_call(c_pad, mod_w, mod_b)

    def modsel(i):
        return jnp.stack([jnp.broadcast_to(m[i, b], (b, 6 * d)), m[i, :b]], 1).reshape(b, 2, 6, 1, d)

    cos_t, sin_t = _rope_tables(n_ctx, n_lat, 256)
    h0 = (ctx, x)

    rwkv_cols = 3 * gw + RWKV_LORA_W + RWKV_LORA_A + RWKV_LORA_G
    plan0 = ((0, rwkv_cols, False, 1.0, False), (rwkv_cols, gw, True, q_scale, False),
             (rwkv_cols + gw, gw, True, 1.0, False), (rwkv_cols + 2 * gw, gw, False, 1.0, True))
    p_rwkv, dq, dk, dvt = _inproj_call(h0, modsel(0), ab_w_in[0].astype(BF16), cos_t, sin_t, plan0,
                                       (F32, BF16, BF16, BF16), n_ctx_tiles, "inproj_ab")

    zw = jnp.zeros((RWKV_LORA_W, gw), F32)
    lora_w = jnp.concatenate([jnp.concatenate([rwkv_w2[0, 0], rwkv_w2[0, 1], zw, zw], 1),
                              jnp.concatenate([zw, zw, rwkv_a2[0, 0], rwkv_a2[0, 1]], 1)], 0)
    lora_b = jnp.concatenate([rwkv_w0[0, 0], rwkv_w0[0, 1], rwkv_a0[0, 0], rwkv_a0[0, 1]])[None, :]
    head_id = jnp.arange(gw, dtype=jnp.int32) // HEAD_DIM
    bd = (head_id[:, None] == head_id[None, :]).astype(BF16)
    at, bt, kt, rt, pe, rv, g, bonus = _rwkv_prep_call(p_rwkv, rwkv_mu[0][None, :], lora_b, _split_hilo(lora_w), _split_hilo(rwkv_g2[0]),
                                                       rwkv_k_k[0][None, :], rwkv_k_a[0][None, :],
                                                       rwkv_r_k[0].reshape(1, gw), bd, n_ctx_tiles)
    y_f, y_r = _rwkv_scan_call(at, bt, kt, rt, pe, rv, n_ctx)

    lam_init0 = 0.8 - 0.6 * math.exp(-0.3 * 0)
    lam = (jnp.exp(jnp.sum(diff_lq1[0] * diff_lk1[0])) - jnp.exp(jnp.sum(diff_lq2[0] * diff_lk2[0]))
           + lam_init0).reshape(1).astype(F32)
    o2 = _diff_attn_call(lam, dq, dk, dvt, diff_subln_g[0][:, None], n_ctx, lam_init0)

    row = lambda t: t[None, :]
    h1 = _outproj_ffn_call(_rwkv_mixer_out, (y_f, y_r, bonus, g), (row(rwkv_gn_g[0]), row(rwkv_gn_b[0])), o2, h0,
                           modsel(0), w_out[0].astype(BF16), ffn_w1[0].astype(BF16),
                           ffn_w2[0].astype(BF16), row(ln_mix_g[0]), row(ln_mix_b[0]), row(ln_ffn_g[0]),
                           row(ln_ffn_b[0]), h_off=0, n_out_tiles=n_ctx_tiles + n_lat_tiles,
                           n_ctx_tiles=n_ctx_tiles, alpha=alpha, name="outproj_ffn_0")

    conv_dim = gw + 2 * SSD_GROUPS * SSD_STATE
    ssd_cols = gw + conv_dim + 2 * SSD_HEADS
    kvw = SWA_KV_HEADS * HEAD_DIM
    wcd = cd_w_in[0]
    group = SWA_HEADS // SWA_KV_HEADS
    q_perm = jnp.arange(gw, dtype=jnp.int32).reshape(SWA_KV_HEADS, group, HEAD_DIM).transpose(1, 0, 2).reshape(-1)
    w1cat = jnp.concatenate([wcd[:, ssd_cols:ssd_cols + gw][:, q_perm], wcd[:, ssd_cols + gw:], wcd[:, :gw],
                             wcd[:, gw:gw + conv_dim], wcd[:, gw + conv_dim:ssd_cols],
                             jnp.zeros((d, LANES - 2 * SSD_HEADS), F32)], 1)
    w_out1 = jnp.concatenate([w_out[1, :gw], w_out[1, gw:][q_perm]], 0)
    c_q, c_k, c_v, c_z, c_x, c_dt = 0, gw, gw + kvw, gw + 2 * kvw, 2 * gw + 2 * kvw, 2 * gw + 2 * kvw + conv_dim
    plan1 = ((c_q, gw, True, q_scale, False), (c_k, kvw, True, 1.0, False), (c_v, kvw, False, 1.0, False),
             (c_z, gw, False, 1.0, False), (c_x, conv_dim, False, 1.0, False), (c_dt, LANES, False, 1.0, False))
    sq, sk, sv, pz, xbc, dt_raw = _inproj_call((h1,), modsel(1), w1cat.astype(BF16), cos_t, sin_t, plan1,
                                               (BF16, BF16, BF16, F32, F32, F32), n_ctx_tiles, "inproj_cd")

    pad_l = lambda t: jnp.concatenate([t, jnp.zeros((LANES - t.shape[0],), F32)])[None, :]
    a_neg = -jnp.exp(ssd_a_log[0].astype(F32))
    conv_w_pad = jnp.concatenate([ssd_conv_w[0], jnp.zeros((8 - SSD_CONV, conv_dim), F32)], 0)
    xc, dt_sp, adt = _ssd_prep_call(xbc, dt_raw, conv_w_pad, ssd_conv_b[0][None, :], pad_l(ssd_dt_bias[0].reshape(-1)),
                                    pad_l(a_neg.reshape(-1)), n_ctx_tiles)
    adt_t = jnp.swapaxes(adt[:, :, :2 * SSD_HEADS], 1, 2)
    jj = jnp.arange(LANES, dtype=jnp.int32)[:, None]
    hh = (jnp.arange(gw, dtype=jnp.int32) // HEAD_DIM)[None, :]
    e_sel = jnp.stack([(jj == hh), (jj == hh + SSD_HEADS)]).astype(BF16)
    ys_f, ys_r = _ssd_scan_call(xc, dt_sp, adt, adt_t, e_sel, n_ctx)
    o2 = _swa_call(swa_sink[0].astype(F32) * log2e, sq, sk, sv, n_ctx)

    return _outproj_ffn_call(_ssd_mixer_out, (ys_f, ys_r, xc, pz),
                             (row(jnp.repeat(ssd_d[0], HEAD_DIM)), row(ssd_norm_g[0])), o2, (h1,),
                             modsel(1), w_out1.astype(BF16), ffn_w1[1].astype(BF16),
                             ffn_w2[1].astype(BF16), row(ln_mix_g[1]), row(ln_mix_b[1]), row(ln_ffn_g[1]),
                             row(ln_ffn_b[1]), h_off=n_ctx_tiles, n_out_tiles=n_lat_tiles,
                             n_ctx_tiles=n_ctx_tiles, alpha=alpha, name="outproj_ffn_1")
```

```python
import functools
import math

import jax
import jax.numpy as jnp
from jax import lax
from jax.experimental import pallas as pl
from jax.experimental.pallas import tpu as pltpu

F32 = jnp.float32
BF16 = jnp.bfloat16

HEAD_DIM = 64
GRID_W = 64
ROPE_BASE = 10000.0
LN_EPS = 1e-5
RWKV_GN_EPS = 64e-5
RWKV_LORA_W = 64
RWKV_LORA_A = 64
RWKV_LORA_G = 128
DIFF_HEADS = 4
SSD_HEADS = 8
SSD_GROUPS = 2
SSD_STATE = 128
SSD_CONV = 5
SWA_HEADS = 8
SWA_KV_HEADS = 2
SWA_WINDOW = 128

ROW_TILE = 256
BATCH_PACK = 2
HALO = 8
SCAN_BATCH_PACK = 4
RWKV_CHUNK = 64
SSD_CHUNK = 128
SWA_TILE = 128
LANES = 128
VMEM_LIMIT = 56 * 1024 * 1024


def _cparams(n_axes):
    return pltpu.CompilerParams(dimension_semantics=("arbitrary",) * n_axes, vmem_limit_bytes=VMEM_LIMIT)


def _dot(a, b):
    return jnp.dot(a, b, preferred_element_type=F32)


def _dot_nt(a, b):
    return lax.dot_general(a, b, (((1,), (1,)), ((), ())), preferred_element_type=F32)


def _split_bf16(a, pieces):
    out = []
    for i in range(pieces):
        part = a.astype(BF16)
        out.append(part)
        if i + 1 < pieces:
            a = a - part.astype(F32)
    return out


def _dot_f32_by_exact(a, b_exact, pieces=3):
    acc = None
    for part in _split_bf16(a, pieces):
        t = _dot(part, b_exact)
        acc = t if acc is None else acc + t
    return acc


def _dot_exact_by_f32(a_exact, b, pieces=3):
    acc = None
    for part in _split_bf16(b, pieces):
        t = _dot(a_exact, part)
        acc = t if acc is None else acc + t
    return acc


def _dot_hilo(a, b):
    ah = a.astype(BF16)
    al = (a - ah.astype(F32)).astype(BF16)
    bh = b.astype(BF16)
    bl = (b - bh.astype(F32)).astype(BF16)
    return _dot(ah, bh) + _dot(ah, bl) + _dot(al, bh)


def _split_hilo(w):
    hi = w.astype(BF16)
    return jnp.stack([hi, (w - hi.astype(F32)).astype(BF16)])


def _dot_hilo_presplit(a, w_ref):
    ah = a.astype(BF16)
    al = (a - ah.astype(F32)).astype(BF16)
    return _dot(ah, w_ref[0]) + _dot(ah, w_ref[1]) + _dot(al, w_ref[0])


def _sigmoid(x):
    return 0.5 * jnp.tanh(0.5 * x) + 0.5


def _softplus(x):
    return jnp.maximum(x, 0.0) + jnp.log(1.0 + jnp.exp(-jnp.abs(x)))


def _layer_norm(y, g, b, eps):
    mu = jnp.mean(y, -1, keepdims=True)
    yc = y - mu
    var = jnp.mean(yc * yc, -1, keepdims=True)
    return yc * lax.rsqrt(var + eps) * g + b


def _mod_kernel(c_ref, w_ref, b_ref, o_ref):
    c = c_ref[...]
    o_ref[0] = _dot_hilo(c * _sigmoid(c), w_ref[0]) + b_ref[0]


def _mod_call(c_pad, mod_w, mod_b):
    depth, d, n = mod_w.shape
    rows = c_pad.shape[0]
    tn = 1536
    return pl.pallas_call(
        _mod_kernel,
        grid=(depth, n // tn),
        in_specs=[pl.BlockSpec((rows, d), lambda i, j: (0, 0)),
                  pl.BlockSpec((1, d, tn), lambda i, j: (i, 0, j)),
                  pl.BlockSpec((1, 1, tn), lambda i, j: (i, 0, j))],
        out_specs=pl.BlockSpec((1, rows, tn), lambda i, j: (i, 0, j)),
        out_shape=jax.ShapeDtypeStruct((depth, rows, n), F32),
        compiler_params=_cparams(2),
        name="adaln_mod",
    )(c_pad, mod_w, mod_b.reshape(depth, 1, n))


def _rope(x, cos, sin):
    lane = lax.broadcasted_iota(jnp.int32, (x.shape[0], LANES), 1)
    first = (lane % 32) < 16
    parts = []
    for g in range(0, x.shape[1], LANES):
        xg = x[:, g:g + LANES]
        parts.append(jnp.where(first, pltpu.roll(xg, LANES - 16, 1), pltpu.roll(xg, 16, 1)))
    sw = parts[0] if len(parts) == 1 else jnp.concatenate(parts, 1)
    return x * cos + sw * sin


def _stream_specs(streams, n_ctx_tiles):
    d = streams[0].shape[-1]
    bp = BATCH_PACK
    if len(streams) == 1:
        return [pl.BlockSpec((bp, ROW_TILE, d), lambda bi, t: (bi, t, 0))]
    return [pl.BlockSpec((bp, ROW_TILE, d), lambda bi, t: (bi, jnp.minimum(t, n_ctx_tiles - 1), 0)),
            pl.BlockSpec((bp, ROW_TILE, d), lambda bi, t: (bi, jnp.maximum(t - n_ctx_tiles, 0), 0))]


def _stream_tile(refs, n_ctx_tiles):
    if len(refs) == 1:
        return refs[0][...]
    return jnp.where(pl.program_id(1) < n_ctx_tiles, refs[0][...], refs[1][...])


def _inproj_kernel(*refs, plan, n_streams, n_ctx_tiles):
    x_refs = refs[:n_streams]
    sh_ref, sc_ref, w_ref, cos_ref, sin_ref = refs[n_streams:n_streams + 5]
    o_refs = refs[n_streams + 5:]
    x = _stream_tile(x_refs, n_ctx_tiles)
    bp, rows, d = x.shape
    xm = (x * (1.0 + sc_ref[:, 0, 0]) + sh_ref[:, 0, 0]).reshape(bp * rows, d).astype(BF16)
    cos = jnp.concatenate([cos_ref[...]] * bp, 0)
    sin = jnp.concatenate([sin_ref[...]] * bp, 0)
    for o_ref, (c0, width, rope, mult, transpose) in zip(o_refs, plan):
        for j in range(0, width, 256):
            cw = min(256, width - j)
            acc = _dot(xm, w_ref[:, c0 + j:c0 + j + cw])
            if rope:
                acc = _rope(acc, cos[:, :cw], sin[:, :cw])
            if mult != 1.0:
                acc = acc * mult
            if transpose:
                for p in range(bp):
                    o_ref[p, j:j + cw, :] = acc[p * rows:(p + 1) * rows].T.astype(o_ref.dtype)
            else:
                o_ref[:, :, j:j + cw] = acc.reshape(bp, rows, cw).astype(o_ref.dtype)


def _inproj_call(streams, modsel, w_bf16, cos_t, sin_t, plan, out_dtypes, n_ctx_tiles, name):
    b, _, d = streams[0].shape
    l = sum(s.shape[1] for s in streams)
    n_tiles = l // ROW_TILE
    seg = lambda bi, t: (bi, (t >= n_ctx_tiles).astype(jnp.int32), 0, 0, 0)
    seg_scale = lambda bi, t: (bi, (t >= n_ctx_tiles).astype(jnp.int32), 1, 0, 0)
    out_shape, out_specs = [], []
    bp = BATCH_PACK
    for (_, width, _, _, transpose), dt in zip(plan, out_dtypes):
        if transpose:
            out_shape.append(jax.ShapeDtypeStruct((b, width, l), dt))
            out_specs.append(pl.BlockSpec((bp, width, ROW_TILE), lambda bi, t: (bi, 0, t)))
        else:
            out_shape.append(jax.ShapeDtypeStruct((b, l, width), dt))
            out_specs.append(pl.BlockSpec((bp, ROW_TILE, width), lambda bi, t: (bi, t, 0)))
    return pl.pallas_call(
        functools.partial(_inproj_kernel, plan=plan, n_streams=len(streams), n_ctx_tiles=n_ctx_tiles),
        grid=(b // bp, n_tiles),
        in_specs=_stream_specs(streams, n_ctx_tiles) + [
            pl.BlockSpec((bp, 1, 1, 1, d), seg),
            pl.BlockSpec((bp, 1, 1, 1, d), seg_scale),
            pl.BlockSpec(w_bf16.shape, lambda bi, t: (0, 0), pipeline_mode=pl.Buffered(1)),
            pl.BlockSpec((ROW_TILE, cos_t.shape[1]), lambda bi, t: (t, 0)),
            pl.BlockSpec((ROW_TILE, sin_t.shape[1]), lambda bi, t: (t, 0))],
        out_specs=out_specs,
        out_shape=out_shape,
        compiler_params=_cparams(2),
        name=name,
    )(*streams, modsel, modsel, w_bf16, cos_t, sin_t)


def _rwkv_prep_kernel(p_ref, hp_ref, hn_ref, mu_ref, bias_ref, lora_ref, g2_ref, kk_ref, ka_ref, rk_ref, bd_ref,
                      at_o, bt_o, kt_o, rt_o, pe_o, v_o, g_o, bonus_o, *, n_ctx_tiles, n_tiles, chunk):
    t = pl.program_id(1)
    gw = v_o.shape[-1]
    p = p_ref[0]
    rows = p.shape[0]
    first = jnp.logical_or(t == 0, t == n_ctx_tiles)
    last = jnp.logical_or(t == n_ctx_tiles - 1, t == n_tiles - 1)
    prev_row = jnp.where(first, 0.0, hp_ref[0, HALO - 1:HALO, :])
    next_row = jnp.where(last, 0.0, hn_ref[0, 0:1, :])
    rowi = lax.broadcasted_iota(jnp.int32, p.shape, 0)
    prev = jnp.where(rowi == 0, prev_row, pltpu.roll(p, 1, 0))
    nxt = jnp.where(rowi == rows - 1, next_row, pltpu.roll(p, rows - 1, 0))
    ps = p + mu_ref[...] * (0.5 * (prev + nxt) - p)

    r = ps[:, :gw]
    k = ps[:, gw:2 * gw]
    v = ps[:, 2 * gw:3 * gw]
    slab = ps[:, 3 * gw:3 * gw + LANES]
    xg = ps[:, 3 * gw + LANES:3 * gw + 2 * LANES]
    lane = lax.broadcasted_iota(jnp.int32, slab.shape, 1)
    lora_in = jnp.where(lane < RWKV_LORA_W, jnp.tanh(slab), slab)
    pre = _dot_hilo_presplit(lora_in, lora_ref) + bias_ref[...]
    bd = bd_ref[...]
    g_o[0] = _dot_hilo_presplit(_sigmoid(xg), g2_ref)
    kk = k * kk_ref[...]
    ss = _dot_f32_by_exact(kk * kk, bd, 2)
    kk = kk * lax.rsqrt(jnp.maximum(ss, 1e-24))
    bonus_o[0] = _dot_f32_by_exact(r * k * rk_ref[...], bd, 2) * v
    v_o[0] = v.astype(v_o.dtype)
    ri = lax.broadcasted_iota(jnp.int32, (rows, rows), 0)
    ci = lax.broadcasted_iota(jnp.int32, (rows, rows), 1)
    same_chunk = (ri // chunk) == (ci // chunk)
    for d in range(2):
        lw = -math.exp(-0.5) * _sigmoid(pre[:, d * gw:(d + 1) * gw])
        a = _sigmoid(pre[:, (2 + d) * gw:(3 + d) * gw])
        before = (ci <= ri) if d == 0 else (ci >= ri)
        tri = jnp.logical_and(same_chunk, before).astype(BF16)
        cum = _dot_exact_by_f32(tri, lw, 2)
        p_inv = jnp.exp(-cum)
        at_o[0, d] = (-kk * jnp.exp(cum - lw)).astype(at_o.dtype)
        bt_o[0, d] = (kk * a * p_inv).astype(bt_o.dtype)
        kt_o[0, d] = (k * (1.0 + (a - 1.0) * ka_ref[...]) * p_inv).astype(kt_o.dtype)
        rt_o[0, d] = (r * jnp.exp(cum)).astype(rt_o.dtype)
        for j in range(rows // chunk):
            last_row = j * chunk + (chunk - 1 if d == 0 else 0)
            pe_o[0, d, j] = jnp.exp(cum[last_row:last_row + 1, :])


def _rwkv_prep_call(p, mu, bias, lora_w, g2, k_k, k_a, r_k, bd, n_ctx_tiles):
    b, l, cols = p.shape
    gw = k_k.shape[-1]
    n_tiles = l // ROW_TILE
    hb = ROW_TILE // HALO
    n_hb = l // HALO
    cpt = ROW_TILE // RWKV_CHUNK
    full = lambda a: pl.BlockSpec(a.shape, lambda bi, t: (0,) * a.ndim)
    tok = pl.BlockSpec((1, ROW_TILE, gw), lambda bi, t: (bi, t, 0))
    tok2 = pl.BlockSpec((1, 2, ROW_TILE, gw), lambda bi, t: (bi, 0, t, 0))
    s1 = jax.ShapeDtypeStruct((b, l, gw), F32)
    s2 = jax.ShapeDtypeStruct((b, 2, l, gw), BF16)
    return pl.pallas_call(
        functools.partial(_rwkv_prep_kernel, n_ctx_tiles=n_ctx_tiles, n_tiles=n_tiles, chunk=RWKV_CHUNK),
        grid=(b, n_tiles),
        in_specs=[pl.BlockSpec((1, ROW_TILE, cols), lambda bi, t: (bi, t, 0)),
                  pl.BlockSpec((1, HALO, cols), lambda bi, t: (bi, jnp.maximum(t * hb - 1, 0), 0)),
                  pl.BlockSpec((1, HALO, cols), lambda bi, t: (bi, jnp.minimum((t + 1) * hb, n_hb - 1), 0)),
                  full(mu), full(bias), full(lora_w), full(g2), full(k_k), full(k_a), full(r_k), full(bd)],
        out_specs=[tok2, tok2, tok2, tok2,
                   pl.BlockSpec((1, 2, cpt, 1, gw), lambda bi, t: (bi, 0, t, 0, 0)), tok, tok, tok],
        out_shape=[s2, s2, s2, s2, jax.ShapeDtypeStruct((b, 2, l // RWKV_CHUNK, 1, gw), F32),
                   jax.ShapeDtypeStruct((b, l, gw), BF16), s1, s1],
        compiler_params=_cparams(2),
        name="rwkv_prep",
    )(p, p, p, mu, bias, lora_w, g2, k_k, k_a, r_k, bd)


def _rwkv_scan_kernel(atf_ref, btf_ref, ktf_ref, rtf_ref, pef_ref, vf_ref, atr_ref, btr_ref, ktr_ref, rtr_ref, per_ref,
                      vr_ref, yf_ref, yr_ref, s_ref):
    i = pl.program_id(1)
    c = vf_ref.shape[1]
    n_pairs = vf_ref.shape[2] // LANES
    c2 = 2 * c

    @pl.when(i == 0)
    def _():
        s_ref[...] = jnp.zeros_like(s_ref)

    row2 = lax.broadcasted_iota(jnp.int32, (c2, c2), 0)
    col2 = lax.broadcasted_iota(jnp.int32, (c2, c2), 1)
    same = (row2 // c) == (col2 // c)
    eye = (row2 == col2).astype(F32)
    low = lax.broadcasted_iota(jnp.int32, (c, LANES), 1) < HEAD_DIM

    def stack(x):
        xf = x.astype(F32)
        return jnp.concatenate([jnp.where(low, xf, 0.0), jnp.where(low, 0.0, xf)], 0)

    chains = []
    dirs = ((1, atf_ref, btf_ref, ktf_ref, rtf_ref, pef_ref, vf_ref, yf_ref),
            (-1, atr_ref, btr_ref, ktr_ref, rtr_ref, per_ref, vr_ref, yr_ref))
    for d, (sgn, at_ref, bt_ref, kt_ref, rt_ref, pe_ref, v_ref, y_ref) in enumerate(dirs):
        dt2 = (row2 - col2) * sgn
        strict = jnp.logical_and(same, dt2 > 0)
        incl = jnp.logical_and(same, dt2 >= 0)
        for p in range(vf_ref.shape[0]):
            for pr in range(n_pairs):
                sl = slice(pr * LANES, (pr + 1) * LANES)
                b_s = bt_ref[p, 0, :, sl]
                k_s = kt_ref[p, 0, :, sl]
                chains.append(dict(st=(p, d, pr), p=p, sl=sl, y_ref=y_ref, strict=strict, incl=incl,
                                   p_end=pe_ref[p, 0, 0, :, sl], vh=stack(v_ref[p, :, sl]),
                                   ar=jnp.concatenate([stack(at_ref[p, 0, :, sl]), stack(rt_ref[p, 0, :, sl])],
                                                      0).astype(BF16),
                                   bk=jnp.concatenate([b_s, b_s, k_s, k_s], 0)))

    for ch in chains:
        gram = _dot_nt(ch["ar"], ch["bk"])
        l_ab = jnp.where(ch["strict"], gram[:c2, :c2], 0.0)
        ch["l_ak"] = jnp.where(ch["strict"], gram[:c2, c2:], 0.0).astype(BF16)
        ch["l_r"] = jnp.concatenate([jnp.where(ch["incl"], gram[c2:, :c2], 0.0),
                                     jnp.where(ch["incl"], gram[c2:, c2:], 0.0)], 1).astype(BF16)
        ch["inv"] = eye + l_ab
        ch["pw"] = l_ab.astype(BF16)
    for ch in chains:
        ch["pw"] = _dot(ch["pw"], ch["pw"]).astype(BF16)
    for _ in range(max(0, int(math.ceil(math.log2(c))) - 2)):
        for ch in chains:
            both = _dot(ch["pw"], jnp.concatenate([ch["inv"].astype(BF16), ch["pw"]], 1))
            ch["inv"] = ch["inv"] + both[:, :c2]
            ch["pw"] = both[:, c2:].astype(BF16)
    for ch in chains:
        ch["inv"] = (ch["inv"] + _dot(ch["pw"], ch["inv"].astype(BF16))).astype(BF16)
    for ch in chains:
        ch["s0"] = s_ref[ch["st"]]
        ch["ars"] = _dot_nt(ch["ar"], ch["s0"].astype(BF16))
        ch["lv"] = _dot(ch["l_ak"], ch["vh"].astype(BF16))
    for ch in chains:
        u = _dot(ch["inv"], (ch["ars"][:c2] + ch["lv"]).astype(BF16))
        ch["uv"] = jnp.concatenate([u, ch["vh"]], 0)
    for ch in chains:
        y = ch["ars"][c2:] + _dot(ch["l_r"], ch["uv"].astype(BF16))
        ch["y_ref"][ch["p"], :, ch["sl"]] = y[:c] + y[c:]
    for ch in chains:
        upd = (ch["s0"] + _dot(ch["uv"].T.astype(BF16), ch["bk"])) * ch["p_end"]
        s_ref[ch["st"]] = jnp.where(same, upd, 0.0)


def _rwkv_scan_call(at, bt, kt, rt, pe, v, n_ctx):
    b, l, gw = v.shape
    c = RWKV_CHUNK
    nc = l // c
    ncc = n_ctx // c

    def rev(i):
        return jnp.where(i < ncc, ncc - 1 - i, nc - 1 - (i - ncc))

    bp = SCAN_BATCH_PACK if b % SCAN_BATCH_PACK == 0 else BATCH_PACK

    def specs(d, ch):
        tokd = pl.BlockSpec((bp, 1, c, gw), lambda bi, i: (bi, d, ch(i), 0))
        return [tokd, tokd, tokd, tokd, pl.BlockSpec((bp, 1, 1, 1, gw), lambda bi, i: (bi, d, ch(i), 0, 0)),
                pl.BlockSpec((bp, c, gw), lambda bi, i: (bi, ch(i), 0))]

    ys = jax.ShapeDtypeStruct((b, l, gw), F32)
    return pl.pallas_call(
        _rwkv_scan_kernel,
        grid=(b // bp, nc),
        in_specs=specs(0, lambda i: i) + specs(1, rev),
        out_specs=[pl.BlockSpec((bp, c, gw), lambda bi, i: (bi, i, 0)),
                   pl.BlockSpec((bp, c, gw), lambda bi, i: (bi, rev(i), 0))],
        out_shape=[ys, ys],
        scratch_shapes=[pltpu.VMEM((bp, 2, gw // LANES, LANES, LANES), F32)],
        compiler_params=_cparams(2),
        name="rwkv_scan",
    )(at, bt, kt, rt, pe, v, at, bt, kt, rt, pe, v)


def _rwkv_mixer_out(yf_ref, yr_ref, bonus_ref, g_ref, gg_ref, gb_ref):
    flat = lambda r: r[...].reshape(-1, r.shape[-1])
    y = flat(yf_ref) + flat(yr_ref)
    low = lax.broadcasted_iota(jnp.int32, (y.shape[0], LANES), 1) < HEAD_DIM

    def head_mean(t):
        tot = jnp.sum(t, -1, keepdims=True)
        lo = jnp.sum(jnp.where(low, t, 0.0), -1, keepdims=True)
        return jnp.where(low, lo, tot - lo) * (1.0 / HEAD_DIM)

    outs = []
    for s in range(0, y.shape[1], LANES):
        ys = y[:, s:s + LANES]
        yc = ys - head_mean(ys)
        outs.append(yc * lax.rsqrt(head_mean(yc * yc) + RWKV_GN_EPS))
    yn = jnp.concatenate(outs, 1) * gg_ref[...] + gb_ref[...]
    return (yn + flat(bonus_ref)) * flat(g_ref)


def _diff_attn_kernel(lam_ref, q_ref, k_ref, vt_ref, g_ref, o_ref, *, n_ctx, n_ctx_tiles, out_scale):
    t = pl.program_id(1)
    lam = lam_ref[0]
    vd = 2 * HEAD_DIM
    tq = q_ref.shape[1]
    low = lax.broadcasted_iota(jnp.int32, (tq, vd), 1) < HEAD_DIM

    def run(nk):
        def logits(u):
            p, hm = divmod(u, 2 * DIFF_HEADS)
            h, m = divmod(hm, 2)
            qf = q_ref[p, :, h * vd:(h + 1) * vd].astype(F32)
            qm = jnp.where(low == (m == 0), qf, 0.0).astype(BF16)
            return _dot_nt(k_ref[p, :nk, h * vd:(h + 1) * vd], qm)

        n_units = 2 * DIFF_HEADS * q_ref.shape[0]
        s_next = logits(0)
        es, invs = [], []
        for u in range(n_units):
            s = s_next
            if u + 1 < n_units:
                s_next = logits(u + 1)
            e = jnp.exp2(s - jnp.max(s, 0, keepdims=True))
            invs.append(1.0 / jnp.sum(e, 0, keepdims=True))
            es.append(e.astype(BF16))
            if u % 2 == 1:
                p, hm = divmod(u, 2 * DIFF_HEADS)
                h = hm // 2
                pv = _dot(vt_ref[p, h * vd:(h + 1) * vd, :nk], jnp.concatenate(es, 1))
                o = pv[:, :tq] * invs[0] - pv[:, tq:] * (lam * invs[1])
                o = o * lax.rsqrt(jnp.mean(o * o, 0, keepdims=True) + LN_EPS) * (g_ref[...] * out_scale)
                o_ref[p, :, h * vd:(h + 1) * vd] = o.T
                es, invs = [], []

    @pl.when(t < n_ctx_tiles)
    def _():
        run(n_ctx)

    @pl.when(t >= n_ctx_tiles)
    def _():
        run(k_ref.shape[1])


def _diff_attn_call(lam, q, k, vt, subln_g_col, n_ctx, lam_init):
    b, l, w = q.shape
    bp = BATCH_PACK
    tok = pl.BlockSpec((bp, ROW_TILE, w), lambda bi, t: (bi, t, 0))
    return pl.pallas_call(
        functools.partial(_diff_attn_kernel, n_ctx=n_ctx, n_ctx_tiles=n_ctx // ROW_TILE, out_scale=1.0 - lam_init),
        grid=(b // bp, l // ROW_TILE),
        in_specs=[pl.BlockSpec(memory_space=pltpu.SMEM), tok,
                  pl.BlockSpec((bp, l, w), lambda bi, t: (bi, 0, 0)),
                  pl.BlockSpec((bp, w, l), lambda bi, t: (bi, 0, 0)),
                  pl.BlockSpec(subln_g_col.shape, lambda bi, t: (0, 0))],
        out_specs=tok,
        out_shape=jax.ShapeDtypeStruct((b, l, w), F32),
        compiler_params=_cparams(2),
        name="diff_attn",
    )(lam, q, k, vt, subln_g_col)


def _outproj_ffn_kernel(*refs, alpha, n_streams, n_ctx_tiles, mixer_out, n_mix):
    o2_ref = refs[n_mix]
    h_refs = refs[n_mix + 1:n_mix + 1 + n_streams]
    (gm_ref, sh_ref, sc_ref, gf_ref, wo_ref, w1_ref, w2_ref, lmg_ref, lmb_ref, lfg_ref, lfb_ref,
     out_ref) = refs[n_mix + 1 + n_streams:]
    bp, rows, gw = o2_ref.shape
    flat = lambda t: t.reshape(bp * rows, t.shape[-1])
    per_b = lambda t: t.reshape(bp, rows, t.shape[-1])
    o1 = mixer_out(*refs[:n_mix])
    o = _dot(o1.astype(BF16), wo_ref[:gw, :]) + _dot(flat(o2_ref[...]).astype(BF16), wo_ref[gw:, :])
    h_in = _stream_tile(h_refs, n_ctx_tiles)
    h1 = _layer_norm(flat(alpha * h_in + gm_ref[:, 0, 0] * per_b(o)), lmg_ref[...], lmb_ref[...], LN_EPS)
    xm = flat(per_b(h1) * (1.0 + sc_ref[:, 0, 0]) + sh_ref[:, 0, 0]).astype(BF16)
    hidden = w1_ref.shape[1]
    step = 512

    def up(j):
        hj = jnp.maximum(_dot(xm, w1_ref[:, j:j + step]), 0.0)
        return (hj * hj).astype(BF16)

    acc = None
    act = up(0)
    for j in range(0, hidden, step):
        nxt = up(j + step) if j + step < hidden else None
        part = _dot(act, w2_ref[j:j + step, :])
        acc = part if acc is None else acc + part
        act = nxt
    out = _layer_norm(alpha * h1 + flat(gf_ref[:, 0, 0] * per_b(acc)), lfg_ref[...], lfb_ref[...], LN_EPS)
    out_ref[...] = per_b(out)


def _outproj_ffn_call(mixer_out, mix_tok, mix_par, o2, streams, modsel, wo, w1, w2, lmg, lmb, lfg, lfb, *, h_off,
                      n_out_tiles, n_ctx_tiles, alpha, name):
    b, _, gw = o2.shape
    d = streams[0].shape[-1]
    bp = BATCH_PACK
    assert len(streams) == 1 or h_off == 0
    seg = lambda col: (lambda bi, t: (bi, ((t + h_off) >= n_ctx_tiles).astype(jnp.int32), col, 0, 0))
    modspec = lambda col: pl.BlockSpec((bp, 1, 1, 1, d), seg(col))
    full = lambda a: pl.BlockSpec(a.shape, lambda bi, t: (0,) * a.ndim, pipeline_mode=pl.Buffered(1))
    mtok = pl.BlockSpec((bp, ROW_TILE, gw), lambda bi, t: (bi, t + h_off, 0))
    otok = pl.BlockSpec((bp, ROW_TILE, gw), lambda bi, t: (bi, t, 0))
    if len(streams) == 1:
        h_specs = [pl.BlockSpec((bp, ROW_TILE, d), lambda bi, t: (bi, t + h_off, 0))]
    else:
        h_specs = _stream_specs(streams, n_ctx_tiles)
    n_mix = len(mix_tok) + len(mix_par)
    return pl.pallas_call(
        functools.partial(_outproj_ffn_kernel, alpha=alpha, n_streams=len(streams), n_ctx_tiles=n_ctx_tiles,
                          mixer_out=mixer_out, n_mix=n_mix),
        grid=(b // bp, n_out_tiles),
        in_specs=[mtok] * len(mix_tok) + [full(p) for p in mix_par] + [otok] + h_specs + [
            modspec(2), modspec(3), modspec(4), modspec(5),
            full(wo), full(w1), full(w2), full(lmg), full(lmb), full(lfg), full(lfb)],
        out_specs=pl.BlockSpec((bp, ROW_TILE, d), lambda bi, t: (bi, t, 0)),
        out_shape=jax.ShapeDtypeStruct((b, n_out_tiles * ROW_TILE, d), F32),
        compiler_params=_cparams(2),
        name=name,
    )(*mix_tok, *mix_par, o2, *streams, modsel, modsel, modsel, modsel, wo, w1, w2, lmg, lmb, lfg, lfb)


def _ssd_prep_kernel(x_ref, hp_ref, hn_ref, dt_ref, cw_ref, cb_ref, dtb_ref, an_ref, xc_o, dt_o, adt_o,
                     *, n_ctx_tiles, n_tiles):
    t = pl.program_id(1)
    x = x_ref[0]
    rows = x.shape[0]
    first = jnp.logical_or(t == 0, t == n_ctx_tiles)
    last = jnp.logical_or(t == n_ctx_tiles - 1, t == n_tiles - 1)
    ext = jnp.concatenate([jnp.where(first, 0.0, hp_ref[0]), x, jnp.where(last, 0.0, hn_ref[0])], 0)
    n_ext = rows + 2 * HALO
    acc = cb_ref[...] + cw_ref[SSD_CONV // 2:SSD_CONV // 2 + 1, :] * x
    for j in range(SSD_CONV):
        s = j - SSD_CONV // 2
        if s == 0:
            continue
        shifted = pltpu.roll(ext, (n_ext - s) % n_ext, 0)[HALO:HALO + rows]
        acc = acc + cw_ref[j:j + 1, :] * shifted
    xc_o[0] = acc * _sigmoid(acc)
    dt = _softplus(dt_ref[0] + dtb_ref[...])
    dt_o[0] = dt
    adt_o[0] = dt * an_ref[...]


def _ssd_prep_call(xbc, dt, conv_w, conv_b, dt_bias, a_neg, n_ctx_tiles):
    b, l, cw = xbc.shape
    n_tiles = l // ROW_TILE
    hb = ROW_TILE // HALO
    n_hb = l // HALO
    full = lambda a: pl.BlockSpec(a.shape, lambda bi, t: (0,) * a.ndim)
    tokd = pl.BlockSpec((1, ROW_TILE, LANES), lambda bi, t: (bi, t, 0))
    sd = jax.ShapeDtypeStruct((b, l, LANES), F32)
    return pl.pallas_call(
        functools.partial(_ssd_prep_kernel, n_ctx_tiles=n_ctx_tiles, n_tiles=n_tiles),
        grid=(b, n_tiles),
        in_specs=[pl.BlockSpec((1, ROW_TILE, cw), lambda bi, t: (bi, t, 0)),
                  pl.BlockSpec((1, HALO, cw), lambda bi, t: (bi, jnp.maximum(t * hb - 1, 0), 0)),
                  pl.BlockSpec((1, HALO, cw), lambda bi, t: (bi, jnp.minimum((t + 1) * hb, n_hb - 1), 0)),
                  tokd, full(conv_w), full(conv_b), full(dt_bias), full(a_neg)],
        out_specs=[pl.BlockSpec((1, ROW_TILE, cw), lambda bi, t: (bi, t, 0)), tokd, tokd],
        out_shape=[jax.ShapeDtypeStruct((b, l, cw), F32), sd, sd],
        compiler_params=_cparams(2),
        name="ssd_prep",
    )(xbc, xbc, xbc, dt, conv_w, conv_b, dt_bias, a_neg)


def _ssd_scan_kernel(xcf_ref, dtf_ref, adtf_ref, adttf_ref, xcr_ref, dtr_ref, adtr_ref, adttr_ref, e_ref,
                     yf_ref, yr_ref, st_ref):
    i = pl.program_id(1)
    q = xcf_ref.shape[1]
    gw = yf_ref.shape[-1]
    n = SSD_STATE
    hpg = SSD_HEADS // SSD_GROUPS
    gcols = hpg * HEAD_DIM

    @pl.when(i == 0)
    def _():
        st_ref[...] = jnp.zeros_like(st_ref)

    row = lax.broadcasted_iota(jnp.int32, (q, q), 0)
    col = lax.broadcasted_iota(jnp.int32, (q, q), 1)
    chains = []
    dirs = ((1, xcf_ref, dtf_ref, adtf_ref, adttf_ref, yf_ref), (-1, xcr_ref, dtr_ref, adtr_ref, adttr_ref, yr_ref))
    for p, d in [(p, d) for p in range(xcf_ref.shape[0]) for d in range(2)]:
        sgn, xc_ref, dt_ref, adt_ref, adtt_ref, y_ref = dirs[d]
        incl = (row - col) * sgn >= 0
        tri = incl.astype(BF16)
        tri_t = ((col - row) * sgn >= 0).astype(BF16)
        cs = _dot_exact_by_f32(tri, adt_ref[p])
        cs_t = _dot_f32_by_exact(adtt_ref[p], tri_t)
        e = e_ref[d]
        cse = _dot_f32_by_exact(cs, e)
        dte = _dot_f32_by_exact(dt_ref[p], e, 2)
        tot = cse[q - 1:q, :] if sgn > 0 else cse[0:1, :]
        xc = xc_ref[p]
        xdt = xc[:, :gw] * dte
        e_cs = jnp.exp(cse)
        x_end = (xdt * jnp.exp(tot - cse)).astype(BF16)
        e_tot = jnp.exp(tot)
        xdt16 = xdt.astype(BF16)
        for g in range(SSD_GROUPS):
            gsl = slice(g * gcols, (g + 1) * gcols)
            lms = []
            for hh in range(hpg):
                h = g * hpg + hh
                colv = cse[:, h * HEAD_DIM:h * HEAD_DIM + 1]
                rowv = cs_t[d * SSD_HEADS + h:d * SSD_HEADS + h + 1, :]
                lms.append(jnp.where(incl, jnp.exp(colv - rowv), 0.0))
            chains.append(dict(st=(p, d, g), p=p, gsl=gsl, y_ref=y_ref, lms=lms, e_cs=e_cs[:, gsl], e_tot=e_tot[:, gsl],
                               x_end=x_end[:, gsl], xdt16=xdt16[:, gsl],
                               bg=xc[:, gw + g * n:gw + (g + 1) * n],
                               cg=xc[:, gw + (SSD_GROUPS + g) * n:gw + (SSD_GROUPS + g + 1) * n].astype(BF16)))
    for ch in chains:
        ch["cb"] = _dot_nt(ch["cg"], ch["bg"].astype(BF16))
        ch["s0"] = st_ref[ch["st"]]
    for ch in chains:
        ch["y_off"] = _dot(ch["cg"], ch["s0"].astype(BF16)) * ch["e_cs"]
    for ch in chains:
        ch["ys"] = [_dot((ch["cb"] * ch["lms"][hh]).astype(BF16), ch["xdt16"][:, hh * HEAD_DIM:(hh + 1) * HEAD_DIM])
                    for hh in range(hpg)]
    for ch in chains:
        ch["y_ref"][ch["p"], :, ch["gsl"]] = jnp.concatenate(ch["ys"], 1) + ch["y_off"]
        st_ref[ch["st"]] = ch["s0"] * ch["e_tot"] + _dot(ch["bg"].T.astype(BF16), ch["x_end"])


def _ssd_scan_call(xc, dt, adt, adt_t, e_sel, n_ctx):
    b, l, cw = xc.shape
    gw = e_sel.shape[-1]
    q = SSD_CHUNK
    nc = l // q
    ncc = n_ctx // q

    def rev(i):
        return jnp.where(i < ncc, ncc - 1 - i, nc - 1 - (i - ncc))

    bp = BATCH_PACK

    def specs(ch):
        tokd = pl.BlockSpec((bp, q, LANES), lambda bi, i: (bi, ch(i), 0))
        return [pl.BlockSpec((bp, q, cw), lambda bi, i: (bi, ch(i), 0)), tokd, tokd,
                pl.BlockSpec((bp, 2 * SSD_HEADS, q), lambda bi, i: (bi, 0, ch(i)))]

    ys = jax.ShapeDtypeStruct((b, l, gw), F32)
    return pl.pallas_call(
        _ssd_scan_kernel,
        grid=(b // bp, nc),
        in_specs=specs(lambda i: i) + specs(rev) + [pl.BlockSpec(e_sel.shape, lambda bi, i: (0, 0, 0))],
        out_specs=[pl.BlockSpec((bp, q, gw), lambda bi, i: (bi, i, 0)),
                   pl.BlockSpec((bp, q, gw), lambda bi, i: (bi, rev(i), 0))],
        out_shape=[ys, ys],
        scratch_shapes=[pltpu.VMEM((bp, 2, SSD_GROUPS, SSD_STATE, gw // SSD_GROUPS), F32)],
        compiler_params=_cparams(2),
        name="ssd_scan",
    )(xc, dt, adt, adt_t, xc, dt, adt, adt_t, e_sel)


def _ssd_mixer_out(yf_ref, yr_ref, xs_ref, z_ref, dsk_ref, ng_ref):
    flat = lambda r: r[...].reshape(-1, r.shape[-1])
    z = flat(z_ref)
    y = (flat(yf_ref) + flat(yr_ref) + dsk_ref[...] * flat(xs_ref)) * (z * _sigmoid(z))
    gwid = y.shape[1] // SSD_GROUPS
    outs = []
    for g in range(SSD_GROUPS):
        yg = y[:, g * gwid:(g + 1) * gwid]
        outs.append(yg * lax.rsqrt(jnp.mean(yg * yg, -1, keepdims=True) + LN_EPS))
    return jnp.concatenate(outs, 1) * ng_ref[...]


def _swa_kernel(sink_ref, q_ref, k_ref, v_ref, o_ref, *, n_ctx, n_lat):
    i = pl.program_id(1)
    tq = q_ref.shape[1]
    span = tq + 2 * SWA_WINDOW
    nk = n_ctx + span
    group = SWA_HEADS // SWA_KV_HEADS
    start = jnp.clip(i * tq - SWA_WINDOW, 0, n_lat - span)
    w0 = pl.multiple_of(n_ctx + start, LANES)
    kpos = start + lax.broadcasted_iota(jnp.int32, (span, tq), 0)
    qpos = i * tq + lax.broadcasted_iota(jnp.int32, (span, tq), 1)
    bias = jnp.where(jnp.abs(qpos - kpos) <= SWA_WINDOW, 0.0, -1e30)
    bias = jnp.concatenate([jnp.zeros((n_ctx, tq), F32), bias], 0)
    bias = jnp.concatenate([bias] * group, 1)
    k_low = lax.broadcasted_iota(jnp.int32, (nk, LANES), 1) < HEAD_DIM
    v_low = lax.broadcasted_iota(jnp.int32, (LANES, nk), 0) < HEAD_DIM
    sinks = [jnp.concatenate([jnp.full((1, tq), sink_ref[kvh * group + g], F32) for g in range(group)], 1)
             for kvh in range(SWA_KV_HEADS)]
    bps = range(q_ref.shape[0])
    vts, ss, es, rdens = [], {}, {}, {}
    for p in bps:
        kcat = jnp.concatenate([k_ref[p, :n_ctx, :], k_ref[p, pl.ds(w0, span), :]], 0).astype(F32)
        vts.append(jnp.concatenate([v_ref[p, :n_ctx, :], v_ref[p, pl.ds(w0, span), :]], 0).astype(F32).T)
        qs = jnp.concatenate([q_ref[p, :, g * LANES:(g + 1) * LANES] for g in range(group)], 0)
        for kvh in range(SWA_KV_HEADS):
            km = jnp.where(k_low == (kvh == 0), kcat, 0.0).astype(BF16)
            ss[p, kvh] = _dot_nt(km, qs) + bias
    for p in bps:
        for kvh in range(SWA_KV_HEADS):
            m = jnp.maximum(jnp.max(ss[p, kvh], 0, keepdims=True), sinks[kvh])
            e = jnp.exp2(ss[p, kvh] - m)
            rdens[p, kvh] = 1.0 / (jnp.exp2(sinks[kvh] - m) + jnp.sum(e, 0, keepdims=True))
            es[p, kvh] = e.astype(BF16)
    for p in bps:
        ot = (_dot(jnp.where(v_low, vts[p], 0.0).astype(BF16), es[p, 0]) * rdens[p, 0]
              + _dot(jnp.where(v_low, 0.0, vts[p]).astype(BF16), es[p, 1]) * rdens[p, 1])
        for g in range(group):
            o_ref[p, :, g * LANES:(g + 1) * LANES] = ot[:, g * tq:(g + 1) * tq].T


def _swa_call(sink, q, k, v, n_ctx):
    b, l, qw = q.shape
    kw = k.shape[-1]
    n_lat = l - n_ctx
    off = n_ctx // SWA_TILE
    bp = BATCH_PACK
    kv = pl.BlockSpec((bp, l, kw), lambda bi, t: (bi, 0, 0))
    return pl.pallas_call(
        functools.partial(_swa_kernel, n_ctx=n_ctx, n_lat=n_lat),
        grid=(b // bp, n_lat // SWA_TILE),
        in_specs=[pl.BlockSpec(memory_space=pltpu.SMEM),
                  pl.BlockSpec((bp, SWA_TILE, qw), lambda bi, t: (bi, t + off, 0)), kv, kv],
        out_specs=pl.BlockSpec((bp, SWA_TILE, qw), lambda bi, t: (bi, t, 0)),
        out_shape=jax.ShapeDtypeStruct((b, n_lat, qw), F32),
        compiler_params=_cparams(2),
        name="swa",
    )(sink, q, k, v)


def _rope_tables(n_ctx, n_lat, width):
    half = HEAD_DIM // 4
    inv = ROPE_BASE ** (-jnp.arange(half, dtype=F32) / half)
    pos = jnp.arange(n_lat, dtype=jnp.int32)
    rows = (pos // GRID_W).astype(F32)[:, None] * inv
    cols = (pos % GRID_W).astype(F32)[:, None] * inv
    cos = jnp.concatenate([jnp.cos(rows), jnp.cos(rows), jnp.cos(cols), jnp.cos(cols)], -1)
    sin = jnp.concatenate([-jnp.sin(rows), jnp.sin(rows), -jnp.sin(cols), jnp.sin(cols)], -1)
    cos = jnp.concatenate([jnp.ones((n_ctx, HEAD_DIM), F32), cos], 0)
    sin = jnp.concatenate([jnp.zeros((n_ctx, HEAD_DIM), F32), sin], 0)
    reps = width // HEAD_DIM
    return jnp.tile(cos, (1, reps)), jnp.tile(sin, (1, reps))


def kernel(x, c, ctx, c_ctx, mod_w, mod_b, ln_mix_g, ln_mix_b, ln_ffn_g, ln_ffn_b, ffn_w1, ffn_w2, w_out, ab_w_in, rwkv_mu, rwkv_w0, rwkv_w2, rwkv_a0, rwkv_a2, rwkv_g2, rwkv_k_k, rwkv_k_a, rwkv_r_k, rwkv_gn_g, rwkv_gn_b, diff_lq1, diff_lk1, diff_lq2, diff_lk2, diff_subln_g, cd_w_in, ssd_conv_w, ssd_conv_b, ssd_dt_bias, ssd_a_log, ssd_d, ssd_norm_g, swa_sink):
    b, n_lat, d = x.shape
    n_ctx = ctx.shape[1]
    depth = mod_w.shape[0]
    assert depth == 2 and n_ctx % ROW_TILE == 0 and n_lat % ROW_TILE == 0 and n_lat % GRID_W == 0
    assert n_lat >= SWA_TILE + 2 * SWA_WINDOW and b % BATCH_PACK == 0
    gw = d // 2
    n_ctx_tiles = n_ctx // ROW_TILE
    n_lat_tiles = n_lat // ROW_TILE
    alpha = (2 * depth) ** 0.25
    log2e = math.log2(math.e)
    q_scale = HEAD_DIM ** -0.5 * log2e

    rows_pad = -(-(b + 1) // 8) * 8
    c_pad = jnp.zeros((rows_pad, d), F32).at[:b].set(c).at[b].set(c_ctx)
    m = _mod_call(c_pad, mod_w, mod_b)

    def modsel(i):
        return jnp.stack([jnp.broadcast_to(m[i, b], (b, 6 * d)), m[i, :b]], 1).reshape(b, 2, 6, 1, d)

    cos_t, sin_t = _rope_tables(n_ctx, n_lat, 256)
    h0 = (ctx, x)

    rwkv_cols = 3 * gw + RWKV_LORA_W + RWKV_LORA_A + RWKV_LORA_G
    plan0 = ((0, rwkv_cols, False, 1.0, False), (rwkv_cols, gw, True, q_scale, False),
             (rwkv_cols + gw, gw, True, 1.0, False), (rwkv_cols + 2 * gw, gw, False, 1.0, True))
    p_rwkv, dq, dk, dvt = _inproj_call(h0, modsel(0), ab_w_in[0].astype(BF16), cos_t, sin_t, plan0,
                                       (F32, BF16, BF16, BF16), n_ctx_tiles, "inproj_ab")

    zw = jnp.zeros((RWKV_LORA_W, gw), F32)
    lora_w = jnp.concatenate([jnp.concatenate([rwkv_w2[0, 0], rwkv_w2[0, 1], zw, zw], 1),
                              jnp.concatenate([zw, zw, rwkv_a2[0, 0], rwkv_a2[0, 1]], 1)], 0)
    lora_b = jnp.concatenate([rwkv_w0[0, 0], rwkv_w0[0, 1], rwkv_a0[0, 0], rwkv_a0[0, 1]])[None, :]
    head_id = jnp.arange(gw, dtype=jnp.int32) // HEAD_DIM
    bd = (head_id[:, None] == head_id[None, :]).astype(BF16)
    at, bt, kt, rt, pe, rv, g, bonus = _rwkv_prep_call(p_rwkv, rwkv_mu[0][None, :], lora_b, _split_hilo(lora_w), _split_hilo(rwkv_g2[0]),
                                                       rwkv_k_k[0][None, :], rwkv_k_a[0][None, :],
                                                       rwkv_r_k[0].reshape(1, gw), bd, n_ctx_tiles)
    y_f, y_r = _rwkv_scan_call(at, bt, kt, rt, pe, rv, n_ctx)

    lam_init0 = 0.8 - 0.6 * math.exp(-0.3 * 0)
    lam = (jnp.exp(jnp.sum(diff_lq1[0] * diff_lk1[0])) - jnp.exp(jnp.sum(diff_lq2[0] * diff_lk2[0]))
           + lam_init0).reshape(1).astype(F32)
    o2 = _diff_attn_call(lam, dq, dk, dvt, diff_subln_g[0][:, None], n_ctx, lam_init0)

    row = lambda t: t[None, :]
    h1 = _outproj_ffn_call(_rwkv_mixer_out, (y_f, y_r, bonus, g), (row(rwkv_gn_g[0]), row(rwkv_gn_b[0])), o2, h0,
                           modsel(0), w_out[0].astype(BF16), ffn_w1[0].astype(BF16),
                           ffn_w2[0].astype(BF16), row(ln_mix_g[0]), row(ln_mix_b[0]), row(ln_ffn_g[0]),
                           row(ln_ffn_b[0]), h_off=0, n_out_tiles=n_ctx_tiles + n_lat_tiles,
                           n_ctx_tiles=n_ctx_tiles, alpha=alpha, name="outproj_ffn_0")

    conv_dim = gw + 2 * SSD_GROUPS * SSD_STATE
    ssd_cols = gw + conv_dim + 2 * SSD_HEADS
    kvw = SWA_KV_HEADS * HEAD_DIM
    wcd = cd_w_in[0]
    group = SWA_HEADS // SWA_KV_HEADS
    q_perm = jnp.arange(gw, dtype=jnp.int32).reshape(SWA_KV_HEADS, group, HEAD_DIM).transpose(1, 0, 2).reshape(-1)
    w1cat = jnp.concatenate([wcd[:, ssd_cols:ssd_cols + gw][:, q_perm], wcd[:, ssd_cols + gw:], wcd[:, :gw],
                             wcd[:, gw:gw + conv_dim], wcd[:, gw + conv_dim:ssd_cols],
                             jnp.zeros((d, LANES - 2 * SSD_HEADS), F32)], 1)
    w_out1 = jnp.concatenate([w_out[1, :gw], w_out[1, gw:][q_perm]], 0)
    c_q, c_k, c_v, c_z, c_x, c_dt = 0, gw, gw + kvw, gw + 2 * kvw, 2 * gw + 2 * kvw, 2 * gw + 2 * kvw + conv_dim
    plan1 = ((c_q, gw, True, q_scale, False), (c_k, kvw, True, 1.0, False), (c_v, kvw, False, 1.0, False),
             (c_z, gw, False, 1.0, False), (c_x, conv_dim, False, 1.0, False), (c_dt, LANES, False, 1.0, False))
    sq, sk, sv, pz, xbc, dt_raw = _inproj_call((h1,), modsel(1), w1cat.astype(BF16), cos_t, sin_t, plan1,
                                               (BF16, BF16, BF16, F32, F32, F32), n_ctx_tiles, "inproj_cd")

    pad_l = lambda t: jnp.concatenate([t, jnp.zeros((LANES - t.shape[0],), F32)])[None, :]
    a_neg = -jnp.exp(ssd_a_log[0].astype(F32))
    conv_w_pad = jnp.concatenate([ssd_conv_w[0], jnp.zeros((8 - SSD_CONV, conv_dim), F32)], 0)
    xc, dt_sp, adt = _ssd_prep_call(xbc, dt_raw, conv_w_pad, ssd_conv_b[0][None, :], pad_l(ssd_dt_bias[0].reshape(-1)),
                                    pad_l(a_neg.reshape(-1)), n_ctx_tiles)
    adt_t = jnp.swapaxes(adt[:, :, :2 * SSD_HEADS], 1, 2)
    jj = jnp.arange(LANES, dtype=jnp.int32)[:, None]
    hh = (jnp.arange(gw, dtype=jnp.int32) // HEAD_DIM)[None, :]
    e_sel = jnp.stack([(jj == hh), (jj == hh + SSD_HEADS)]).astype(BF16)
    ys_f, ys_r = _ssd_scan_call(xc, dt_sp, adt, adt_t, e_sel, n_ctx)
    o2 = _swa_call(swa_sink[0].astype(F32) * log2e, sq, sk, sv, n_ctx)

    return _outproj_ffn_call(_ssd_mixer_out, (ys_f, ys_r, xc, pz),
                             (row(jnp.repeat(ssd_d[0], HEAD_DIM)), row(ssd_norm_g[0])), o2, (h1,),
                             modsel(1), w_out1.astype(BF16), ffn_w1[1].astype(BF16),
                             ffn_w2[1].astype(BF16), row(ln_mix_g[1]), row(ln_mix_b[1]), row(ln_ffn_g[1]),
                             row(ln_ffn_b[1]), h_off=n_ctx_tiles, n_out_tiles=n_lat_tiles,
                             n_ctx_tiles=n_ctx_tiles, alpha=alpha, name="outproj_ffn_1")
```

```python
import functools
import math

import jax
import jax.numpy as jnp
from jax import lax
from jax.experimental import pallas as pl
from jax.experimental.pallas import tpu as pltpu

F32 = jnp.float32
BF16 = jnp.bfloat16

HEAD_DIM = 64
GRID_W = 64
ROPE_BASE = 10000.0
LN_EPS = 1e-5
RWKV_GN_EPS = 64e-5
RWKV_LORA_W = 64
RWKV_LORA_A = 64
RWKV_LORA_G = 128
DIFF_HEADS = 4
SSD_HEADS = 8
SSD_GROUPS = 2
SSD_STATE = 128
SSD_CONV = 5
SWA_HEADS = 8
SWA_KV_HEADS = 2
SWA_WINDOW = 128

ROW_TILE = 256
BATCH_PACK = 2
HALO = 8
SCAN_BATCH_PACK = 4
RWKV_CHUNK = 64
SSD_CHUNK = 128
SWA_TILE = 128
LANES = 128
VMEM_LIMIT = 56 * 1024 * 1024


def _cparams(n_axes):
    return pltpu.CompilerParams(dimension_semantics=("arbitrary",) * n_axes, vmem_limit_bytes=VMEM_LIMIT)


def _dot(a, b):
    return jnp.dot(a, b, preferred_element_type=F32)


def _dot_nt(a, b):
    return lax.dot_general(a, b, (((1,), (1,)), ((), ())), preferred_element_type=F32)


def _split_bf16(a, pieces):
    out = []
    for i in range(pieces):
        part = a.astype(BF16)
        out.append(part)
        if i + 1 < pieces:
            a = a - part.astype(F32)
    return out


def _dot_f32_by_exact(a, b_exact, pieces=3):
    acc = None
    for part in _split_bf16(a, pieces):
        t = _dot(part, b_exact)
        acc = t if acc is None else acc + t
    return acc


def _dot_exact_by_f32(a_exact, b, pieces=3):
    acc = None
    for part in _split_bf16(b, pieces):
        t = _dot(a_exact, part)
        acc = t if acc is None else acc + t
    return acc


def _dot_hilo(a, b):
    ah = a.astype(BF16)
    al = (a - ah.astype(F32)).astype(BF16)
    bh = b.astype(BF16)
    bl = (b - bh.astype(F32)).astype(BF16)
    return _dot(ah, bh) + _dot(ah, bl) + _dot(al, bh)


def _split_hilo(w):
    hi = w.astype(BF16)
    return jnp.stack([hi, (w - hi.astype(F32)).astype(BF16)])


def _dot_hilo_presplit(a, w_ref):
    ah = a.astype(BF16)
    al = (a - ah.astype(F32)).astype(BF16)
    return _dot(ah, w_ref[0]) + _dot(ah, w_ref[1]) + _dot(al, w_ref[0])


def _sigmoid(x):
    return 0.5 * jnp.tanh(0.5 * x) + 0.5


def _softplus(x):
    return jnp.maximum(x, 0.0) + jnp.log(1.0 + jnp.exp(-jnp.abs(x)))


def _layer_norm(y, g, b, eps):
    mu = jnp.mean(y, -1, keepdims=True)
    yc = y - mu
    var = jnp.mean(yc * yc, -1, keepdims=True)
    return yc * lax.rsqrt(var + eps) * g + b


def _mod_kernel(c_ref, w_ref, b_ref, o_ref):
    c = c_ref[...]
    o_ref[0] = _dot_hilo(c * _sigmoid(c), w_ref[0]) + b_ref[0]


def _mod_call(c_pad, mod_w, mod_b):
    depth, d, n = mod_w.shape
    rows = c_pad.shape[0]
    tn = 1536
    return pl.pallas_call(
        _mod_kernel,
        grid=(depth, n // tn),
        in_specs=[pl.BlockSpec((rows, d), lambda i, j: (0, 0)),
                  pl.BlockSpec((1, d, tn), lambda i, j: (i, 0, j)),
                  pl.BlockSpec((1, 1, tn), lambda i, j: (i, 0, j))],
        out_specs=pl.BlockSpec((1, rows, tn), lambda i, j: (i, 0, j)),
        out_shape=jax.ShapeDtypeStruct((depth, rows, n), F32),
        compiler_params=_cparams(2),
        name="adaln_mod",
    )(c_pad, mod_w, mod_b.reshape(depth, 1, n))


def _rope(x, cos, sin):
    lane = lax.broadcasted_iota(jnp.int32, (x.shape[0], LANES), 1)
    first = (lane % 32) < 16
    parts = []
    for g in range(0, x.shape[1], LANES):
        xg = x[:, g:g + LANES]
        parts.append(jnp.where(first, pltpu.roll(xg, LANES - 16, 1), pltpu.roll(xg, 16, 1)))
    sw = parts[0] if len(parts) == 1 else jnp.concatenate(parts, 1)
    return x * cos + sw * sin


def _stream_specs(streams, n_ctx_tiles):
    d = streams[0].shape[-1]
    bp = BATCH_PACK
    if len(streams) == 1:
        return [pl.BlockSpec((bp, ROW_TILE, d), lambda bi, t: (bi, t, 0))]
    return [pl.BlockSpec((bp, ROW_TILE, d), lambda bi, t: (bi, jnp.minimum(t, n_ctx_tiles - 1), 0)),
            pl.BlockSpec((bp, ROW_TILE, d), lambda bi, t: (bi, jnp.maximum(t - n_ctx_tiles, 0), 0))]


def _stream_tile(refs, n_ctx_tiles):
    if len(refs) == 1:
        return refs[0][...]
    return jnp.where(pl.program_id(1) < n_ctx_tiles, refs[0][...], refs[1][...])


def _inproj_kernel(*refs, plan, n_streams, n_ctx_tiles):
    x_refs = refs[:n_streams]
    sh_ref, sc_ref, w_ref, cos_ref, sin_ref = refs[n_streams:n_streams + 5]
    o_refs = refs[n_streams + 5:]
    x = _stream_tile(x_refs, n_ctx_tiles)
    bp, rows, d = x.shape
    xm = (x * (1.0 + sc_ref[:, 0, 0]) + sh_ref[:, 0, 0]).reshape(bp * rows, d).astype(BF16)
    cos = jnp.concatenate([cos_ref[...]] * bp, 0)
    sin = jnp.concatenate([sin_ref[...]] * bp, 0)
    for o_ref, (c0, width, rope, mult, transpose) in zip(o_refs, plan):
        for j in range(0, width, 256):
            cw = min(256, width - j)
            acc = _dot(xm, w_ref[:, c0 + j:c0 + j + cw])
            if rope:
                acc = _rope(acc, cos[:, :cw], sin[:, :cw])
            if mult != 1.0:
                acc = acc * mult
            if transpose:
                for p in range(bp):
                    o_ref[p, j:j + cw, :] = acc[p * rows:(p + 1) * rows].T.astype(o_ref.dtype)
            else:
                o_ref[:, :, j:j + cw] = acc.reshape(bp, rows, cw).astype(o_ref.dtype)


def _inproj_call(streams, modsel, w_bf16, cos_t, sin_t, plan, out_dtypes, n_ctx_tiles, name):
    b, _, d = streams[0].shape
    l = sum(s.shape[1] for s in streams)
    n_tiles = l // ROW_TILE
    seg = lambda bi, t: (bi, (t >= n_ctx_tiles).astype(jnp.int32), 0, 0, 0)
    seg_scale = lambda bi, t: (bi, (t >= n_ctx_tiles).astype(jnp.int32), 1, 0, 0)
    out_shape, out_specs = [], []
    bp = BATCH_PACK
    for (_, width, _, _, transpose), dt in zip(plan, out_dtypes):
        if transpose:
            out_shape.append(jax.ShapeDtypeStruct((b, width, l), dt))
            out_specs.append(pl.BlockSpec((bp, width, ROW_TILE), lambda bi, t: (bi, 0, t)))
        else:
            out_shape.append(jax.ShapeDtypeStruct((b, l, width), dt))
            out_specs.append(pl.BlockSpec((bp, ROW_TILE, width), lambda bi, t: (bi, t, 0)))
    return pl.pallas_call(
        functools.partial(_inproj_kernel, plan=plan, n_streams=len(streams), n_ctx_tiles=n_ctx_tiles),
        grid=(b // bp, n_tiles),
        in_specs=_stream_specs(streams, n_ctx_tiles) + [
            pl.BlockSpec((bp, 1, 1, 1, d), seg),
            pl.BlockSpec((bp, 1, 1, 1, d), seg_scale),
            pl.BlockSpec(w_bf16.shape, lambda bi, t: (0, 0), pipeline_mode=pl.Buffered(1)),
            pl.BlockSpec((ROW_TILE, cos_t.shape[1]), lambda bi, t: (t, 0)),
            pl.BlockSpec((ROW_TILE, sin_t.shape[1]), lambda bi, t: (t, 0))],
        out_specs=out_specs,
        out_shape=out_shape,
        compiler_params=_cparams(2),
        name=name,
    )(*streams, modsel, modsel, w_bf16, cos_t, sin_t)


def _rwkv_prep_kernel(p_ref, hp_ref, hn_ref, mu_ref, bias_ref, lora_ref, g2_ref, kk_ref, ka_ref, rk_ref, bd_ref,
                      sel_ref, selt_ref, at_o, bt_o, kt_o, rt_o, pe_o, v_o, g_o, bonus_o, ps_ref,
                      *, n_ctx_tiles, n_tiles, chunk):
    t = pl.program_id(1)
    gw = v_o.shape[-1]
    p = p_ref[0]
    rows = p.shape[0]
    first = jnp.logical_or(t == 0, t == n_ctx_tiles)
    last = jnp.logical_or(t == n_ctx_tiles - 1, t == n_tiles - 1)
    prev_row = jnp.where(first, 0.0, hp_ref[0, HALO - 1:HALO, :])
    next_row = jnp.where(last, 0.0, hn_ref[0, 0:1, :])
    half_mu = 0.5 * mu_ref[...]
    ps_ref[...] = (1.0 - mu_ref[...]) * p + half_mu * (pltpu.roll(p, 1, 0) + pltpu.roll(p, rows - 1, 0))
    row8 = lax.broadcasted_iota(jnp.int32, (HALO, p.shape[1]), 0)
    ps_ref[0:HALO, :] += jnp.where(row8 == 0, half_mu * (prev_row - p[rows - 1:rows, :]), 0.0)
    ps_ref[rows - HALO:, :] += jnp.where(row8 == HALO - 1, half_mu * (next_row - p[0:1, :]), 0.0)

    r = ps_ref[:, :gw]
    k = ps_ref[:, gw:2 * gw]
    v = ps_ref[:, 2 * gw:3 * gw]
    slab = ps_ref[:, 3 * gw:3 * gw + LANES]
    xg = ps_ref[:, 3 * gw + LANES:3 * gw + 2 * LANES]
    lane = lax.broadcasted_iota(jnp.int32, slab.shape, 1)
    lora_in = jnp.where(lane < RWKV_LORA_W, jnp.tanh(slab), slab)
    pre = _dot_hilo_presplit(lora_in, lora_ref) + bias_ref[...]
    bd = bd_ref[...]
    g_o[0] = _dot_hilo_presplit(_sigmoid(xg), g2_ref)
    kk = k * kk_ref[...]
    ss = _dot_f32_by_exact(kk * kk, sel_ref[...], 2)
    kk = kk * _dot_f32_by_exact(lax.rsqrt(jnp.maximum(ss, 1e-24)), selt_ref[...], 2)
    bonus_o[0] = _dot_f32_by_exact(r * k * rk_ref[...], bd, 2) * v
    v_o[0] = v.astype(v_o.dtype)
    ri = lax.broadcasted_iota(jnp.int32, (rows, rows), 0)
    ci = lax.broadcasted_iota(jnp.int32, (rows, rows), 1)
    same_chunk = (ri // chunk) == (ci // chunk)
    for d in range(2):
        lw = -math.exp(-0.5) * _sigmoid(pre[:, d * gw:(d + 1) * gw])
        a = _sigmoid(pre[:, (2 + d) * gw:(3 + d) * gw])
        before = (ci <= ri) if d == 0 else (ci >= ri)
        tri = jnp.logical_and(same_chunk, before).astype(BF16)
        cum = _dot_exact_by_f32(tri, lw, 2)
        p_inv = jnp.exp(-cum)
        at_o[0, d] = (-kk * jnp.exp(cum - lw)).astype(at_o.dtype)
        bt_o[0, d] = (kk * a * p_inv).astype(bt_o.dtype)
        kt_o[0, d] = (k * (1.0 + (a - 1.0) * ka_ref[...]) * p_inv).astype(kt_o.dtype)
        rt_o[0, d] = (r * jnp.exp(cum)).astype(rt_o.dtype)
        for j in range(rows // chunk):
            last_row = j * chunk + (chunk - 1 if d == 0 else 0)
            pe_o[0, d, j] = jnp.exp(cum[last_row:last_row + 1, :])


def _rwkv_prep_call(p, mu, bias, lora_w, g2, k_k, k_a, r_k, bd, sel, sel_t, n_ctx_tiles):
    b, l, cols = p.shape
    gw = k_k.shape[-1]
    n_tiles = l // ROW_TILE
    hb = ROW_TILE // HALO
    n_hb = l // HALO
    cpt = ROW_TILE // RWKV_CHUNK
    full = lambda a: pl.BlockSpec(a.shape, lambda bi, t: (0,) * a.ndim)
    tok = pl.BlockSpec((1, ROW_TILE, gw), lambda bi, t: (bi, t, 0))
    tok2 = pl.BlockSpec((1, 2, ROW_TILE, gw), lambda bi, t: (bi, 0, t, 0))
    s1 = jax.ShapeDtypeStruct((b, l, gw), F32)
    s2 = jax.ShapeDtypeStruct((b, 2, l, gw), BF16)
    return pl.pallas_call(
        functools.partial(_rwkv_prep_kernel, n_ctx_tiles=n_ctx_tiles, n_tiles=n_tiles, chunk=RWKV_CHUNK),
        grid=(b, n_tiles),
        in_specs=[pl.BlockSpec((1, ROW_TILE, cols), lambda bi, t: (bi, t, 0)),
                  pl.BlockSpec((1, HALO, cols), lambda bi, t: (bi, jnp.maximum(t * hb - 1, 0), 0)),
                  pl.BlockSpec((1, HALO, cols), lambda bi, t: (bi, jnp.minimum((t + 1) * hb, n_hb - 1), 0)),
                  full(mu), full(bias), full(lora_w), full(g2), full(k_k), full(k_a), full(r_k), full(bd),
                  full(sel), full(sel_t)],
        out_specs=[tok2, tok2, tok2, tok2,
                   pl.BlockSpec((1, 2, cpt, 1, gw), lambda bi, t: (bi, 0, t, 0, 0)), tok, tok, tok],
        out_shape=[s2, s2, s2, s2, jax.ShapeDtypeStruct((b, 2, l // RWKV_CHUNK, 1, gw), F32),
                   jax.ShapeDtypeStruct((b, l, gw), BF16), s1, s1],
        scratch_shapes=[pltpu.VMEM((ROW_TILE, cols), F32)],
        compiler_params=_cparams(2),
        name="rwkv_prep",
    )(p, p, p, mu, bias, lora_w, g2, k_k, k_a, r_k, bd, sel, sel_t)


def _rwkv_scan_kernel(atf_ref, btf_ref, ktf_ref, rtf_ref, pef_ref, vf_ref, atr_ref, btr_ref, ktr_ref, rtr_ref, per_ref,
                      vr_ref, yf_ref, yr_ref, s_ref):
    i = pl.program_id(1)
    c = vf_ref.shape[1]
    n_pairs = vf_ref.shape[2] // LANES
    c2 = 2 * c

    @pl.when(i == 0)
    def _():
        s_ref[...] = jnp.zeros_like(s_ref)

    row2 = lax.broadcasted_iota(jnp.int32, (c2, c2), 0)
    col2 = lax.broadcasted_iota(jnp.int32, (c2, c2), 1)
    same = (row2 // c) == (col2 // c)
    eye = (row2 == col2).astype(F32)
    low = lax.broadcasted_iota(jnp.int32, (c, LANES), 1) < HEAD_DIM

    def stack(x):
        xf = x.astype(F32)
        return jnp.concatenate([jnp.where(low, xf, 0.0), jnp.where(low, 0.0, xf)], 0)

    chains = []
    dirs = ((1, atf_ref, btf_ref, ktf_ref, rtf_ref, pef_ref, vf_ref, yf_ref),
            (-1, atr_ref, btr_ref, ktr_ref, rtr_ref, per_ref, vr_ref, yr_ref))
    for d, (sgn, at_ref, bt_ref, kt_ref, rt_ref, pe_ref, v_ref, y_ref) in enumerate(dirs):
        dt2 = (row2 - col2) * sgn
        strict = jnp.logical_and(same, dt2 > 0)
        incl = jnp.logical_and(same, dt2 >= 0)
        for p in range(vf_ref.shape[0]):
            for pr in range(n_pairs):
                sl = slice(pr * LANES, (pr + 1) * LANES)
                b_s = bt_ref[p, 0, :, sl]
                k_s = kt_ref[p, 0, :, sl]
                chains.append(dict(st=(p, d, pr), p=p, sl=sl, y_ref=y_ref, strict=strict, incl=incl,
                                   p_end=pe_ref[p, 0, 0, :, sl], vh=stack(v_ref[p, :, sl]),
                                   ar=jnp.concatenate([stack(at_ref[p, 0, :, sl]), stack(rt_ref[p, 0, :, sl])],
                                                      0).astype(BF16),
                                   bk2=jnp.concatenate([b_s, k_s], 0),
                                   bk=jnp.concatenate([b_s, b_s, k_s, k_s], 0)))

    low2 = lax.broadcasted_iota(jnp.int32, (c2, c2), 1) < c
    for ch in chains:
        ch["s0"] = s_ref[ch["st"]]
        prod = _dot_nt(ch["ar"], jnp.concatenate([ch["bk2"], ch["s0"].astype(BF16)], 0))
        ch["ars"] = prod[:, c2:]
        a_bk = prod[:c2, :c2]
        r_bk = prod[c2:, :c2]
        a_kb = pltpu.roll(a_bk, c, 1)
        r_kb = pltpu.roll(r_bk, c, 1)
        l_ab = jnp.where(ch["strict"], jnp.where(low2, a_bk, a_kb), 0.0)
        ch["l_ak"] = jnp.where(ch["strict"], jnp.where(low2, a_kb, a_bk), 0.0).astype(BF16)
        ch["l_r"] = jnp.concatenate([jnp.where(ch["incl"], jnp.where(low2, r_bk, r_kb), 0.0),
                                     jnp.where(ch["incl"], jnp.where(low2, r_kb, r_bk), 0.0)], 1).astype(BF16)
        ch["inv"] = eye + l_ab
        ch["pw"] = l_ab.astype(BF16)
    for ch in chains:
        ch["pw"] = _dot(ch["pw"], ch["pw"]).astype(BF16)
    for _ in range(max(0, int(math.ceil(math.log2(c))) - 2)):
        for ch in chains:
            both = _dot(ch["pw"], jnp.concatenate([ch["inv"].astype(BF16), ch["pw"]], 1))
            ch["inv"] = ch["inv"] + both[:, :c2]
            ch["pw"] = both[:, c2:].astype(BF16)
    for ch in chains:
        ch["inv"] = (ch["inv"] + _dot(ch["pw"], ch["inv"].astype(BF16))).astype(BF16)
    for ch in chains:
        ch["lv"] = _dot(ch["l_ak"], ch["vh"].astype(BF16))
    for ch in chains:
        u = _dot(ch["inv"], (ch["ars"][:c2] + ch["lv"]).astype(BF16))
        ch["uv"] = jnp.concatenate([u, ch["vh"]], 0)
    for ch in chains:
        y = ch["ars"][c2:] + _dot(ch["l_r"], ch["uv"].astype(BF16))
        ch["y_ref"][ch["p"], :, ch["sl"]] = y[:c] + y[c:]
    for ch in chains:
        upd = (ch["s0"] + _dot(ch["uv"].T.astype(BF16), ch["bk"])) * ch["p_end"]
        s_ref[ch["st"]] = jnp.where(same, upd, 0.0)


def _rwkv_scan_call(at, bt, kt, rt, pe, v, n_ctx):
    b, l, gw = v.shape
    c = RWKV_CHUNK
    nc = l // c
    ncc = n_ctx // c

    def rev(i):
        return jnp.where(i < ncc, ncc - 1 - i, nc - 1 - (i - ncc))

    bp = SCAN_BATCH_PACK if b % SCAN_BATCH_PACK == 0 else BATCH_PACK

    def specs(d, ch):
        tokd = pl.BlockSpec((bp, 1, c, gw), lambda bi, i: (bi, d, ch(i), 0))
        return [tokd, tokd, tokd, tokd, pl.BlockSpec((bp, 1, 1, 1, gw), lambda bi, i: (bi, d, ch(i), 0, 0)),
                pl.BlockSpec((bp, c, gw), lambda bi, i: (bi, ch(i), 0))]

    ys = jax.ShapeDtypeStruct((b, l, gw), F32)
    return pl.pallas_call(
        _rwkv_scan_kernel,
        grid=(b // bp, nc),
        in_specs=specs(0, lambda i: i) + specs(1, rev),
        out_specs=[pl.BlockSpec((bp, c, gw), lambda bi, i: (bi, i, 0)),
                   pl.BlockSpec((bp, c, gw), lambda bi, i: (bi, rev(i), 0))],
        out_shape=[ys, ys],
        scratch_shapes=[pltpu.VMEM((bp, 2, gw // LANES, LANES, LANES), F32)],
        compiler_params=_cparams(2),
        name="rwkv_scan",
    )(at, bt, kt, rt, pe, v, at, bt, kt, rt, pe, v)


def _rwkv_mixer_out(yf_ref, yr_ref, bonus_ref, g_ref, gg_ref, gb_ref):
    flat = lambda r: r[...].reshape(-1, r.shape[-1])
    y = flat(yf_ref) + flat(yr_ref)
    low = lax.broadcasted_iota(jnp.int32, (y.shape[0], LANES), 1) < HEAD_DIM

    def head_mean(t):
        tot = jnp.sum(t, -1, keepdims=True)
        lo = jnp.sum(jnp.where(low, t, 0.0), -1, keepdims=True)
        return jnp.where(low, lo, tot - lo) * (1.0 / HEAD_DIM)

    outs = []
    for s in range(0, y.shape[1], LANES):
        ys = y[:, s:s + LANES]
        yc = ys - head_mean(ys)
        outs.append(yc * lax.rsqrt(head_mean(yc * yc) + RWKV_GN_EPS))
    yn = jnp.concatenate(outs, 1) * gg_ref[...] + gb_ref[...]
    return (yn + flat(bonus_ref)) * flat(g_ref)


def _diff_attn_kernel(lam_ref, q_ref, k_ref, vt_ref, g_ref, o_ref, *, n_ctx, n_ctx_tiles, out_scale):
    t = pl.program_id(1)
    lam = lam_ref[0]
    vd = 2 * HEAD_DIM
    tq = q_ref.shape[1]
    low = lax.broadcasted_iota(jnp.int32, (tq, vd), 1) < HEAD_DIM

    def run(nk):
        def logits(u):
            p, hm = divmod(u, 2 * DIFF_HEADS)
            h, m = divmod(hm, 2)
            qf = q_ref[p, :, h * vd:(h + 1) * vd].astype(F32)
            qm = jnp.where(low == (m == 0), qf, 0.0).astype(BF16)
            return _dot_nt(k_ref[p, :nk, h * vd:(h + 1) * vd], qm)

        n_units = 2 * DIFF_HEADS * q_ref.shape[0]
        s_next = logits(0)
        es, invs = [], []
        for u in range(n_units):
            s = s_next
            if u + 1 < n_units:
                s_next = logits(u + 1)
            e = jnp.exp2(s - jnp.max(s, 0, keepdims=True))
            invs.append(1.0 / jnp.sum(e, 0, keepdims=True))
            es.append(e.astype(BF16))
            if u % 2 == 1:
                p, hm = divmod(u, 2 * DIFF_HEADS)
                h = hm // 2
                pv = _dot(vt_ref[p, h * vd:(h + 1) * vd, :nk], jnp.concatenate(es, 1))
                o = pv[:, :tq] * invs[0] - pv[:, tq:] * (lam * invs[1])
                o = o * lax.rsqrt(jnp.mean(o * o, 0, keepdims=True) + LN_EPS) * (g_ref[...] * out_scale)
                o_ref[p, :, h * vd:(h + 1) * vd] = o.T
                es, invs = [], []

    @pl.when(t < n_ctx_tiles)
    def _():
        run(n_ctx)

    @pl.when(t >= n_ctx_tiles)
    def _():
        run(k_ref.shape[1])


def _diff_attn_call(lam, q, k, vt, subln_g_col, n_ctx, lam_init):
    b, l, w = q.shape
    bp = BATCH_PACK
    tok = pl.BlockSpec((bp, ROW_TILE, w), lambda bi, t: (bi, t, 0))
    return pl.pallas_call(
        functools.partial(_diff_attn_kernel, n_ctx=n_ctx, n_ctx_tiles=n_ctx // ROW_TILE, out_scale=1.0 - lam_init),
        grid=(b // bp, l // ROW_TILE),
        in_specs=[pl.BlockSpec(memory_space=pltpu.SMEM), tok,
                  pl.BlockSpec((bp, l, w), lambda bi, t: (bi, 0, 0)),
                  pl.BlockSpec((bp, w, l), lambda bi, t: (bi, 0, 0)),
                  pl.BlockSpec(subln_g_col.shape, lambda bi, t: (0, 0))],
        out_specs=tok,
        out_shape=jax.ShapeDtypeStruct((b, l, w), F32),
        compiler_params=_cparams(2),
        name="diff_attn",
    )(lam, q, k, vt, subln_g_col)


def _outproj_ffn_kernel(*refs, alpha, n_streams, n_ctx_tiles, mixer_out, n_mix):
    o2_ref = refs[n_mix]
    h_refs = refs[n_mix + 1:n_mix + 1 + n_streams]
    (gm_ref, sh_ref, sc_ref, gf_ref, wo_ref, w1_ref, w2_ref, lmg_ref, lmb_ref, lfg_ref, lfb_ref,
     out_ref) = refs[n_mix + 1 + n_streams:]
    bp, rows, gw = o2_ref.shape
    flat = lambda t: t.reshape(bp * rows, t.shape[-1])
    per_b = lambda t: t.reshape(bp, rows, t.shape[-1])
    o1 = mixer_out(*refs[:n_mix])
    o = _dot(o1.astype(BF16), wo_ref[:gw, :]) + _dot(flat(o2_ref[...]).astype(BF16), wo_ref[gw:, :])
    h_in = _stream_tile(h_refs, n_ctx_tiles)
    h1 = _layer_norm(flat(alpha * h_in + gm_ref[:, 0, 0] * per_b(o)), lmg_ref[...], lmb_ref[...], LN_EPS)
    xm = flat(per_b(h1) * (1.0 + sc_ref[:, 0, 0]) + sh_ref[:, 0, 0]).astype(BF16)
    hidden = w1_ref.shape[1]
    step = 512

    def up(j):
        hj = jnp.maximum(_dot(xm, w1_ref[:, j:j + step]), 0.0)
        return (hj * hj).astype(BF16)

    acc = None
    act = up(0)
    for j in range(0, hidden, step):
        nxt = up(j + step) if j + step < hidden else None
        part = _dot(act, w2_ref[j:j + step, :])
        acc = part if acc is None else acc + part
        act = nxt
    out = _layer_norm(alpha * h1 + flat(gf_ref[:, 0, 0] * per_b(acc)), lfg_ref[...], lfb_ref[...], LN_EPS)
    out_ref[...] = per_b(out)


def _outproj_ffn_call(mixer_out, mix_tok, mix_par, o2, streams, modsel, wo, w1, w2, lmg, lmb, lfg, lfb, *, h_off,
                      n_out_tiles, n_ctx_tiles, alpha, name):
    b, _, gw = o2.shape
    d = streams[0].shape[-1]
    bp = BATCH_PACK
    assert len(streams) == 1 or h_off == 0
    seg = lambda col: (lambda bi, t: (bi, ((t + h_off) >= n_ctx_tiles).astype(jnp.int32), col, 0, 0))
    modspec = lambda col: pl.BlockSpec((bp, 1, 1, 1, d), seg(col))
    full = lambda a: pl.BlockSpec(a.shape, lambda bi, t: (0,) * a.ndim, pipeline_mode=pl.Buffered(1))
    mtok = pl.BlockSpec((bp, ROW_TILE, gw), lambda bi, t: (bi, t + h_off, 0))
    otok = pl.BlockSpec((bp, ROW_TILE, gw), lambda bi, t: (bi, t, 0))
    if len(streams) == 1:
        h_specs = [pl.BlockSpec((bp, ROW_TILE, d), lambda bi, t: (bi, t + h_off, 0))]
    else:
        h_specs = _stream_specs(streams, n_ctx_tiles)
    n_mix = len(mix_tok) + len(mix_par)
    return pl.pallas_call(
        functools.partial(_outproj_ffn_kernel, alpha=alpha, n_streams=len(streams), n_ctx_tiles=n_ctx_tiles,
                          mixer_out=mixer_out, n_mix=n_mix),
        grid=(b // bp, n_out_tiles),
        in_specs=[mtok] * len(mix_tok) + [full(p) for p in mix_par] + [otok] + h_specs + [
            modspec(2), modspec(3), modspec(4), modspec(5),
            full(wo), full(w1), full(w2), full(lmg), full(lmb), full(lfg), full(lfb)],
        out_specs=pl.BlockSpec((bp, ROW_TILE, d), lambda bi, t: (bi, t, 0)),
        out_shape=jax.ShapeDtypeStruct((b, n_out_tiles * ROW_TILE, d), F32),
        compiler_params=_cparams(2),
        name=name,
    )(*mix_tok, *mix_par, o2, *streams, modsel, modsel, modsel, modsel, wo, w1, w2, lmg, lmb, lfg, lfb)


def _ssd_prep_kernel(x_ref, hp_ref, hn_ref, dt_ref, cw_ref, cb_ref, dtb_ref, an_ref, xc_o, dt_o, adt_o,
                     *, n_ctx_tiles, n_tiles):
    t = pl.program_id(1)
    x = x_ref[0]
    rows = x.shape[0]
    first = jnp.logical_or(t == 0, t == n_ctx_tiles)
    last = jnp.logical_or(t == n_ctx_tiles - 1, t == n_tiles - 1)
    ext = jnp.concatenate([jnp.where(first, 0.0, hp_ref[0]), x, jnp.where(last, 0.0, hn_ref[0])], 0)
    n_ext = rows + 2 * HALO
    acc = cb_ref[...] + cw_ref[SSD_CONV // 2:SSD_CONV // 2 + 1, :] * x
    for j in range(SSD_CONV):
        s = j - SSD_CONV // 2
        if s == 0:
            continue
        shifted = pltpu.roll(ext, (n_ext - s) % n_ext, 0)[HALO:HALO + rows]
        acc = acc + cw_ref[j:j + 1, :] * shifted
    xc_o[0] = acc * _sigmoid(acc)
    dt = _softplus(dt_ref[0] + dtb_ref[...])
    dt_o[0] = dt
    adt_o[0] = dt * an_ref[...]


def _ssd_prep_call(xbc, dt, conv_w, conv_b, dt_bias, a_neg, n_ctx_tiles):
    b, l, cw = xbc.shape
    n_tiles = l // ROW_TILE
    hb = ROW_TILE // HALO
    n_hb = l // HALO
    full = lambda a: pl.BlockSpec(a.shape, lambda bi, t: (0,) * a.ndim)
    tokd = pl.BlockSpec((1, ROW_TILE, LANES), lambda bi, t: (bi, t, 0))
    sd = jax.ShapeDtypeStruct((b, l, LANES), F32)
    return pl.pallas_call(
        functools.partial(_ssd_prep_kernel, n_ctx_tiles=n_ctx_tiles, n_tiles=n_tiles),
        grid=(b, n_tiles),
        in_specs=[pl.BlockSpec((1, ROW_TILE, cw), lambda bi, t: (bi, t, 0)),
                  pl.BlockSpec((1, HALO, cw), lambda bi, t: (bi, jnp.maximum(t * hb - 1, 0), 0)),
                  pl.BlockSpec((1, HALO, cw), lambda bi, t: (bi, jnp.minimum((t + 1) * hb, n_hb - 1), 0)),
                  tokd, full(conv_w), full(conv_b), full(dt_bias), full(a_neg)],
        out_specs=[pl.BlockSpec((1, ROW_TILE, cw), lambda bi, t: (bi, t, 0)), tokd, tokd],
        out_shape=[jax.ShapeDtypeStruct((b, l, cw), F32), sd, sd],
        compiler_params=_cparams(2),
        name="ssd_prep",
    )(xbc, xbc, xbc, dt, conv_w, conv_b, dt_bias, a_neg)


def _ssd_scan_kernel(xcf_ref, dtf_ref, adtf_ref, adttf_ref, xcr_ref, dtr_ref, adtr_ref, adttr_ref, e_ref,
                     yf_ref, yr_ref, st_ref):
    i = pl.program_id(1)
    q = xcf_ref.shape[1]
    gw = yf_ref.shape[-1]
    n = SSD_STATE
    hpg = SSD_HEADS // SSD_GROUPS
    gcols = hpg * HEAD_DIM

    @pl.when(i == 0)
    def _():
        st_ref[...] = jnp.zeros_like(st_ref)

    row = lax.broadcasted_iota(jnp.int32, (q, q), 0)
    col = lax.broadcasted_iota(jnp.int32, (q, q), 1)
    chains = []
    dirs = ((1, xcf_ref, dtf_ref, adtf_ref, adttf_ref, yf_ref), (-1, xcr_ref, dtr_ref, adtr_ref, adttr_ref, yr_ref))
    for p, d in [(p, d) for p in range(xcf_ref.shape[0]) for d in range(2)]:
        sgn, xc_ref, dt_ref, adt_ref, adtt_ref, y_ref = dirs[d]
        incl = (row - col) * sgn >= 0
        tri = incl.astype(BF16)
        tri_t = ((col - row) * sgn >= 0).astype(BF16)
        cs = _dot_exact_by_f32(tri, adt_ref[p])
        cs_t = _dot_f32_by_exact(adtt_ref[p], tri_t)
        e = e_ref[d]
        cse = _dot_f32_by_exact(cs, e)
        dte = _dot_f32_by_exact(dt_ref[p], e, 2)
        tot = cse[q - 1:q, :] if sgn > 0 else cse[0:1, :]
        xc = xc_ref[p]
        xdt = xc[:, :gw] * dte
        e_cs = jnp.exp(cse)
        x_end = (xdt * jnp.exp(tot - cse)).astype(BF16)
        e_tot = jnp.exp(tot)
        xdt16 = xdt.astype(BF16)
        for g in range(SSD_GROUPS):
            gsl = slice(g * gcols, (g + 1) * gcols)
            lms = []
            for hh in range(hpg):
                h = g * hpg + hh
                colv = cse[:, h * HEAD_DIM:h * HEAD_DIM + 1]
                rowv = cs_t[d * SSD_HEADS + h:d * SSD_HEADS + h + 1, :]
                lms.append(jnp.where(incl, jnp.exp(colv - rowv), 0.0))
            chains.append(dict(st=(p, d, g), p=p, gsl=gsl, y_ref=y_ref, lms=lms, e_cs=e_cs[:, gsl], e_tot=e_tot[:, gsl],
                               x_end=x_end[:, gsl], xdt16=xdt16[:, gsl],
                               bg=xc[:, gw + g * n:gw + (g + 1) * n],
                               cg=xc[:, gw + (SSD_GROUPS + g) * n:gw + (SSD_GROUPS + g + 1) * n].astype(BF16)))
    for ch in chains:
        ch["cb"] = _dot_nt(ch["cg"], ch["bg"].astype(BF16))
        ch["s0"] = st_ref[ch["st"]]
    for ch in chains:
        ch["y_off"] = _dot(ch["cg"], ch["s0"].astype(BF16)) * ch["e_cs"]
    for ch in chains:
        ch["ys"] = [_dot((ch["cb"] * ch["lms"][hh]).astype(BF16), ch["xdt16"][:, hh * HEAD_DIM:(hh + 1) * HEAD_DIM])
                    for hh in range(hpg)]
    for ch in chains:
        ch["y_ref"][ch["p"], :, ch["gsl"]] = jnp.concatenate(ch["ys"], 1) + ch["y_off"]
        st_ref[ch["st"]] = ch["s0"] * ch["e_tot"] + _dot(ch["bg"].T.astype(BF16), ch["x_end"])


def _ssd_scan_call(xc, dt, adt, adt_t, e_sel, n_ctx):
    b, l, cw = xc.shape
    gw = e_sel.shape[-1]
    q = SSD_CHUNK
    nc = l // q
    ncc = n_ctx // q

    def rev(i):
        return jnp.where(i < ncc, ncc - 1 - i, nc - 1 - (i - ncc))

    bp = BATCH_PACK

    def specs(ch):
        tokd = pl.BlockSpec((bp, q, LANES), lambda bi, i: (bi, ch(i), 0))
        return [pl.BlockSpec((bp, q, cw), lambda bi, i: (bi, ch(i), 0)), tokd, tokd,
                pl.BlockSpec((bp, 2 * SSD_HEADS, q), lambda bi, i: (bi, 0, ch(i)))]

    ys = jax.ShapeDtypeStruct((b, l, gw), F32)
    return pl.pallas_call(
        _ssd_scan_kernel,
        grid=(b // bp, nc),
        in_specs=specs(lambda i: i) + specs(rev) + [pl.BlockSpec(e_sel.shape, lambda bi, i: (0, 0, 0))],
        out_specs=[pl.BlockSpec((bp, q, gw), lambda bi, i: (bi, i, 0)),
                   pl.BlockSpec((bp, q, gw), lambda bi, i: (bi, rev(i), 0))],
        out_shape=[ys, ys],
        scratch_shapes=[pltpu.VMEM((bp, 2, SSD_GROUPS, SSD_STATE, gw // SSD_GROUPS), F32)],
        compiler_params=_cparams(2),
        name="ssd_scan",
    )(xc, dt, adt, adt_t, xc, dt, adt, adt_t, e_sel)


def _ssd_mixer_out(yf_ref, yr_ref, xs_ref, z_ref, dsk_ref, ng_ref):
    flat = lambda r: r[...].reshape(-1, r.shape[-1])
    z = flat(z_ref)
    y = (flat(yf_ref) + flat(yr_ref) + dsk_ref[...] * flat(xs_ref)) * (z * _sigmoid(z))
    gwid = y.shape[1] // SSD_GROUPS
    outs = []
    for g in range(SSD_GROUPS):
        yg = y[:, g * gwid:(g + 1) * gwid]
        outs.append(yg * lax.rsqrt(jnp.mean(yg * yg, -1, keepdims=True) + LN_EPS))
    return jnp.concatenate(outs, 1) * ng_ref[...]


def _swa_kernel(sink_ref, q_ref, k_ref, v_ref, o_ref, *, n_ctx, n_lat):
    i = pl.program_id(1)
    tq = q_ref.shape[1]
    span = tq + 2 * SWA_WINDOW
    nk = n_ctx + span
    group = SWA_HEADS // SWA_KV_HEADS
    start = jnp.clip(i * tq - SWA_WINDOW, 0, n_lat - span)
    w0 = pl.multiple_of(n_ctx + start, LANES)
    kpos = start + lax.broadcasted_iota(jnp.int32, (span, tq), 0)
    qpos = i * tq + lax.broadcasted_iota(jnp.int32, (span, tq), 1)
    bias = jnp.where(jnp.abs(qpos - kpos) <= SWA_WINDOW, 0.0, -1e30)
    bias = jnp.concatenate([jnp.zeros((n_ctx, tq), F32), bias], 0)
    bias = jnp.concatenate([bias] * group, 1)
    k_low = lax.broadcasted_iota(jnp.int32, (nk, LANES), 1) < HEAD_DIM
    v_low = lax.broadcasted_iota(jnp.int32, (LANES, nk), 0) < HEAD_DIM
    sinks = [jnp.concatenate([jnp.full((1, tq), sink_ref[kvh * group + g], F32) for g in range(group)], 1)
             for kvh in range(SWA_KV_HEADS)]
    bps = range(q_ref.shape[0])
    vts, ss, es, rdens = [], {}, {}, {}
    for p in bps:
        kcat = jnp.concatenate([k_ref[p, :n_ctx, :], k_ref[p, pl.ds(w0, span), :]], 0).astype(F32)
        vts.append(jnp.concatenate([v_ref[p, :n_ctx, :], v_ref[p, pl.ds(w0, span), :]], 0).astype(F32).T)
        qs = jnp.concatenate([q_ref[p, :, g * LANES:(g + 1) * LANES] for g in range(group)], 0)
        for kvh in range(SWA_KV_HEADS):
            km = jnp.where(k_low == (kvh == 0), kcat, 0.0).astype(BF16)
            ss[p, kvh] = _dot_nt(km, qs) + bias
    for p in bps:
        for kvh in range(SWA_KV_HEADS):
            m = jnp.maximum(jnp.max(ss[p, kvh], 0, keepdims=True), sinks[kvh])
            e = jnp.exp2(ss[p, kvh] - m)
            rdens[p, kvh] = 1.0 / (jnp.exp2(sinks[kvh] - m) + jnp.sum(e, 0, keepdims=True))
            es[p, kvh] = e.astype(BF16)
    for p in bps:
        ot = (_dot(jnp.where(v_low, vts[p], 0.0).astype(BF16), es[p, 0]) * rdens[p, 0]
              + _dot(jnp.where(v_low, 0.0, vts[p]).astype(BF16), es[p, 1]) * rdens[p, 1])
        for g in range(group):
            o_ref[p, :, g * LANES:(g + 1) * LANES] = ot[:, g * tq:(g + 1) * tq].T


def _swa_call(sink, q, k, v, n_ctx):
    b, l, qw = q.shape
    kw = k.shape[-1]
    n_lat = l - n_ctx
    off = n_ctx // SWA_TILE
    bp = BATCH_PACK
    kv = pl.BlockSpec((bp, l, kw), lambda bi, t: (bi, 0, 0))
    return pl.pallas_call(
        functools.partial(_swa_kernel, n_ctx=n_ctx, n_lat=n_lat),
        grid=(b // bp, n_lat // SWA_TILE),
        in_specs=[pl.BlockSpec(memory_space=pltpu.SMEM),
                  pl.BlockSpec((bp, SWA_TILE, qw), lambda bi, t: (bi, t + off, 0)), kv, kv],
        out_specs=pl.BlockSpec((bp, SWA_TILE, qw), lambda bi, t: (bi, t, 0)),
        out_shape=jax.ShapeDtypeStruct((b, n_lat, qw), F32),
        compiler_params=_cparams(2),
        name="swa",
    )(sink, q, k, v)


def _rope_tables(n_ctx, n_lat, width):
    half = HEAD_DIM // 4
    inv = ROPE_BASE ** (-jnp.arange(half, dtype=F32) / half)
    pos = jnp.arange(n_lat, dtype=jnp.int32)
    rows = (pos // GRID_W).astype(F32)[:, None] * inv
    cols = (pos % GRID_W).astype(F32)[:, None] * inv
    cos = jnp.concatenate([jnp.cos(rows), jnp.cos(rows), jnp.cos(cols), jnp.cos(cols)], -1)
    sin = jnp.concatenate([-jnp.sin(rows), jnp.sin(rows), -jnp.sin(cols), jnp.sin(cols)], -1)
    cos = jnp.concatenate([jnp.ones((n_ctx, HEAD_DIM), F32), cos], 0)
    sin = jnp.concatenate([jnp.zeros((n_ctx, HEAD_DIM), F32), sin], 0)
    reps = width // HEAD_DIM
    return jnp.tile(cos, (1, reps)), jnp.tile(sin, (1, reps))


def kernel(x, c, ctx, c_ctx, mod_w, mod_b, ln_mix_g, ln_mix_b, ln_ffn_g, ln_ffn_b, ffn_w1, ffn_w2, w_out, ab_w_in, rwkv_mu, rwkv_w0, rwkv_w2, rwkv_a0, rwkv_a2, rwkv_g2, rwkv_k_k, rwkv_k_a, rwkv_r_k, rwkv_gn_g, rwkv_gn_b, diff_lq1, diff_lk1, diff_lq2, diff_lk2, diff_subln_g, cd_w_in, ssd_conv_w, ssd_conv_b, ssd_dt_bias, ssd_a_log, ssd_d, ssd_norm_g, swa_sink):
    b, n_lat, d = x.shape
    n_ctx = ctx.shape[1]
    depth = mod_w.shape[0]
    assert depth == 2 and n_ctx % ROW_TILE == 0 and n_lat % ROW_TILE == 0 and n_lat % GRID_W == 0
    assert n_lat >= SWA_TILE + 2 * SWA_WINDOW and b % BATCH_PACK == 0
    gw = d // 2
    n_ctx_tiles = n_ctx // ROW_TILE
    n_lat_tiles = n_lat // ROW_TILE
    alpha = (2 * depth) ** 0.25
    log2e = math.log2(math.e)
    q_scale = HEAD_DIM ** -0.5 * log2e

    rows_pad = -(-(b + 1) // 8) * 8
    c_pad = jnp.zeros((rows_pad, d), F32).at[:b].set(c).at[b].set(c_ctx)
    m = _mod_call(c_pad, mod_w, mod_b)

    def modsel(i):
        return jnp.stack([jnp.broadcast_to(m[i, b], (b, 6 * d)), m[i, :b]], 1).reshape(b, 2, 6, 1, d)

    cos_t, sin_t = _rope_tables(n_ctx, n_lat, 256)
    h0 = (ctx, x)

    rwkv_cols = 3 * gw + RWKV_LORA_W + RWKV_LORA_A + RWKV_LORA_G
    plan0 = ((0, rwkv_cols, False, 1.0, False), (rwkv_cols, gw, True, q_scale, False),
             (rwkv_cols + gw, gw, True, 1.0, False), (rwkv_cols + 2 * gw, gw, False, 1.0, True))
    p_rwkv, dq, dk, dvt = _inproj_call(h0, modsel(0), ab_w_in[0].astype(BF16), cos_t, sin_t, plan0,
                                       (F32, BF16, BF16, BF16), n_ctx_tiles, "inproj_ab")

    zw = jnp.zeros((RWKV_LORA_W, gw), F32)
    lora_w = jnp.concatenate([jnp.concatenate([rwkv_w2[0, 0], rwkv_w2[0, 1], zw, zw], 1),
                              jnp.concatenate([zw, zw, rwkv_a2[0, 0], rwkv_a2[0, 1]], 1)], 0)
    lora_b = jnp.concatenate([rwkv_w0[0, 0], rwkv_w0[0, 1], rwkv_a0[0, 0], rwkv_a0[0, 1]])[None, :]
    head_id = jnp.arange(gw, dtype=jnp.int32) // HEAD_DIM
    bd = (head_id[:, None] == head_id[None, :]).astype(BF16)
    sel = (head_id[:, None] == jnp.arange(LANES, dtype=jnp.int32)[None, :]).astype(BF16)
    at, bt, kt, rt, pe, rv, g, bonus = _rwkv_prep_call(p_rwkv, rwkv_mu[0][None, :], lora_b, _split_hilo(lora_w), _split_hilo(rwkv_g2[0]),
                                                       rwkv_k_k[0][None, :], rwkv_k_a[0][None, :],
                                                       rwkv_r_k[0].reshape(1, gw), bd, sel, sel.T, n_ctx_tiles)
    y_f, y_r = _rwkv_scan_call(at, bt, kt, rt, pe, rv, n_ctx)

    lam_init0 = 0.8 - 0.6 * math.exp(-0.3 * 0)
    lam = (jnp.exp(jnp.sum(diff_lq1[0] * diff_lk1[0])) - jnp.exp(jnp.sum(diff_lq2[0] * diff_lk2[0]))
           + lam_init0).reshape(1).astype(F32)
    o2 = _diff_attn_call(lam, dq, dk, dvt, diff_subln_g[0][:, None], n_ctx, lam_init0)

    row = lambda t: t[None, :]
    h1 = _outproj_ffn_call(_rwkv_mixer_out, (y_f, y_r, bonus, g), (row(rwkv_gn_g[0]), row(rwkv_gn_b[0])), o2, h0,
                           modsel(0), w_out[0].astype(BF16), ffn_w1[0].astype(BF16),
                           ffn_w2[0].astype(BF16), row(ln_mix_g[0]), row(ln_mix_b[0]), row(ln_ffn_g[0]),
                           row(ln_ffn_b[0]), h_off=0, n_out_tiles=n_ctx_tiles + n_lat_tiles,
                           n_ctx_tiles=n_ctx_tiles, alpha=alpha, name="outproj_ffn_0")

    conv_dim = gw + 2 * SSD_GROUPS * SSD_STATE
    ssd_cols = gw + conv_dim + 2 * SSD_HEADS
    kvw = SWA_KV_HEADS * HEAD_DIM
    wcd = cd_w_in[0]
    group = SWA_HEADS // SWA_KV_HEADS
    q_perm = jnp.arange(gw, dtype=jnp.int32).reshape(SWA_KV_HEADS, group, HEAD_DIM).transpose(1, 0, 2).reshape(-1)
    w1cat = jnp.concatenate([wcd[:, ssd_cols:ssd_cols + gw][:, q_perm], wcd[:, ssd_cols + gw:], wcd[:, :gw],
                             wcd[:, gw:gw + conv_dim], wcd[:, gw + conv_dim:ssd_cols],
                             jnp.zeros((d, LANES - 2 * SSD_HEADS), F32)], 1)
    w_out1 = jnp.concatenate([w_out[1, :gw], w_out[1, gw:][q_perm]], 0)
    c_q, c_k, c_v, c_z, c_x, c_dt = 0, gw, gw + kvw, gw + 2 * kvw, 2 * gw + 2 * kvw, 2 * gw + 2 * kvw + conv_dim
    plan1 = ((c_q, gw, True, q_scale, False), (c_k, kvw, True, 1.0, False), (c_v, kvw, False, 1.0, False),
             (c_z, gw, False, 1.0, False), (c_x, conv_dim, False, 1.0, False), (c_dt, LANES, False, 1.0, False))
    sq, sk, sv, pz, xbc, dt_raw = _inproj_call((h1,), modsel(1), w1cat.astype(BF16), cos_t, sin_t, plan1,
                                               (BF16, BF16, BF16, F32, F32, F32), n_ctx_tiles, "inproj_cd")

    pad_l = lambda t: jnp.concatenate([t, jnp.zeros((LANES - t.shape[0],), F32)])[None, :]
    a_neg = -jnp.exp(ssd_a_log[0].astype(F32))
    conv_w_pad = jnp.concatenate([ssd_conv_w[0], jnp.zeros((8 - SSD_CONV, conv_dim), F32)], 0)
    xc, dt_sp, adt = _ssd_prep_call(xbc, dt_raw, conv_w_pad, ssd_conv_b[0][None, :], pad_l(ssd_dt_bias[0].reshape(-1)),
                                    pad_l(a_neg.reshape(-1)), n_ctx_tiles)
    adt_t = jnp.swapaxes(adt[:, :, :2 * SSD_HEADS], 1, 2)
    jj = jnp.arange(LANES, dtype=jnp.int32)[:, None]
    hh = (jnp.arange(gw, dtype=jnp.int32) // HEAD_DIM)[None, :]
    e_sel = jnp.stack([(jj == hh), (jj == hh + SSD_HEADS)]).astype(BF16)
    ys_f, ys_r = _ssd_scan_call(xc, dt_sp, adt, adt_t, e_sel, n_ctx)
    o2 = _swa_call(swa_sink[0].astype(F32) * log2e, sq, sk, sv, n_ctx)

    return _outproj_ffn_call(_ssd_mixer_out, (ys_f, ys_r, xc, pz),
                             (row(jnp.repeat(ssd_d[0], HEAD_DIM)), row(ssd_norm_g[0])), o2, (h1,),
                             modsel(1), w_out1.astype(BF16), ffn_w1[1].astype(BF16),
                             ffn_w2[1].astype(BF16), row(ln_mix_g[1]), row(ln_mix_b[1]), row(ln_ffn_g[1]),
                             row(ln_ffn_b[1]), h_off=n_ctx_tiles, n_out_tiles=n_lat_tiles,
                             n_ctx_tiles=n_ctx_tiles, alpha=alpha, name="outproj_ffn_1")
```

```python
import functools
import math

import jax
import jax.numpy as jnp
from jax import lax
from jax.experimental import pallas as pl
from jax.experimental.pallas import tpu as pltpu

F32 = jnp.float32
BF16 = jnp.bfloat16

HEAD_DIM = 64
GRID_W = 64
ROPE_BASE = 10000.0
LN_EPS = 1e-5
RWKV_GN_EPS = 64e-5
RWKV_LORA_W = 64
RWKV_LORA_A = 64
RWKV_LORA_G = 128
DIFF_HEADS = 4
SSD_HEADS = 8
SSD_GROUPS = 2
SSD_STATE = 128
SSD_CONV = 5
SWA_HEADS = 8
SWA_KV_HEADS = 2
SWA_WINDOW = 128

ROW_TILE = 256
BATCH_PACK = 2
HALO = 8
SCAN_BATCH_PACK = 4
RWKV_CHUNK = 64
SSD_CHUNK = 128
FFN_UP_AHEAD = 1
DIFF_QK_AHEAD = 3
SWA_TILE = 128
LANES = 128
VMEM_LIMIT = 56 * 1024 * 1024


def _cparams(n_axes):
    return pltpu.CompilerParams(dimension_semantics=("arbitrary",) * n_axes, vmem_limit_bytes=VMEM_LIMIT)


def _dot(a, b):
    return jnp.dot(a, b, preferred_element_type=F32)


def _dot_nt(a, b):
    return lax.dot_general(a, b, (((1,), (1,)), ((), ())), preferred_element_type=F32)


def _split_bf16(a, pieces):
    out = []
    for i in range(pieces):
        part = a.astype(BF16)
        out.append(part)
        if i + 1 < pieces:
            a = a - part.astype(F32)
    return out


def _dot_f32_by_exact(a, b_exact, pieces=3):
    acc = None
    for part in _split_bf16(a, pieces):
        t = _dot(part, b_exact)
        acc = t if acc is None else acc + t
    return acc


def _dot_exact_by_f32(a_exact, b, pieces=3):
    acc = None
    for part in _split_bf16(b, pieces):
        t = _dot(a_exact, part)
        acc = t if acc is None else acc + t
    return acc


def _dot_hilo(a, b):
    ah = a.astype(BF16)
    al = (a - ah.astype(F32)).astype(BF16)
    bh = b.astype(BF16)
    bl = (b - bh.astype(F32)).astype(BF16)
    return _dot(ah, bh) + _dot(ah, bl) + _dot(al, bh)


def _split_hilo(w):
    hi = w.astype(BF16)
    return jnp.stack([hi, (w - hi.astype(F32)).astype(BF16)])


def _dot_hilo_presplit(a, w_ref):
    ah = a.astype(BF16)
    al = (a - ah.astype(F32)).astype(BF16)
    return _dot(ah, w_ref[0]) + _dot(ah, w_ref[1]) + _dot(al, w_ref[0])


def _sigmoid(x):
    return 0.5 * jnp.tanh(0.5 * x) + 0.5


def _softplus(x):
    return jnp.maximum(x, 0.0) + jnp.log(1.0 + jnp.exp(-jnp.abs(x)))


def _layer_norm(y, g, b, eps):
    mu = jnp.mean(y, -1, keepdims=True)
    yc = y - mu
    var = jnp.mean(yc * yc, -1, keepdims=True)
    return yc * lax.rsqrt(var + eps) * g + b


def _mod_kernel(c_ref, w_ref, b_ref, o_ref):
    c = c_ref[...]
    o_ref[0] = _dot_hilo(c * _sigmoid(c), w_ref[0]) + b_ref[0]


def _mod_call(c_pad, mod_w, mod_b):
    depth, d, n = mod_w.shape
    rows = c_pad.shape[0]
    tn = 1536
    return pl.pallas_call(
        _mod_kernel,
        grid=(depth, n // tn),
        in_specs=[pl.BlockSpec((rows, d), lambda i, j: (0, 0)),
                  pl.BlockSpec((1, d, tn), lambda i, j: (i, 0, j)),
                  pl.BlockSpec((1, 1, tn), lambda i, j: (i, 0, j))],
        out_specs=pl.BlockSpec((1, rows, tn), lambda i, j: (i, 0, j)),
        out_shape=jax.ShapeDtypeStruct((depth, rows, n), F32),
        compiler_params=_cparams(2),
        name="adaln_mod",
    )(c_pad, mod_w, mod_b.reshape(depth, 1, n))


def _rope(x, cos, sin):
    lane = lax.broadcasted_iota(jnp.int32, (x.shape[0], LANES), 1)
    first = (lane % 32) < 16
    parts = []
    for g in range(0, x.shape[1], LANES):
        xg = x[:, g:g + LANES]
        parts.append(jnp.where(first, pltpu.roll(xg, LANES - 16, 1), pltpu.roll(xg, 16, 1)))
    sw = parts[0] if len(parts) == 1 else jnp.concatenate(parts, 1)
    return x * cos + sw * sin


def _stream_specs(streams, n_ctx_tiles):
    d = streams[0].shape[-1]
    bp = BATCH_PACK
    if len(streams) == 1:
        return [pl.BlockSpec((bp, ROW_TILE, d), lambda bi, t: (bi, t, 0))]
    return [pl.BlockSpec((bp, ROW_TILE, d), lambda bi, t: (bi, jnp.minimum(t, n_ctx_tiles - 1), 0)),
            pl.BlockSpec((bp, ROW_TILE, d), lambda bi, t: (bi, jnp.maximum(t - n_ctx_tiles, 0), 0))]


def _stream_tile(refs, n_ctx_tiles):
    if len(refs) == 1:
        return refs[0][...]
    return jnp.where(pl.program_id(1) < n_ctx_tiles, refs[0][...], refs[1][...])


def _inproj_kernel(*refs, plan, n_streams, n_ctx_tiles):
    x_refs = refs[:n_streams]
    sh_ref, sc_ref, w_ref, cos_ref, sin_ref = refs[n_streams:n_streams + 5]
    o_refs = refs[n_streams + 5:]
    x = _stream_tile(x_refs, n_ctx_tiles)
    bp, rows, d = x.shape
    xm = (x * (1.0 + sc_ref[:, 0, 0]) + sh_ref[:, 0, 0]).reshape(bp * rows, d).astype(BF16)
    cos = jnp.concatenate([cos_ref[...]] * bp, 0)
    sin = jnp.concatenate([sin_ref[...]] * bp, 0)
    for o_ref, (c0, width, rope, mult, transpose) in zip(o_refs, plan):
        for j in range(0, width, 256):
            cw = min(256, width - j)
            acc = _dot(xm, w_ref[:, c0 + j:c0 + j + cw])
            if rope:
                acc = _rope(acc, cos[:, :cw], sin[:, :cw])
            if mult != 1.0:
                acc = acc * mult
            if transpose:
                for p in range(bp):
                    o_ref[p, j:j + cw, :] = acc[p * rows:(p + 1) * rows].T.astype(o_ref.dtype)
            else:
                o_ref[:, :, j:j + cw] = acc.reshape(bp, rows, cw).astype(o_ref.dtype)


def _inproj_call(streams, modsel, w_bf16, cos_t, sin_t, plan, out_dtypes, n_ctx_tiles, name):
    b, _, d = streams[0].shape
    l = sum(s.shape[1] for s in streams)
    n_tiles = l // ROW_TILE
    seg = lambda bi, t: (bi, (t >= n_ctx_tiles).astype(jnp.int32), 0, 0, 0)
    seg_scale = lambda bi, t: (bi, (t >= n_ctx_tiles).astype(jnp.int32), 1, 0, 0)
    out_shape, out_specs = [], []
    bp = BATCH_PACK
    for (_, width, _, _, transpose), dt in zip(plan, out_dtypes):
        if transpose:
            out_shape.append(jax.ShapeDtypeStruct((b, width, l), dt))
            out_specs.append(pl.BlockSpec((bp, width, ROW_TILE), lambda bi, t: (bi, 0, t)))
        else:
            out_shape.append(jax.ShapeDtypeStruct((b, l, width), dt))
            out_specs.append(pl.BlockSpec((bp, ROW_TILE, width), lambda bi, t: (bi, t, 0)))
    return pl.pallas_call(
        functools.partial(_inproj_kernel, plan=plan, n_streams=len(streams), n_ctx_tiles=n_ctx_tiles),
        grid=(b // bp, n_tiles),
        in_specs=_stream_specs(streams, n_ctx_tiles) + [
            pl.BlockSpec((bp, 1, 1, 1, d), seg),
            pl.BlockSpec((bp, 1, 1, 1, d), seg_scale),
            pl.BlockSpec(w_bf16.shape, lambda bi, t: (0, 0), pipeline_mode=pl.Buffered(1)),
            pl.BlockSpec((ROW_TILE, cos_t.shape[1]), lambda bi, t: (t, 0)),
            pl.BlockSpec((ROW_TILE, sin_t.shape[1]), lambda bi, t: (t, 0))],
        out_specs=out_specs,
        out_shape=out_shape,
        compiler_params=_cparams(2),
        name=name,
    )(*streams, modsel, modsel, w_bf16, cos_t, sin_t)


def _rwkv_prep_kernel(p_ref, hp_ref, hn_ref, mu_ref, bias_ref, lora_ref, g2_ref, kk_ref, ka_ref, rk_ref, bd_ref,
                      sel_ref, selt_ref, at_o, bt_o, kt_o, rt_o, pe_o, v_o, g_o, bonus_o, ps_ref,
                      *, n_ctx_tiles, n_tiles, chunk):
    t = pl.program_id(1)
    gw = v_o.shape[-1]
    rows, cols = p_ref.shape[1:]
    first = jnp.logical_or(t == 0, t == n_ctx_tiles)
    last = jnp.logical_or(t == n_ctx_tiles - 1, t == n_tiles - 1)
    half_mu = 0.5 * mu_ref[...]
    row8 = lax.broadcasted_iota(jnp.int32, (HALO, cols), 0)
    lane = lax.broadcasted_iota(jnp.int32, (rows, LANES), 1)
    bd = bd_ref[...]
    ri = lax.broadcasted_iota(jnp.int32, (rows, rows), 0)
    ci = lax.broadcasted_iota(jnp.int32, (rows, rows), 1)
    same_chunk = (ri // chunk) == (ci // chunk)
    tris = [jnp.logical_and(same_chunk, (ci <= ri) if d == 0 else (ci >= ri)).astype(BF16)
            for d in range(2)]
    for pb in range(p_ref.shape[0]):
        p = p_ref[pb]
        prev_row = jnp.where(first, 0.0, hp_ref[pb, HALO - 1:HALO, :])
        next_row = jnp.where(last, 0.0, hn_ref[pb, 0:1, :])
        ps = ps_ref.at[pb]
        ps[...] = (1.0 - mu_ref[...]) * p + half_mu * (pltpu.roll(p, 1, 0) + pltpu.roll(p, rows - 1, 0))
        ps[0:HALO, :] += jnp.where(row8 == 0, half_mu * (prev_row - p[rows - 1:rows, :]), 0.0)
        ps[rows - HALO:, :] += jnp.where(row8 == HALO - 1, half_mu * (next_row - p[0:1, :]), 0.0)

        r = ps[:, :gw]
        k = ps[:, gw:2 * gw]
        v = ps[:, 2 * gw:3 * gw]
        slab = ps[:, 3 * gw:3 * gw + LANES]
        xg = ps[:, 3 * gw + LANES:3 * gw + 2 * LANES]
        lora_in = jnp.where(lane < RWKV_LORA_W, jnp.tanh(slab), slab)
        pre = _dot_hilo_presplit(lora_in, lora_ref) + bias_ref[...]
        g_o[pb] = _dot_hilo_presplit(_sigmoid(xg), g2_ref)
        kk = k * kk_ref[...]
        ss = _dot_f32_by_exact(kk * kk, sel_ref[...], 2)
        kk = kk * _dot_f32_by_exact(lax.rsqrt(jnp.maximum(ss, 1e-24)), selt_ref[...], 2)
        bonus_o[pb] = _dot_f32_by_exact(r * k * rk_ref[...], bd, 2) * v
        v_o[pb] = v.astype(v_o.dtype)
        for d in range(2):
            lw = -math.exp(-0.5) * _sigmoid(pre[:, d * gw:(d + 1) * gw])
            a = _sigmoid(pre[:, (2 + d) * gw:(3 + d) * gw])
            cum = _dot_exact_by_f32(tris[d], lw, 2)
            p_inv = jnp.exp(-cum)
            at_o[pb, d] = (-kk * jnp.exp(cum - lw)).astype(at_o.dtype)
            bt_o[pb, d] = (kk * a * p_inv).astype(bt_o.dtype)
            kt_o[pb, d] = (k * (1.0 + (a - 1.0) * ka_ref[...]) * p_inv).astype(kt_o.dtype)
            rt_o[pb, d] = (r * jnp.exp(cum)).astype(rt_o.dtype)
            for j in range(rows // chunk):
                last_row = j * chunk + (chunk - 1 if d == 0 else 0)
                pe_o[pb, d, j] = jnp.exp(cum[last_row:last_row + 1, :])


def _rwkv_prep_call(p, mu, bias, lora_w, g2, k_k, k_a, r_k, bd, sel, sel_t, n_ctx_tiles):
    b, l, cols = p.shape
    gw = k_k.shape[-1]
    n_tiles = l // ROW_TILE
    hb = ROW_TILE // HALO
    n_hb = l // HALO
    cpt = ROW_TILE // RWKV_CHUNK
    bp = BATCH_PACK
    full = lambda a: pl.BlockSpec(a.shape, lambda bi, t: (0,) * a.ndim)
    tok = pl.BlockSpec((bp, ROW_TILE, gw), lambda bi, t: (bi, t, 0))
    tok2 = pl.BlockSpec((bp, 2, ROW_TILE, gw), lambda bi, t: (bi, 0, t, 0))
    s1 = jax.ShapeDtypeStruct((b, l, gw), F32)
    s2 = jax.ShapeDtypeStruct((b, 2, l, gw), BF16)
    return pl.pallas_call(
        functools.partial(_rwkv_prep_kernel, n_ctx_tiles=n_ctx_tiles, n_tiles=n_tiles, chunk=RWKV_CHUNK),
        grid=(b // bp, n_tiles),
        in_specs=[pl.BlockSpec((bp, ROW_TILE, cols), lambda bi, t: (bi, t, 0)),
                  pl.BlockSpec((bp, HALO, cols), lambda bi, t: (bi, jnp.maximum(t * hb - 1, 0), 0)),
                  pl.BlockSpec((bp, HALO, cols), lambda bi, t: (bi, jnp.minimum((t + 1) * hb, n_hb - 1), 0)),
                  full(mu), full(bias), full(lora_w), full(g2), full(k_k), full(k_a), full(r_k), full(bd),
                  full(sel), full(sel_t)],
        out_specs=[tok2, tok2, tok2, tok2,
                   pl.BlockSpec((bp, 2, cpt, 1, gw), lambda bi, t: (bi, 0, t, 0, 0)), tok, tok, tok],
        out_shape=[s2, s2, s2, s2, jax.ShapeDtypeStruct((b, 2, l // RWKV_CHUNK, 1, gw), F32),
                   jax.ShapeDtypeStruct((b, l, gw), BF16), s1, s1],
        scratch_shapes=[pltpu.VMEM((bp, ROW_TILE, cols), F32)],
        compiler_params=_cparams(2),
        name="rwkv_prep",
    )(p, p, p, mu, bias, lora_w, g2, k_k, k_a, r_k, bd, sel, sel_t)


def _rwkv_scan_kernel(atf_ref, btf_ref, ktf_ref, rtf_ref, pef_ref, vf_ref, atr_ref, btr_ref, ktr_ref, rtr_ref, per_ref,
                      vr_ref, yf_ref, yr_ref, s_ref):
    i = pl.program_id(1)
    c = vf_ref.shape[1]
    n_pairs = vf_ref.shape[2] // LANES
    c2 = 2 * c

    @pl.when(i == 0)
    def _():
        s_ref[...] = jnp.zeros_like(s_ref)

    row2 = lax.broadcasted_iota(jnp.int32, (c2, c2), 0)
    col2 = lax.broadcasted_iota(jnp.int32, (c2, c2), 1)
    same = (row2 // c) == (col2 // c)
    eye = (row2 == col2).astype(F32)
    low = lax.broadcasted_iota(jnp.int32, (c, LANES), 1) < HEAD_DIM

    def stack(x):
        xf = x.astype(F32)
        return jnp.concatenate([jnp.where(low, xf, 0.0), jnp.where(low, 0.0, xf)], 0)

    chains = []
    dirs = ((1, atf_ref, btf_ref, ktf_ref, rtf_ref, pef_ref, vf_ref, yf_ref),
            (-1, atr_ref, btr_ref, ktr_ref, rtr_ref, per_ref, vr_ref, yr_ref))
    for d, (sgn, at_ref, bt_ref, kt_ref, rt_ref, pe_ref, v_ref, y_ref) in enumerate(dirs):
        dt2 = (row2 - col2) * sgn
        strict = jnp.logical_and(same, dt2 > 0)
        incl = jnp.logical_and(same, dt2 >= 0)
        for p in range(vf_ref.shape[0]):
            for pr in range(n_pairs):
                sl = slice(pr * LANES, (pr + 1) * LANES)
                b_s = bt_ref[p, 0, :, sl]
                k_s = kt_ref[p, 0, :, sl]
                chains.append(dict(st=(p, d, pr), p=p, sl=sl, y_ref=y_ref, strict=strict, incl=incl,
                                   p_end=pe_ref[p, 0, 0, :, sl], vh=stack(v_ref[p, :, sl]),
                                   ar=jnp.concatenate([stack(at_ref[p, 0, :, sl]), stack(rt_ref[p, 0, :, sl])],
                                                      0).astype(BF16),
                                   bk2=jnp.concatenate([b_s, k_s], 0),
                                   bk=jnp.concatenate([b_s, b_s, k_s, k_s], 0)))

    low2 = lax.broadcasted_iota(jnp.int32, (c2, c2), 1) < c
    for ch in chains:
        ch["s0"] = s_ref[ch["st"]]
        prod = _dot_nt(ch["ar"], jnp.concatenate([ch["bk2"], ch["s0"].astype(BF16)], 0))
        ch["ars"] = prod[:, c2:]
        a_bk = prod[:c2, :c2]
        r_bk = prod[c2:, :c2]
        a_kb = pltpu.roll(a_bk, c, 1)
        r_kb = pltpu.roll(r_bk, c, 1)
        l_ab = jnp.where(ch["strict"], jnp.where(low2, a_bk, a_kb), 0.0)
        ch["l_ak"] = jnp.where(ch["strict"], jnp.where(low2, a_kb, a_bk), 0.0).astype(BF16)
        ch["l_r"] = jnp.concatenate([jnp.where(ch["incl"], jnp.where(low2, r_bk, r_kb), 0.0),
                                     jnp.where(ch["incl"], jnp.where(low2, r_kb, r_bk), 0.0)], 1).astype(BF16)
        ch["inv"] = eye + l_ab
        ch["pw"] = l_ab.astype(BF16)
    for ch in chains:
        ch["pw"] = _dot(ch["pw"], ch["pw"]).astype(BF16)
    for _ in range(max(0, int(math.ceil(math.log2(c))) - 2)):
        for ch in chains:
            both = _dot(ch["pw"], jnp.concatenate([ch["inv"].astype(BF16), ch["pw"]], 1))
            ch["inv"] = ch["inv"] + both[:, :c2]
            ch["pw"] = both[:, c2:].astype(BF16)
    for ch in chains:
        ch["inv"] = (ch["inv"] + _dot(ch["pw"], ch["inv"].astype(BF16))).astype(BF16)
    for ch in chains:
        ch["lv"] = _dot(ch["l_ak"], ch["vh"].astype(BF16))
    for ch in chains:
        u = _dot(ch["inv"], (ch["ars"][:c2] + ch["lv"]).astype(BF16))
        ch["uv"] = jnp.concatenate([u, ch["vh"]], 0)
    for ch in chains:
        y = ch["ars"][c2:] + _dot(ch["l_r"], ch["uv"].astype(BF16))
        ch["y_ref"][ch["p"], :, ch["sl"]] = y[:c] + y[c:]
    for ch in chains:
        upd = (ch["s0"] + _dot(ch["uv"].T.astype(BF16), ch["bk"])) * ch["p_end"]
        s_ref[ch["st"]] = jnp.where(same, upd, 0.0)


def _rwkv_scan_call(at, bt, kt, rt, pe, v, n_ctx):
    b, l, gw = v.shape
    c = RWKV_CHUNK
    nc = l // c
    ncc = n_ctx // c

    def rev(i):
        return jnp.where(i < ncc, ncc - 1 - i, nc - 1 - (i - ncc))

    bp = SCAN_BATCH_PACK if b % SCAN_BATCH_PACK == 0 else BATCH_PACK

    def specs(d, ch):
        tokd = pl.BlockSpec((bp, 1, c, gw), lambda bi, i: (bi, d, ch(i), 0))
        return [tokd, tokd, tokd, tokd, pl.BlockSpec((bp, 1, 1, 1, gw), lambda bi, i: (bi, d, ch(i), 0, 0)),
                pl.BlockSpec((bp, c, gw), lambda bi, i: (bi, ch(i), 0))]

    ys = jax.ShapeDtypeStruct((b, l, gw), F32)
    return pl.pallas_call(
        _rwkv_scan_kernel,
        grid=(b // bp, nc),
        in_specs=specs(0, lambda i: i) + specs(1, rev),
        out_specs=[pl.BlockSpec((bp, c, gw), lambda bi, i: (bi, i, 0)),
                   pl.BlockSpec((bp, c, gw), lambda bi, i: (bi, rev(i), 0))],
        out_shape=[ys, ys],
        scratch_shapes=[pltpu.VMEM((bp, 2, gw // LANES, LANES, LANES), F32)],
        compiler_params=_cparams(2),
        name="rwkv_scan",
    )(at, bt, kt, rt, pe, v, at, bt, kt, rt, pe, v)


def _rwkv_mixer_out(yf_ref, yr_ref, bonus_ref, g_ref, gg_ref, gb_ref):
    flat = lambda r: r[...].reshape(-1, r.shape[-1])
    y = flat(yf_ref) + flat(yr_ref)
    low = lax.broadcasted_iota(jnp.int32, (y.shape[0], LANES), 1) < HEAD_DIM

    def head_mean(t):
        tot = jnp.sum(t, -1, keepdims=True)
        lo = jnp.sum(jnp.where(low, t, 0.0), -1, keepdims=True)
        return jnp.where(low, lo, tot - lo) * (1.0 / HEAD_DIM)

    outs = []
    for s in range(0, y.shape[1], LANES):
        ys = y[:, s:s + LANES]
        yc = ys - head_mean(ys)
        outs.append(yc * lax.rsqrt(head_mean(yc * yc) + RWKV_GN_EPS))
    yn = jnp.concatenate(outs, 1) * gg_ref[...] + gb_ref[...]
    return (yn + flat(bonus_ref)) * flat(g_ref)


def _diff_attn_kernel(lam_ref, q_ref, k_ref, vt_ref, g_ref, o_ref, *, n_ctx, n_ctx_tiles, out_scale):
    t = pl.program_id(1)
    lam = lam_ref[0]
    vd = 2 * HEAD_DIM
    tq = q_ref.shape[1]
    low = lax.broadcasted_iota(jnp.int32, (tq, vd), 1) < HEAD_DIM

    def run(nk):
        def logits(u):
            p, hm = divmod(u, 2 * DIFF_HEADS)
            h, m = divmod(hm, 2)
            qf = q_ref[p, :, h * vd:(h + 1) * vd].astype(F32)
            qm = jnp.where(low == (m == 0), qf, 0.0).astype(BF16)
            return _dot_nt(k_ref[p, :nk, h * vd:(h + 1) * vd], qm)

        n_units = 2 * DIFF_HEADS * q_ref.shape[0]
        ahead = DIFF_QK_AHEAD
        pending = [logits(u) for u in range(min(ahead, n_units))]
        es, invs = [], []
        for u in range(n_units):
            s = pending.pop(0)
            if u + ahead < n_units:
                pending.append(logits(u + ahead))
            e = jnp.exp2(s - jnp.max(s, 0, keepdims=True))
            invs.append(1.0 / jnp.sum(e, 0, keepdims=True))
            es.append(e.astype(BF16))
            if u % 2 == 1:
                p, hm = divmod(u, 2 * DIFF_HEADS)
                h = hm // 2
                pv = _dot(vt_ref[p, h * vd:(h + 1) * vd, :nk], jnp.concatenate(es, 1))
                o = pv[:, :tq] * invs[0] - pv[:, tq:] * (lam * invs[1])
                o = o * lax.rsqrt(jnp.mean(o * o, 0, keepdims=True) + LN_EPS) * (g_ref[...] * out_scale)
                o_ref[p, :, h * vd:(h + 1) * vd] = o.T
                es, invs = [], []

    @pl.when(t < n_ctx_tiles)
    def _():
        run(n_ctx)

    @pl.when(t >= n_ctx_tiles)
    def _():
        run(k_ref.shape[1])


def _diff_attn_call(lam, q, k, vt, subln_g_col, n_ctx, lam_init):
    b, l, w = q.shape
    bp = BATCH_PACK
    tok = pl.BlockSpec((bp, ROW_TILE, w), lambda bi, t: (bi, t, 0))
    return pl.pallas_call(
        functools.partial(_diff_attn_kernel, n_ctx=n_ctx, n_ctx_tiles=n_ctx // ROW_TILE, out_scale=1.0 - lam_init),
        grid=(b // bp, l // ROW_TILE),
        in_specs=[pl.BlockSpec(memory_space=pltpu.SMEM), tok,
                  pl.BlockSpec((bp, l, w), lambda bi, t: (bi, 0, 0)),
                  pl.BlockSpec((bp, w, l), lambda bi, t: (bi, 0, 0)),
                  pl.BlockSpec(subln_g_col.shape, lambda bi, t: (0, 0))],
        out_specs=tok,
        out_shape=jax.ShapeDtypeStruct((b, l, w), F32),
        compiler_params=_cparams(2),
        name="diff_attn",
    )(lam, q, k, vt, subln_g_col)


def _outproj_ffn_kernel(*refs, alpha, n_streams, n_ctx_tiles, mixer_out, n_mix):
    o2_ref = refs[n_mix]
    h_refs = refs[n_mix + 1:n_mix + 1 + n_streams]
    (gm_ref, sh_ref, sc_ref, gf_ref, wo_ref, w1_ref, w2_ref, lmg_ref, lmb_ref, lfg_ref, lfb_ref,
     out_ref) = refs[n_mix + 1 + n_streams:]
    bp, rows, gw = o2_ref.shape
    flat = lambda t: t.reshape(bp * rows, t.shape[-1])
    per_b = lambda t: t.reshape(bp, rows, t.shape[-1])
    o1 = mixer_out(*refs[:n_mix])
    o = _dot(o1.astype(BF16), wo_ref[:gw, :]) + _dot(flat(o2_ref[...]).astype(BF16), wo_ref[gw:, :])
    h_in = _stream_tile(h_refs, n_ctx_tiles)
    h1 = _layer_norm(flat(alpha * h_in + gm_ref[:, 0, 0] * per_b(o)), lmg_ref[...], lmb_ref[...], LN_EPS)
    xm = flat(per_b(h1) * (1.0 + sc_ref[:, 0, 0]) + sh_ref[:, 0, 0]).astype(BF16)
    hidden = w1_ref.shape[1]
    step = 512

    def up(j):
        hj = jnp.maximum(_dot(xm, w1_ref[:, j:j + step]), 0.0)
        return (hj * hj).astype(BF16)

    acc = None
    offs = list(range(0, hidden, step))
    acts = [up(j) for j in offs[:FFN_UP_AHEAD]]
    for i, j in enumerate(offs):
        if i + FFN_UP_AHEAD < len(offs):
            acts.append(up(offs[i + FFN_UP_AHEAD]))
        part = _dot(acts.pop(0), w2_ref[j:j + step, :])
        acc = part if acc is None else acc + part
    out = _layer_norm(alpha * h1 + flat(gf_ref[:, 0, 0] * per_b(acc)), lfg_ref[...], lfb_ref[...], LN_EPS)
    out_ref[...] = per_b(out)


def _outproj_ffn_call(mixer_out, mix_tok, mix_par, o2, streams, modsel, wo, w1, w2, lmg, lmb, lfg, lfb, *, h_off,
                      n_out_tiles, n_ctx_tiles, alpha, name):
    b, _, gw = o2.shape
    d = streams[0].shape[-1]
    bp = BATCH_PACK
    assert len(streams) == 1 or h_off == 0
    seg = lambda col: (lambda bi, t: (bi, ((t + h_off) >= n_ctx_tiles).astype(jnp.int32), col, 0, 0))
    modspec = lambda col: pl.BlockSpec((bp, 1, 1, 1, d), seg(col))
    full = lambda a: pl.BlockSpec(a.shape, lambda bi, t: (0,) * a.ndim, pipeline_mode=pl.Buffered(1))
    mtok = pl.BlockSpec((bp, ROW_TILE, gw), lambda bi, t: (bi, t + h_off, 0))
    otok = pl.BlockSpec((bp, ROW_TILE, gw), lambda bi, t: (bi, t, 0))
    if len(streams) == 1:
        h_specs = [pl.BlockSpec((bp, ROW_TILE, d), lambda bi, t: (bi, t + h_off, 0))]
    else:
        h_specs = _stream_specs(streams, n_ctx_tiles)
    n_mix = len(mix_tok) + len(mix_par)
    return pl.pallas_call(
        functools.partial(_outproj_ffn_kernel, alpha=alpha, n_streams=len(streams), n_ctx_tiles=n_ctx_tiles,
                          mixer_out=mixer_out, n_mix=n_mix),
        grid=(b // bp, n_out_tiles),
        in_specs=[mtok] * len(mix_tok) + [full(p) for p in mix_par] + [otok] + h_specs + [
            modspec(2), modspec(3), modspec(4), modspec(5),
            full(wo), full(w1), full(w2), full(lmg), full(lmb), full(lfg), full(lfb)],
        out_specs=pl.BlockSpec((bp, ROW_TILE, d), lambda bi, t: (bi, t, 0)),
        out_shape=jax.ShapeDtypeStruct((b, n_out_tiles * ROW_TILE, d), F32),
        compiler_params=_cparams(2),
        name=name,
    )(*mix_tok, *mix_par, o2, *streams, modsel, modsel, modsel, modsel, wo, w1, w2, lmg, lmb, lfg, lfb)


def _ssd_prep_kernel(x_ref, hp_ref, hn_ref, dt_ref, cw_ref, cb_ref, dtb_ref, an_ref, xc_o, dt_o, adt_o,
                     *, n_ctx_tiles, n_tiles):
    t = pl.program_id(1)
    x = x_ref[0]
    rows = x.shape[0]
    first = jnp.logical_or(t == 0, t == n_ctx_tiles)
    last = jnp.logical_or(t == n_ctx_tiles - 1, t == n_tiles - 1)
    ext = jnp.concatenate([jnp.where(first, 0.0, hp_ref[0]), x, jnp.where(last, 0.0, hn_ref[0])], 0)
    n_ext = rows + 2 * HALO
    acc = cb_ref[...] + cw_ref[SSD_CONV // 2:SSD_CONV // 2 + 1, :] * x
    for j in range(SSD_CONV):
        s = j - SSD_CONV // 2
        if s == 0:
            continue
        shifted = pltpu.roll(ext, (n_ext - s) % n_ext, 0)[HALO:HALO + rows]
        acc = acc + cw_ref[j:j + 1, :] * shifted
    xc_o[0] = acc * _sigmoid(acc)
    dt = _softplus(dt_ref[0] + dtb_ref[...])
    dt_o[0] = dt
    adt_o[0] = dt * an_ref[...]


def _ssd_prep_call(xbc, dt, conv_w, conv_b, dt_bias, a_neg, n_ctx_tiles):
    b, l, cw = xbc.shape
    n_tiles = l // ROW_TILE
    hb = ROW_TILE // HALO
    n_hb = l // HALO
    full = lambda a: pl.BlockSpec(a.shape, lambda bi, t: (0,) * a.ndim)
    tokd = pl.BlockSpec((1, ROW_TILE, LANES), lambda bi, t: (bi, t, 0))
    sd = jax.ShapeDtypeStruct((b, l, LANES), F32)
    return pl.pallas_call(
        functools.partial(_ssd_prep_kernel, n_ctx_tiles=n_ctx_tiles, n_tiles=n_tiles),
        grid=(b, n_tiles),
        in_specs=[pl.BlockSpec((1, ROW_TILE, cw), lambda bi, t: (bi, t, 0)),
                  pl.BlockSpec((1, HALO, cw), lambda bi, t: (bi, jnp.maximum(t * hb - 1, 0), 0)),
                  pl.BlockSpec((1, HALO, cw), lambda bi, t: (bi, jnp.minimum((t + 1) * hb, n_hb - 1), 0)),
                  tokd, full(conv_w), full(conv_b), full(dt_bias), full(a_neg)],
        out_specs=[pl.BlockSpec((1, ROW_TILE, cw), lambda bi, t: (bi, t, 0)), tokd, tokd],
        out_shape=[jax.ShapeDtypeStruct((b, l, cw), F32), sd, sd],
        compiler_params=_cparams(2),
        name="ssd_prep",
    )(xbc, xbc, xbc, dt, conv_w, conv_b, dt_bias, a_neg)


def _ssd_scan_kernel(xcf_ref, dtf_ref, adtf_ref, adttf_ref, xcr_ref, dtr_ref, adtr_ref, adttr_ref, e_ref,
                     yf_ref, yr_ref, st_ref):
    i = pl.program_id(1)
    q = xcf_ref.shape[1]
    gw = yf_ref.shape[-1]
    n = SSD_STATE
    hpg = SSD_HEADS // SSD_GROUPS
    gcols = hpg * HEAD_DIM

    @pl.when(i == 0)
    def _():
        st_ref[...] = jnp.zeros_like(st_ref)

    row = lax.broadcasted_iota(jnp.int32, (q, q), 0)
    col = lax.broadcasted_iota(jnp.int32, (q, q), 1)
    chains = []
    dirs = ((1, xcf_ref, dtf_ref, adtf_ref, adttf_ref, yf_ref), (-1, xcr_ref, dtr_ref, adtr_ref, adttr_ref, yr_ref))
    for p, d in [(p, d) for p in range(xcf_ref.shape[0]) for d in range(2)]:
        sgn, xc_ref, dt_ref, adt_ref, adtt_ref, y_ref = dirs[d]
        incl = (row - col) * sgn >= 0
        tri = incl.astype(BF16)
        tri_t = ((col - row) * sgn >= 0).astype(BF16)
        cs = _dot_exact_by_f32(tri, adt_ref[p])
        cs_t = _dot_f32_by_exact(adtt_ref[p], tri_t)
        e = e_ref[d]
        cse = _dot_f32_by_exact(cs, e)
        dte = _dot_f32_by_exact(dt_ref[p], e, 2)
        tot = cse[q - 1:q, :] if sgn > 0 else cse[0:1, :]
        xc = xc_ref[p]
        xdt = xc[:, :gw] * dte
        e_cs = jnp.exp(cse)
        x_end = (xdt * jnp.exp(tot - cse)).astype(BF16)
        e_tot = jnp.exp(tot)
        xdt16 = xdt.astype(BF16)
        for g in range(SSD_GROUPS):
            gsl = slice(g * gcols, (g + 1) * gcols)
            lms = []
            for hh in range(hpg):
                h = g * hpg + hh
                colv = cse[:, h * HEAD_DIM:h * HEAD_DIM + 1]
                rowv = cs_t[d * SSD_HEADS + h:d * SSD_HEADS + h + 1, :]
                lms.append(jnp.where(incl, jnp.exp(colv - rowv), 0.0))
            chains.append(dict(st=(p, d, g), p=p, gsl=gsl, y_ref=y_ref, lms=lms, e_cs=e_cs[:, gsl], e_tot=e_tot[:, gsl],
                               x_end=x_end[:, gsl], xdt16=xdt16[:, gsl],
                               bg=xc[:, gw + g * n:gw + (g + 1) * n],
                               cg=xc[:, gw + (SSD_GROUPS + g) * n:gw + (SSD_GROUPS + g + 1) * n].astype(BF16)))
    for ch in chains:
        ch["cb"] = _dot_nt(ch["cg"], ch["bg"].astype(BF16))
        ch["s0"] = st_ref[ch["st"]]
    for ch in chains:
        ch["y_off"] = _dot(ch["cg"], ch["s0"].astype(BF16)) * ch["e_cs"]
    for ch in chains:
        ch["ys"] = [_dot((ch["cb"] * ch["lms"][hh]).astype(BF16), ch["xdt16"][:, hh * HEAD_DIM:(hh + 1) * HEAD_DIM])
                    for hh in range(hpg)]
    for ch in chains:
        ch["y_ref"][ch["p"], :, ch["gsl"]] = jnp.concatenate(ch["ys"], 1) + ch["y_off"]
        st_ref[ch["st"]] = ch["s0"] * ch["e_tot"] + _dot(ch["bg"].T.astype(BF16), ch["x_end"])


def _ssd_scan_call(xc, dt, adt, adt_t, e_sel, n_ctx):
    b, l, cw = xc.shape
    gw = e_sel.shape[-1]
    q = SSD_CHUNK
    nc = l // q
    ncc = n_ctx // q

    def rev(i):
        return jnp.where(i < ncc, ncc - 1 - i, nc - 1 - (i - ncc))

    bp = BATCH_PACK

    def specs(ch):
        tokd = pl.BlockSpec((bp, q, LANES), lambda bi, i: (bi, ch(i), 0))
        return [pl.BlockSpec((bp, q, cw), lambda bi, i: (bi, ch(i), 0)), tokd, tokd,
                pl.BlockSpec((bp, 2 * SSD_HEADS, q), lambda bi, i: (bi, 0, ch(i)))]

    ys = jax.ShapeDtypeStruct((b, l, gw), F32)
    return pl.pallas_call(
        _ssd_scan_kernel,
        grid=(b // bp, nc),
        in_specs=specs(lambda i: i) + specs(rev) + [pl.BlockSpec(e_sel.shape, lambda bi, i: (0, 0, 0))],
        out_specs=[pl.BlockSpec((bp, q, gw), lambda bi, i: (bi, i, 0)),
                   pl.BlockSpec((bp, q, gw), lambda bi, i: (bi, rev(i), 0))],
        out_shape=[ys, ys],
        scratch_shapes=[pltpu.VMEM((bp, 2, SSD_GROUPS, SSD_STATE, gw // SSD_GROUPS), F32)],
        compiler_params=_cparams(2),
        name="ssd_scan",
    )(xc, dt, adt, adt_t, xc, dt, adt, adt_t, e_sel)


def _ssd_mixer_out(yf_ref, yr_ref, xs_ref, z_ref, dsk_ref, ng_ref):
    flat = lambda r: r[...].reshape(-1, r.shape[-1])
    z = flat(z_ref)
    y = (flat(yf_ref) + flat(yr_ref) + dsk_ref[...] * flat(xs_ref)) * (z * _sigmoid(z))
    gwid = y.shape[1] // SSD_GROUPS
    outs = []
    for g in range(SSD_GROUPS):
        yg = y[:, g * gwid:(g + 1) * gwid]
        outs.append(yg * lax.rsqrt(jnp.mean(yg * yg, -1, keepdims=True) + LN_EPS))
    return jnp.concatenate(outs, 1) * ng_ref[...]


def _swa_kernel(sink_ref, q_ref, k_ref, v_ref, o_ref, *, n_ctx, n_lat):
    i = pl.program_id(1)
    tq = q_ref.shape[1]
    span = tq + 2 * SWA_WINDOW
    nk = n_ctx + span
    group = SWA_HEADS // SWA_KV_HEADS
    start = jnp.clip(i * tq - SWA_WINDOW, 0, n_lat - span)
    w0 = pl.multiple_of(n_ctx + start, LANES)
    kpos = start + lax.broadcasted_iota(jnp.int32, (span, tq), 0)
    qpos = i * tq + lax.broadcasted_iota(jnp.int32, (span, tq), 1)
    bias = jnp.where(jnp.abs(qpos - kpos) <= SWA_WINDOW, 0.0, -1e30)
    bias = jnp.concatenate([jnp.zeros((n_ctx, tq), F32), bias], 0)
    bias = jnp.concatenate([bias] * group, 1)
    k_low = lax.broadcasted_iota(jnp.int32, (nk, LANES), 1) < HEAD_DIM
    v_low = lax.broadcasted_iota(jnp.int32, (LANES, nk), 0) < HEAD_DIM
    sinks = [jnp.concatenate([jnp.full((1, tq), sink_ref[kvh * group + g], F32) for g in range(group)], 1)
             for kvh in range(SWA_KV_HEADS)]
    bps = range(q_ref.shape[0])
    vts, ss, es, rdens = [], {}, {}, {}
    for p in bps:
        kcat = jnp.concatenate([k_ref[p, :n_ctx, :], k_ref[p, pl.ds(w0, span), :]], 0).astype(F32)
        vts.append(jnp.concatenate([v_ref[p, :n_ctx, :], v_ref[p, pl.ds(w0, span), :]], 0).astype(F32).T)
        qs = jnp.concatenate([q_ref[p, :, g * LANES:(g + 1) * LANES] for g in range(group)], 0)
        for kvh in range(SWA_KV_HEADS):
            km = jnp.where(k_low == (kvh == 0), kcat, 0.0).astype(BF16)
            ss[p, kvh] = _dot_nt(km, qs) + bias
    for p in bps:
        for kvh in range(SWA_KV_HEADS):
            m = jnp.maximum(jnp.max(ss[p, kvh], 0, keepdims=True), sinks[kvh])
            e = jnp.exp2(ss[p, kvh] - m)
            rdens[p, kvh] = 1.0 / (jnp.exp2(sinks[kvh] - m) + jnp.sum(e, 0, keepdims=True))
            es[p, kvh] = e.astype(BF16)
    for p in bps:
        ot = (_dot(jnp.where(v_low, vts[p], 0.0).astype(BF16), es[p, 0]) * rdens[p, 0]
              + _dot(jnp.where(v_low, 0.0, vts[p]).astype(BF16), es[p, 1]) * rdens[p, 1])
        for g in range(group):
            o_ref[p, :, g * LANES:(g + 1) * LANES] = ot[:, g * tq:(g + 1) * tq].T


def _swa_call(sink, q, k, v, n_ctx):
    b, l, qw = q.shape
    kw = k.shape[-1]
    n_lat = l - n_ctx
    off = n_ctx // SWA_TILE
    bp = BATCH_PACK
    kv = pl.BlockSpec((bp, l, kw), lambda bi, t: (bi, 0, 0))
    return pl.pallas_call(
        functools.partial(_swa_kernel, n_ctx=n_ctx, n_lat=n_lat),
        grid=(b // bp, n_lat // SWA_TILE),
        in_specs=[pl.BlockSpec(memory_space=pltpu.SMEM),
                  pl.BlockSpec((bp, SWA_TILE, qw), lambda bi, t: (bi, t + off, 0)), kv, kv],
        out_specs=pl.BlockSpec((bp, SWA_TILE, qw), lambda bi, t: (bi, t, 0)),
        out_shape=jax.ShapeDtypeStruct((b, n_lat, qw), F32),
        compiler_params=_cparams(2),
        name="swa",
    )(sink, q, k, v)


def _rope_tables(n_ctx, n_lat, width):
    half = HEAD_DIM // 4
    inv = ROPE_BASE ** (-jnp.arange(half, dtype=F32) / half)
    pos = jnp.arange(n_lat, dtype=jnp.int32)
    rows = (pos // GRID_W).astype(F32)[:, None] * inv
    cols = (pos % GRID_W).astype(F32)[:, None] * inv
    cos = jnp.concatenate([jnp.cos(rows), jnp.cos(rows), jnp.cos(cols), jnp.cos(cols)], -1)
    sin = jnp.concatenate([-jnp.sin(rows), jnp.sin(rows), -jnp.sin(cols), jnp.sin(cols)], -1)
    cos = jnp.concatenate([jnp.ones((n_ctx, HEAD_DIM), F32), cos], 0)
    sin = jnp.concatenate([jnp.zeros((n_ctx, HEAD_DIM), F32), sin], 0)
    reps = width // HEAD_DIM
    return jnp.tile(cos, (1, reps)), jnp.tile(sin, (1, reps))


def kernel(x, c, ctx, c_ctx, mod_w, mod_b, ln_mix_g, ln_mix_b, ln_ffn_g, ln_ffn_b, ffn_w1, ffn_w2, w_out, ab_w_in, rwkv_mu, rwkv_w0, rwkv_w2, rwkv_a0, rwkv_a2, rwkv_g2, rwkv_k_k, rwkv_k_a, rwkv_r_k, rwkv_gn_g, rwkv_gn_b, diff_lq1, diff_lk1, diff_lq2, diff_lk2, diff_subln_g, cd_w_in, ssd_conv_w, ssd_conv_b, ssd_dt_bias, ssd_a_log, ssd_d, ssd_norm_g, swa_sink):
    b, n_lat, d = x.shape
    n_ctx = ctx.shape[1]
    depth = mod_w.shape[0]
    assert depth == 2 and n_ctx % ROW_TILE == 0 and n_lat % ROW_TILE == 0 and n_lat % GRID_W == 0
    assert n_lat >= SWA_TILE + 2 * SWA_WINDOW and b % BATCH_PACK == 0
    gw = d // 2
    n_ctx_tiles = n_ctx // ROW_TILE
    n_lat_tiles = n_lat // ROW_TILE
    alpha = (2 * depth) ** 0.25
    log2e = math.log2(math.e)
    q_scale = HEAD_DIM ** -0.5 * log2e

    rows_pad = -(-(b + 1) // 8) * 8
    c_pad = jnp.zeros((rows_pad, d), F32).at[:b].set(c).at[b].set(c_ctx)
    m = _mod_call(c_pad, mod_w, mod_b)

    def modsel(i):
        return jnp.stack([jnp.broadcast_to(m[i, b], (b, 6 * d)), m[i, :b]], 1).reshape(b, 2, 6, 1, d)

    cos_t, sin_t = _rope_tables(n_ctx, n_lat, 256)
    h0 = (ctx, x)

    rwkv_cols = 3 * gw + RWKV_LORA_W + RWKV_LORA_A + RWKV_LORA_G
    plan0 = ((0, rwkv_cols, False, 1.0, False), (rwkv_cols, gw, True, q_scale, False),
             (rwkv_cols + gw, gw, True, 1.0, False), (rwkv_cols + 2 * gw, gw, False, 1.0, True))
    p_rwkv, dq, dk, dvt = _inproj_call(h0, modsel(0), ab_w_in[0].astype(BF16), cos_t, sin_t, plan0,
                                       (F32, BF16, BF16, BF16), n_ctx_tiles, "inproj_ab")

    zw = jnp.zeros((RWKV_LORA_W, gw), F32)
    lora_w = jnp.concatenate([jnp.concatenate([rwkv_w2[0, 0], rwkv_w2[0, 1], zw, zw], 1),
                              jnp.concatenate([zw, zw, rwkv_a2[0, 0], rwkv_a2[0, 1]], 1)], 0)
    lora_b = jnp.concatenate([rwkv_w0[0, 0], rwkv_w0[0, 1], rwkv_a0[0, 0], rwkv_a0[0, 1]])[None, :]
    head_id = jnp.arange(gw, dtype=jnp.int32) // HEAD_DIM
    bd = (head_id[:, None] == head_id[None, :]).astype(BF16)
    sel = (head_id[:, None] == jnp.arange(LANES, dtype=jnp.int32)[None, :]).astype(BF16)
    at, bt, kt, rt, pe, rv, g, bonus = _rwkv_prep_call(p_rwkv, rwkv_mu[0][None, :], lora_b, _split_hilo(lora_w), _split_hilo(rwkv_g2[0]),
                                                       rwkv_k_k[0][None, :], rwkv_k_a[0][None, :],
                                                       rwkv_r_k[0].reshape(1, gw), bd, sel, sel.T, n_ctx_tiles)
    y_f, y_r = _rwkv_scan_call(at, bt, kt, rt, pe, rv, n_ctx)

    lam_init0 = 0.8 - 0.6 * math.exp(-0.3 * 0)
    lam = (jnp.exp(jnp.sum(diff_lq1[0] * diff_lk1[0])) - jnp.exp(jnp.sum(diff_lq2[0] * diff_lk2[0]))
           + lam_init0).reshape(1).astype(F32)
    o2 = _diff_attn_call(lam, dq, dk, dvt, diff_subln_g[0][:, None], n_ctx, lam_init0)

    row = lambda t: t[None, :]
    h1 = _outproj_ffn_call(_rwkv_mixer_out, (y_f, y_r, bonus, g), (row(rwkv_gn_g[0]), row(rwkv_gn_b[0])), o2, h0,
                           modsel(0), w_out[0].astype(BF16), ffn_w1[0].astype(BF16),
                           ffn_w2[0].astype(BF16), row(ln_mix_g[0]), row(ln_mix_b[0]), row(ln_ffn_g[0]),
                           row(ln_ffn_b[0]), h_off=0, n_out_tiles=n_ctx_tiles + n_lat_tiles,
                           n_ctx_tiles=n_ctx_tiles, alpha=alpha, name="outproj_ffn_0")

    conv_dim = gw + 2 * SSD_GROUPS * SSD_STATE
    ssd_cols = gw + conv_dim + 2 * SSD_HEADS
    kvw = SWA_KV_HEADS * HEAD_DIM
    wcd = cd_w_in[0]
    group = SWA_HEADS // SWA_KV_HEADS
    q_perm = jnp.arange(gw, dtype=jnp.int32).reshape(SWA_KV_HEADS, group, HEAD_DIM).transpose(1, 0, 2).reshape(-1)
    w1cat = jnp.concatenate([wcd[:, ssd_cols:ssd_cols + gw][:, q_perm], wcd[:, ssd_cols + gw:], wcd[:, :gw],
                             wcd[:, gw:gw + conv_dim], wcd[:, gw + conv_dim:ssd_cols],
                             jnp.zeros((d, LANES - 2 * SSD_HEADS), F32)], 1)
    w_out1 = jnp.concatenate([w_out[1, :gw], w_out[1, gw:][q_perm]], 0)
    c_q, c_k, c_v, c_z, c_x, c_dt = 0, gw, gw + kvw, gw + 2 * kvw, 2 * gw + 2 * kvw, 2 * gw + 2 * kvw + conv_dim
    plan1 = ((c_q, gw, True, q_scale, False), (c_k, kvw, True, 1.0, False), (c_v, kvw, False, 1.0, False),
             (c_z, gw, False, 1.0, False), (c_x, conv_dim, False, 1.0, False), (c_dt, LANES, False, 1.0, False))
    sq, sk, sv, pz, xbc, dt_raw = _inproj_call((h1,), modsel(1), w1cat.astype(BF16), cos_t, sin_t, plan1,
                                               (BF16, BF16, BF16, F32, F32, F32), n_ctx_tiles, "inproj_cd")

    pad_l = lambda t: jnp.concatenate([t, jnp.zeros((LANES - t.shape[0],), F32)])[None, :]
    a_neg = -jnp.exp(ssd_a_log[0].astype(F32))
    conv_w_pad = jnp.concatenate([ssd_conv_w[0], jnp.zeros((8 - SSD_CONV, conv_dim), F32)], 0)
    xc, dt_sp, adt = _ssd_prep_call(xbc, dt_raw, conv_w_pad, ssd_conv_b[0][None, :], pad_l(ssd_dt_bias[0].reshape(-1)),
                                    pad_l(a_neg.reshape(-1)), n_ctx_tiles)
    adt_t = jnp.swapaxes(adt[:, :, :2 * SSD_HEADS], 1, 2)
    jj = jnp.arange(LANES, dtype=jnp.int32)[:, None]
    hh = (jnp.arange(gw, dtype=jnp.int32) // HEAD_DIM)[None, :]
    e_sel = jnp.stack([(jj == hh), (jj == hh + SSD_HEADS)]).astype(BF16)
    ys_f, ys_r = _ssd_scan_call(xc, dt_sp, adt, adt_t, e_sel, n_ctx)
    o2 = _swa_call(swa_sink[0].astype(F32) * log2e, sq, sk, sv, n_ctx)

    return _outproj_ffn_call(_ssd_mixer_out, (ys_f, ys_r, xc, pz),
                             (row(jnp.repeat(ssd_d[0], HEAD_DIM)), row(ssd_norm_g[0])), o2, (h1,),
                             modsel(1), w_out1.astype(BF16), ffn_w1[1].astype(BF16),
                             ffn_w2[1].astype(BF16), row(ln_mix_g[1]), row(ln_mix_b[1]), row(ln_ffn_g[1]),
                             row(ln_ffn_b[1]), h_off=n_ctx_tiles, n_out_tiles=n_lat_tiles,
                             n_ctx_tiles=n_ctx_tiles, alpha=alpha, name="outproj_ffn_1")
```

```python
import functools
import math

import jax
import jax.numpy as jnp
from jax import lax
from jax.experimental import pallas as pl
from jax.experimental.pallas import tpu as pltpu

F32 = jnp.float32
BF16 = jnp.bfloat16

HEAD_DIM = 64
GRID_W = 64
ROPE_BASE = 10000.0
LN_EPS = 1e-5
RWKV_GN_EPS = 64e-5
RWKV_LORA_W = 64
RWKV_LORA_A = 64
RWKV_LORA_G = 128
DIFF_HEADS = 4
SSD_HEADS = 8
SSD_GROUPS = 2
SSD_STATE = 128
SSD_CONV = 5
SWA_HEADS = 8
SWA_KV_HEADS = 2
SWA_WINDOW = 128

ROW_TILE = 256
BATCH_PACK = 2
HALO = 8
SCAN_BATCH_PACK = 4
RWKV_CHUNK = 64
SSD_CHUNK = 128
FFN_UP_AHEAD = 1
DIFF_QK_AHEAD = 3
SWA_TILE = 128
LANES = 128
VMEM_LIMIT = 56 * 1024 * 1024


def _cparams(n_axes):
    return pltpu.CompilerParams(dimension_semantics=("arbitrary",) * n_axes, vmem_limit_bytes=VMEM_LIMIT)


def _dot(a, b):
    return jnp.dot(a, b, preferred_element_type=F32)


def _dot_nt(a, b):
    return lax.dot_general(a, b, (((1,), (1,)), ((), ())), preferred_element_type=F32)


def _split_bf16(a, pieces):
    out = []
    for i in range(pieces):
        part = a.astype(BF16)
        out.append(part)
        if i + 1 < pieces:
            a = a - part.astype(F32)
    return out


def _dot_f32_by_exact(a, b_exact, pieces=3):
    acc = None
    for part in _split_bf16(a, pieces):
        t = _dot(part, b_exact)
        acc = t if acc is None else acc + t
    return acc


def _dot_exact_by_f32(a_exact, b, pieces=3):
    acc = None
    for part in _split_bf16(b, pieces):
        t = _dot(a_exact, part)
        acc = t if acc is None else acc + t
    return acc


def _dot_hilo(a, b):
    ah = a.astype(BF16)
    al = (a - ah.astype(F32)).astype(BF16)
    bh = b.astype(BF16)
    bl = (b - bh.astype(F32)).astype(BF16)
    return _dot(ah, bh) + _dot(ah, bl) + _dot(al, bh)


def _split_hilo(w):
    hi = w.astype(BF16)
    return jnp.stack([hi, (w - hi.astype(F32)).astype(BF16)])


def _dot_hilo_presplit(a, w_ref):
    ah = a.astype(BF16)
    al = (a - ah.astype(F32)).astype(BF16)
    return _dot(ah, w_ref[0]) + _dot(ah, w_ref[1]) + _dot(al, w_ref[0])


def _sigmoid(x):
    return 0.5 * jnp.tanh(0.5 * x) + 0.5


def _softplus(x):
    return jnp.maximum(x, 0.0) + jnp.log(1.0 + jnp.exp(-jnp.abs(x)))


def _layer_norm(y, g, b, eps):
    mu = jnp.mean(y, -1, keepdims=True)
    yc = y - mu
    var = jnp.mean(yc * yc, -1, keepdims=True)
    return yc * lax.rsqrt(var + eps) * g + b


def _mod_kernel(c_ref, w_ref, b_ref, o_ref):
    c = c_ref[...]
    o_ref[0] = _dot_hilo(c * _sigmoid(c), w_ref[0]) + b_ref[0]


def _mod_call(c_pad, mod_w, mod_b):
    depth, d, n = mod_w.shape
    rows = c_pad.shape[0]
    tn = 1536
    return pl.pallas_call(
        _mod_kernel,
        grid=(depth, n // tn),
        in_specs=[pl.BlockSpec((rows, d), lambda i, j: (0, 0)),
                  pl.BlockSpec((1, d, tn), lambda i, j: (i, 0, j)),
                  pl.BlockSpec((1, 1, tn), lambda i, j: (i, 0, j))],
        out_specs=pl.BlockSpec((1, rows, tn), lambda i, j: (i, 0, j)),
        out_shape=jax.ShapeDtypeStruct((depth, rows, n), F32),
        compiler_params=_cparams(2),
        name="adaln_mod",
    )(c_pad, mod_w, mod_b.reshape(depth, 1, n))


def _rope(x, cos, sin):
    lane = lax.broadcasted_iota(jnp.int32, (x.shape[0], LANES), 1)
    first = (lane % 32) < 16
    parts = []
    for g in range(0, x.shape[1], LANES):
        xg = x[:, g:g + LANES]
        parts.append(jnp.where(first, pltpu.roll(xg, LANES - 16, 1), pltpu.roll(xg, 16, 1)))
    sw = parts[0] if len(parts) == 1 else jnp.concatenate(parts, 1)
    return x * cos + sw * sin


def _stream_specs(streams, n_ctx_tiles):
    d = streams[0].shape[-1]
    bp = BATCH_PACK
    if len(streams) == 1:
        return [pl.BlockSpec((bp, ROW_TILE, d), lambda bi, t: (bi, t, 0))]
    return [pl.BlockSpec((bp, ROW_TILE, d), lambda bi, t: (bi, jnp.minimum(t, n_ctx_tiles - 1), 0)),
            pl.BlockSpec((bp, ROW_TILE, d), lambda bi, t: (bi, jnp.maximum(t - n_ctx_tiles, 0), 0))]


def _stream_tile(refs, n_ctx_tiles):
    if len(refs) == 1:
        return refs[0][...]
    return jnp.where(pl.program_id(1) < n_ctx_tiles, refs[0][...], refs[1][...])


def _inproj_kernel(*refs, plan, n_streams, n_ctx_tiles):
    x_refs = refs[:n_streams]
    sh_ref, sc_ref, w_ref, cos_ref, sin_ref = refs[n_streams:n_streams + 5]
    o_refs = refs[n_streams + 5:]
    x = _stream_tile(x_refs, n_ctx_tiles)
    bp, rows, d = x.shape
    xm = (x * (1.0 + sc_ref[:, 0, 0]) + sh_ref[:, 0, 0]).reshape(bp * rows, d).astype(BF16)
    cos = jnp.concatenate([cos_ref[...]] * bp, 0)
    sin = jnp.concatenate([sin_ref[...]] * bp, 0)
    for o_ref, (c0, width, rope, mult, transpose) in zip(o_refs, plan):
        for j in range(0, width, 256):
            cw = min(256, width - j)
            acc = _dot(xm, w_ref[:, c0 + j:c0 + j + cw])
            if rope:
                acc = _rope(acc, cos[:, :cw], sin[:, :cw])
            if mult != 1.0:
                acc = acc * mult
            if transpose:
                for p in range(bp):
                    o_ref[p, j:j + cw, :] = acc[p * rows:(p + 1) * rows].T.astype(o_ref.dtype)
            else:
                o_ref[:, :, j:j + cw] = acc.reshape(bp, rows, cw).astype(o_ref.dtype)


def _inproj_call(streams, modsel, w_bf16, cos_t, sin_t, plan, out_dtypes, n_ctx_tiles, name):
    b, _, d = streams[0].shape
    l = sum(s.shape[1] for s in streams)
    n_tiles = l // ROW_TILE
    seg = lambda bi, t: (bi, (t >= n_ctx_tiles).astype(jnp.int32), 0, 0, 0)
    seg_scale = lambda bi, t: (bi, (t >= n_ctx_tiles).astype(jnp.int32), 1, 0, 0)
    out_shape, out_specs = [], []
    bp = BATCH_PACK
    for (_, width, _, _, transpose), dt in zip(plan, out_dtypes):
        if transpose:
            out_shape.append(jax.ShapeDtypeStruct((b, width, l), dt))
            out_specs.append(pl.BlockSpec((bp, width, ROW_TILE), lambda bi, t: (bi, 0, t)))
        else:
            out_shape.append(jax.ShapeDtypeStruct((b, l, width), dt))
            out_specs.append(pl.BlockSpec((bp, ROW_TILE, width), lambda bi, t: (bi, t, 0)))
    return pl.pallas_call(
        functools.partial(_inproj_kernel, plan=plan, n_streams=len(streams), n_ctx_tiles=n_ctx_tiles),
        grid=(b // bp, n_tiles),
        in_specs=_stream_specs(streams, n_ctx_tiles) + [
            pl.BlockSpec((bp, 1, 1, 1, d), seg),
            pl.BlockSpec((bp, 1, 1, 1, d), seg_scale),
            pl.BlockSpec(w_bf16.shape, lambda bi, t: (0, 0), pipeline_mode=pl.Buffered(1)),
            pl.BlockSpec((ROW_TILE, cos_t.shape[1]), lambda bi, t: (t, 0)),
            pl.BlockSpec((ROW_TILE, sin_t.shape[1]), lambda bi, t: (t, 0))],
        out_specs=out_specs,
        out_shape=out_shape,
        compiler_params=_cparams(2),
        name=name,
    )(*streams, modsel, modsel, w_bf16, cos_t, sin_t)


def _rwkv_prep_kernel(p_ref, hp_ref, hn_ref, mu_ref, bias_ref, lora_ref, g2_ref, kk_ref, ka_ref, rk_ref, bd_ref,
                      sel_ref, selt_ref, at_o, bt_o, kt_o, rt_o, pe_o, v_o, g_o, bonus_o, ps_ref,
                      *, n_ctx_tiles, n_tiles, chunk):
    t = pl.program_id(1)
    gw = v_o.shape[-1]
    rows, cols = p_ref.shape[1:]
    first = jnp.logical_or(t == 0, t == n_ctx_tiles)
    last = jnp.logical_or(t == n_ctx_tiles - 1, t == n_tiles - 1)
    half_mu = 0.5 * mu_ref[...]
    row8 = lax.broadcasted_iota(jnp.int32, (HALO, cols), 0)
    lane = lax.broadcasted_iota(jnp.int32, (rows, LANES), 1)
    bd = bd_ref[...]
    ri = lax.broadcasted_iota(jnp.int32, (rows, rows), 0)
    ci = lax.broadcasted_iota(jnp.int32, (rows, rows), 1)
    same_chunk = (ri // chunk) == (ci // chunk)
    tris = [jnp.logical_and(same_chunk, (ci <= ri) if d == 0 else (ci >= ri)).astype(BF16)
            for d in range(2)]
    for pb in range(p_ref.shape[0]):
        p = p_ref[pb]
        prev_row = jnp.where(first, 0.0, hp_ref[pb, HALO - 1:HALO, :])
        next_row = jnp.where(last, 0.0, hn_ref[pb, 0:1, :])
        ps = ps_ref.at[pb]
        ps[...] = (1.0 - mu_ref[...]) * p + half_mu * (pltpu.roll(p, 1, 0) + pltpu.roll(p, rows - 1, 0))
        ps[0:HALO, :] += jnp.where(row8 == 0, half_mu * (prev_row - p[rows - 1:rows, :]), 0.0)
        ps[rows - HALO:, :] += jnp.where(row8 == HALO - 1, half_mu * (next_row - p[0:1, :]), 0.0)

        r = ps[:, :gw]
        k = ps[:, gw:2 * gw]
        v = ps[:, 2 * gw:3 * gw]
        slab = ps[:, 3 * gw:3 * gw + LANES]
        xg = ps[:, 3 * gw + LANES:3 * gw + 2 * LANES]
        lora_in = jnp.where(lane < RWKV_LORA_W, jnp.tanh(slab), slab)
        pre = _dot_hilo_presplit(lora_in, lora_ref) + bias_ref[...]
        g_o[pb] = _dot_hilo_presplit(_sigmoid(xg), g2_ref)
        kk = k * kk_ref[...]
        ss = _dot_f32_by_exact(kk * kk, sel_ref[...], 2)
        kk = kk * _dot_f32_by_exact(lax.rsqrt(jnp.maximum(ss, 1e-24)), selt_ref[...], 2)
        bonus_o[pb] = _dot_f32_by_exact(r * k * rk_ref[...], bd, 2) * v
        v_o[pb] = v.astype(v_o.dtype)
        for d in range(2):
            lw = -math.exp(-0.5) * _sigmoid(pre[:, d * gw:(d + 1) * gw])
            a = _sigmoid(pre[:, (2 + d) * gw:(3 + d) * gw])
            cum = _dot_exact_by_f32(tris[d], lw, 2)
            p_inv = jnp.exp(-cum)
            at_o[pb, d] = (-kk * jnp.exp(cum - lw)).astype(at_o.dtype)
            bt_o[pb, d] = (kk * a * p_inv).astype(bt_o.dtype)
            kt_o[pb, d] = (k * (1.0 + (a - 1.0) * ka_ref[...]) * p_inv).astype(kt_o.dtype)
            rt_o[pb, d] = (r * jnp.exp(cum)).astype(rt_o.dtype)
            for j in range(rows // chunk):
                last_row = j * chunk + (chunk - 1 if d == 0 else 0)
                pe_o[pb, d, j] = jnp.exp(cum[last_row:last_row + 1, :])


def _rwkv_prep_call(p, mu, bias, lora_w, g2, k_k, k_a, r_k, bd, sel, sel_t, n_ctx_tiles):
    b, l, cols = p.shape
    gw = k_k.shape[-1]
    n_tiles = l // ROW_TILE
    hb = ROW_TILE // HALO
    n_hb = l // HALO
    cpt = ROW_TILE // RWKV_CHUNK
    bp = BATCH_PACK
    full = lambda a: pl.BlockSpec(a.shape, lambda bi, t: (0,) * a.ndim)
    tok = pl.BlockSpec((bp, ROW_TILE, gw), lambda bi, t: (bi, t, 0))
    tok2 = pl.BlockSpec((bp, 2, ROW_TILE, gw), lambda bi, t: (bi, 0, t, 0))
    s1 = jax.ShapeDtypeStruct((b, l, gw), F32)
    s2 = jax.ShapeDtypeStruct((b, 2, l, gw), BF16)
    return pl.pallas_call(
        functools.partial(_rwkv_prep_kernel, n_ctx_tiles=n_ctx_tiles, n_tiles=n_tiles, chunk=RWKV_CHUNK),
        grid=(b // bp, n_tiles),
        in_specs=[pl.BlockSpec((bp, ROW_TILE, cols), lambda bi, t: (bi, t, 0)),
                  pl.BlockSpec((bp, HALO, cols), lambda bi, t: (bi, jnp.maximum(t * hb - 1, 0), 0)),
                  pl.BlockSpec((bp, HALO, cols), lambda bi, t: (bi, jnp.minimum((t + 1) * hb, n_hb - 1), 0)),
                  full(mu), full(bias), full(lora_w), full(g2), full(k_k), full(k_a), full(r_k), full(bd),
                  full(sel), full(sel_t)],
        out_specs=[tok2, tok2, tok2, tok2,
                   pl.BlockSpec((bp, 2, cpt, 1, gw), lambda bi, t: (bi, 0, t, 0, 0)), tok, tok, tok],
        out_shape=[s2, s2, s2, s2, jax.ShapeDtypeStruct((b, 2, l // RWKV_CHUNK, 1, gw), F32),
                   jax.ShapeDtypeStruct((b, l, gw), BF16), s1, s1],
        scratch_shapes=[pltpu.VMEM((bp, ROW_TILE, cols), F32)],
        compiler_params=_cparams(2),
        name="rwkv_prep",
    )(p, p, p, mu, bias, lora_w, g2, k_k, k_a, r_k, bd, sel, sel_t)


def _rwkv_scan_kernel(atf_ref, btf_ref, ktf_ref, rtf_ref, pef_ref, vf_ref, atr_ref, btr_ref, ktr_ref, rtr_ref, per_ref,
                      vr_ref, yf_ref, yr_ref, s_ref):
    i = pl.program_id(1)
    c = vf_ref.shape[1]
    n_pairs = vf_ref.shape[2] // LANES
    c2 = 2 * c

    @pl.when(i == 0)
    def _():
        s_ref[...] = jnp.zeros_like(s_ref)

    row2 = lax.broadcasted_iota(jnp.int32, (c2, c2), 0)
    col2 = lax.broadcasted_iota(jnp.int32, (c2, c2), 1)
    same = (row2 // c) == (col2 // c)
    eye = (row2 == col2).astype(F32)
    low = lax.broadcasted_iota(jnp.int32, (c, LANES), 1) < HEAD_DIM

    def stack(x):
        xf = x.astype(F32)
        return jnp.concatenate([jnp.where(low, xf, 0.0), jnp.where(low, 0.0, xf)], 0)

    chains = []
    dirs = ((1, atf_ref, btf_ref, ktf_ref, rtf_ref, pef_ref, vf_ref, yf_ref),
            (-1, atr_ref, btr_ref, ktr_ref, rtr_ref, per_ref, vr_ref, yr_ref))
    for d, (sgn, at_ref, bt_ref, kt_ref, rt_ref, pe_ref, v_ref, y_ref) in enumerate(dirs):
        dt2 = (row2 - col2) * sgn
        strict = jnp.logical_and(same, dt2 > 0)
        incl = jnp.logical_and(same, dt2 >= 0)
        for p in range(vf_ref.shape[0]):
            for pr in range(n_pairs):
                sl = slice(pr * LANES, (pr + 1) * LANES)
                b_s = bt_ref[p, 0, :, sl]
                k_s = kt_ref[p, 0, :, sl]
                chains.append(dict(st=(p, d, pr), p=p, sl=sl, y_ref=y_ref, strict=strict, incl=incl,
                                   p_end=pe_ref[p, 0, 0, :, sl], vh=stack(v_ref[p, :, sl]),
                                   ar=jnp.concatenate([stack(at_ref[p, 0, :, sl]), stack(rt_ref[p, 0, :, sl])],
                                                      0).astype(BF16),
                                   bk2=jnp.concatenate([b_s, k_s], 0),
                                   bk=jnp.concatenate([b_s, b_s, k_s, k_s], 0)))

    low2 = lax.broadcasted_iota(jnp.int32, (c2, c2), 1) < c
    for ch in chains:
        ch["s0"] = s_ref[ch["st"]]
        prod = _dot_nt(ch["ar"], jnp.concatenate([ch["bk2"], ch["s0"].astype(BF16)], 0))
        ch["ars"] = prod[:, c2:]
        a_bk = prod[:c2, :c2]
        r_bk = prod[c2:, :c2]
        a_kb = pltpu.roll(a_bk, c, 1)
        r_kb = pltpu.roll(r_bk, c, 1)
        l_ab = jnp.where(ch["strict"], jnp.where(low2, a_bk, a_kb), 0.0)
        ch["l_ak"] = jnp.where(ch["strict"], jnp.where(low2, a_kb, a_bk), 0.0).astype(BF16)
        ch["l_r"] = jnp.concatenate([jnp.where(ch["incl"], jnp.where(low2, r_bk, r_kb), 0.0),
                                     jnp.where(ch["incl"], jnp.where(low2, r_kb, r_bk), 0.0)], 1).astype(BF16)
        ch["inv"] = eye + l_ab
        ch["pw"] = l_ab.astype(BF16)
    for ch in chains:
        ch["pw"] = _dot(ch["pw"], ch["pw"]).astype(BF16)
    for _ in range(max(0, int(math.ceil(math.log2(c))) - 2)):
        for ch in chains:
            both = _dot(ch["pw"], jnp.concatenate([ch["inv"].astype(BF16), ch["pw"]], 1))
            ch["inv"] = ch["inv"] + both[:, :c2]
            ch["pw"] = both[:, c2:].astype(BF16)
    for ch in chains:
        ch["inv"] = (ch["inv"] + _dot(ch["pw"], ch["inv"].astype(BF16))).astype(BF16)
    for ch in chains:
        ch["lv"] = _dot(ch["l_ak"], ch["vh"].astype(BF16))
    for ch in chains:
        u = _dot(ch["inv"], (ch["ars"][:c2] + ch["lv"]).astype(BF16))
        ch["uv"] = jnp.concatenate([u, ch["vh"]], 0)
    for ch in chains:
        y = ch["ars"][c2:] + _dot(ch["l_r"], ch["uv"].astype(BF16))
        ch["y_ref"][ch["p"], :, ch["sl"]] = y[:c] + y[c:]
    for ch in chains:
        upd = (ch["s0"] + _dot(ch["uv"].T.astype(BF16), ch["bk"])) * ch["p_end"]
        s_ref[ch["st"]] = jnp.where(same, upd, 0.0)


def _rwkv_scan_call(at, bt, kt, rt, pe, v, n_ctx):
    b, l, gw = v.shape
    c = RWKV_CHUNK
    nc = l // c
    ncc = n_ctx // c

    def rev(i):
        return jnp.where(i < ncc, ncc - 1 - i, nc - 1 - (i - ncc))

    bp = SCAN_BATCH_PACK if b % SCAN_BATCH_PACK == 0 else BATCH_PACK

    def specs(d, ch):
        tokd = pl.BlockSpec((bp, 1, c, gw), lambda bi, i: (bi, d, ch(i), 0))
        return [tokd, tokd, tokd, tokd, pl.BlockSpec((bp, 1, 1, 1, gw), lambda bi, i: (bi, d, ch(i), 0, 0)),
                pl.BlockSpec((bp, c, gw), lambda bi, i: (bi, ch(i), 0))]

    ys = jax.ShapeDtypeStruct((b, l, gw), F32)
    return pl.pallas_call(
        _rwkv_scan_kernel,
        grid=(b // bp, nc),
        in_specs=specs(0, lambda i: i) + specs(1, rev),
        out_specs=[pl.BlockSpec((bp, c, gw), lambda bi, i: (bi, i, 0)),
                   pl.BlockSpec((bp, c, gw), lambda bi, i: (bi, rev(i), 0))],
        out_shape=[ys, ys],
        scratch_shapes=[pltpu.VMEM((bp, 2, gw // LANES, LANES, LANES), F32)],
        compiler_params=_cparams(2),
        name="rwkv_scan",
    )(at, bt, kt, rt, pe, v, at, bt, kt, rt, pe, v)


def _rwkv_mixer_out(yf_ref, yr_ref, bonus_ref, g_ref, gg_ref, gb_ref):
    flat = lambda r: r[...].reshape(-1, r.shape[-1])
    y = flat(yf_ref) + flat(yr_ref)
    low = lax.broadcasted_iota(jnp.int32, (y.shape[0], LANES), 1) < HEAD_DIM

    def head_mean(t):
        tot = jnp.sum(t, -1, keepdims=True)
        lo = jnp.sum(jnp.where(low, t, 0.0), -1, keepdims=True)
        return jnp.where(low, lo, tot - lo) * (1.0 / HEAD_DIM)

    outs = []
    for s in range(0, y.shape[1], LANES):
        ys = y[:, s:s + LANES]
        yc = ys - head_mean(ys)
        outs.append(yc * lax.rsqrt(head_mean(yc * yc) + RWKV_GN_EPS))
    yn = jnp.concatenate(outs, 1) * gg_ref[...] + gb_ref[...]
    return (yn + flat(bonus_ref)) * flat(g_ref)


def _diff_attn_kernel(lam_ref, q_ref, k_ref, vt_ref, g_ref, o_ref, *, n_ctx, n_ctx_tiles, out_scale):
    t = pl.program_id(1)
    lam = lam_ref[0]
    vd = 2 * HEAD_DIM
    tq = q_ref.shape[1]
    low = lax.broadcasted_iota(jnp.int32, (tq, vd), 1) < HEAD_DIM

    def run(nk):
        def logits(u):
            p, hm = divmod(u, 2 * DIFF_HEADS)
            h, m = divmod(hm, 2)
            qf = q_ref[p, :, h * vd:(h + 1) * vd].astype(F32)
            qm = jnp.where(low == (m == 0), qf, 0.0).astype(BF16)
            return _dot_nt(k_ref[p, :nk, h * vd:(h + 1) * vd], qm)

        n_units = 2 * DIFF_HEADS * q_ref.shape[0]
        ahead = DIFF_QK_AHEAD
        pending = [logits(u) for u in range(min(ahead, n_units))]
        es, invs = [], []
        for u in range(n_units):
            s = pending.pop(0)
            if u + ahead < n_units:
                pending.append(logits(u + ahead))
            e = jnp.exp2(s - jnp.max(s, 0, keepdims=True))
            invs.append(1.0 / jnp.sum(e, 0, keepdims=True))
            es.append(e.astype(BF16))
            if u % 2 == 1:
                p, hm = divmod(u, 2 * DIFF_HEADS)
                h = hm // 2
                pv = _dot(vt_ref[p, h * vd:(h + 1) * vd, :nk], jnp.concatenate(es, 1))
                o = pv[:, :tq] * invs[0] - pv[:, tq:] * (lam * invs[1])
                o = o * lax.rsqrt(jnp.mean(o * o, 0, keepdims=True) + LN_EPS) * (g_ref[...] * out_scale)
                o_ref[p, :, h * vd:(h + 1) * vd] = o.T
                es, invs = [], []

    @pl.when(t < n_ctx_tiles)
    def _():
        run(n_ctx)

    @pl.when(t >= n_ctx_tiles)
    def _():
        run(k_ref.shape[1])


def _diff_attn_call(lam, q, k, vt, subln_g_col, n_ctx, lam_init):
    b, l, w = q.shape
    bp = BATCH_PACK
    tok = pl.BlockSpec((bp, ROW_TILE, w), lambda bi, t: (bi, t, 0))
    return pl.pallas_call(
        functools.partial(_diff_attn_kernel, n_ctx=n_ctx, n_ctx_tiles=n_ctx // ROW_TILE, out_scale=1.0 - lam_init),
        grid=(b // bp, l // ROW_TILE),
        in_specs=[pl.BlockSpec(memory_space=pltpu.SMEM), tok,
                  pl.BlockSpec((bp, l, w), lambda bi, t: (bi, 0, 0)),
                  pl.BlockSpec((bp, w, l), lambda bi, t: (bi, 0, 0)),
                  pl.BlockSpec(subln_g_col.shape, lambda bi, t: (0, 0))],
        out_specs=tok,
        out_shape=jax.ShapeDtypeStruct((b, l, w), F32),
        compiler_params=_cparams(2),
        name="diff_attn",
    )(lam, q, k, vt, subln_g_col)


def _outproj_ffn_kernel(*refs, alpha, n_streams, n_ctx_tiles, mixer_out, n_mix):
    o2_ref = refs[n_mix]
    h_refs = refs[n_mix + 1:n_mix + 1 + n_streams]
    (gm_ref, sh_ref, sc_ref, gf_ref, wo_ref, w1_ref, w2_ref, lmg_ref, lmb_ref, lfg_ref, lfb_ref,
     out_ref) = refs[n_mix + 1 + n_streams:]
    bp, rows, gw = o2_ref.shape
    flat = lambda t: t.reshape(bp * rows, t.shape[-1])
    per_b = lambda t: t.reshape(bp, rows, t.shape[-1])
    o1 = mixer_out(*refs[:n_mix])
    o = _dot(o1.astype(BF16), wo_ref[:gw, :]) + _dot(flat(o2_ref[...]).astype(BF16), wo_ref[gw:, :])
    h_in = _stream_tile(h_refs, n_ctx_tiles)
    h1 = _layer_norm(flat(alpha * h_in + gm_ref[:, 0, 0] * per_b(o)), lmg_ref[...], lmb_ref[...], LN_EPS)
    xm = flat(per_b(h1) * (1.0 + sc_ref[:, 0, 0]) + sh_ref[:, 0, 0]).astype(BF16)
    hidden = w1_ref.shape[1]
    step = 512

    def up(j):
        hj = jnp.maximum(_dot(xm, w1_ref[:, j:j + step]), 0.0)
        return (hj * hj).astype(BF16)

    acc = None
    offs = list(range(0, hidden, step))
    acts = [up(j) for j in offs[:FFN_UP_AHEAD]]
    for i, j in enumerate(offs):
        if i + FFN_UP_AHEAD < len(offs):
            acts.append(up(offs[i + FFN_UP_AHEAD]))
        part = _dot(acts.pop(0), w2_ref[j:j + step, :])
        acc = part if acc is None else acc + part
    out = _layer_norm(alpha * h1 + flat(gf_ref[:, 0, 0] * per_b(acc)), lfg_ref[...], lfb_ref[...], LN_EPS)
    out_ref[...] = per_b(out)


def _outproj_ffn_call(mixer_out, mix_tok, mix_par, o2, streams, modsel, wo, w1, w2, lmg, lmb, lfg, lfb, *, h_off,
                      n_out_tiles, n_ctx_tiles, alpha, name):
    b, _, gw = o2.shape
    d = streams[0].shape[-1]
    bp = BATCH_PACK
    assert len(streams) == 1 or h_off == 0
    seg = lambda col: (lambda bi, t: (bi, ((t + h_off) >= n_ctx_tiles).astype(jnp.int32), col, 0, 0))
    modspec = lambda col: pl.BlockSpec((bp, 1, 1, 1, d), seg(col))
    full = lambda a: pl.BlockSpec(a.shape, lambda bi, t: (0,) * a.ndim, pipeline_mode=pl.Buffered(1))
    mtok = pl.BlockSpec((bp, ROW_TILE, gw), lambda bi, t: (bi, t + h_off, 0))
    otok = pl.BlockSpec((bp, ROW_TILE, gw), lambda bi, t: (bi, t, 0))
    if len(streams) == 1:
        h_specs = [pl.BlockSpec((bp, ROW_TILE, d), lambda bi, t: (bi, t + h_off, 0))]
    else:
        h_specs = _stream_specs(streams, n_ctx_tiles)
    n_mix = len(mix_tok) + len(mix_par)
    return pl.pallas_call(
        functools.partial(_outproj_ffn_kernel, alpha=alpha, n_streams=len(streams), n_ctx_tiles=n_ctx_tiles,
                          mixer_out=mixer_out, n_mix=n_mix),
        grid=(b // bp, n_out_tiles),
        in_specs=[mtok] * len(mix_tok) + [full(p) for p in mix_par] + [otok] + h_specs + [
            modspec(2), modspec(3), modspec(4), modspec(5),
            full(wo), full(w1), full(w2), full(lmg), full(lmb), full(lfg), full(lfb)],
        out_specs=pl.BlockSpec((bp, ROW_TILE, d), lambda bi, t: (bi, t, 0)),
        out_shape=jax.ShapeDtypeStruct((b, n_out_tiles * ROW_TILE, d), F32),
        compiler_params=_cparams(2),
        name=name,
    )(*mix_tok, *mix_par, o2, *streams, modsel, modsel, modsel, modsel, wo, w1, w2, lmg, lmb, lfg, lfb)


def _ssd_prep_kernel(x_ref, hp_ref, hn_ref, dt_ref, cw_ref, cb_ref, dtb_ref, an_ref, xc_o, dt_o, adt_o,
                     *, n_ctx_tiles, n_tiles):
    t = pl.program_id(1)
    x = x_ref[0]
    rows = x.shape[0]
    first = jnp.logical_or(t == 0, t == n_ctx_tiles)
    last = jnp.logical_or(t == n_ctx_tiles - 1, t == n_tiles - 1)
    ext = jnp.concatenate([jnp.where(first, 0.0, hp_ref[0]), x, jnp.where(last, 0.0, hn_ref[0])], 0)
    n_ext = rows + 2 * HALO
    acc = cb_ref[...] + cw_ref[SSD_CONV // 2:SSD_CONV // 2 + 1, :] * x
    for j in range(SSD_CONV):
        s = j - SSD_CONV // 2
        if s == 0:
            continue
        shifted = pltpu.roll(ext, (n_ext - s) % n_ext, 0)[HALO:HALO + rows]
        acc = acc + cw_ref[j:j + 1, :] * shifted
    xc_o[0] = acc * _sigmoid(acc)
    dt = _softplus(dt_ref[0] + dtb_ref[...])
    dt_o[0] = dt
    adt_o[0] = dt * an_ref[...]


def _ssd_prep_call(xbc, dt, conv_w, conv_b, dt_bias, a_neg, n_ctx_tiles):
    b, l, cw = xbc.shape
    n_tiles = l // ROW_TILE
    hb = ROW_TILE // HALO
    n_hb = l // HALO
    full = lambda a: pl.BlockSpec(a.shape, lambda bi, t: (0,) * a.ndim)
    tokd = pl.BlockSpec((1, ROW_TILE, LANES), lambda bi, t: (bi, t, 0))
    sd = jax.ShapeDtypeStruct((b, l, LANES), F32)
    return pl.pallas_call(
        functools.partial(_ssd_prep_kernel, n_ctx_tiles=n_ctx_tiles, n_tiles=n_tiles),
        grid=(b, n_tiles),
        in_specs=[pl.BlockSpec((1, ROW_TILE, cw), lambda bi, t: (bi, t, 0)),
                  pl.BlockSpec((1, HALO, cw), lambda bi, t: (bi, jnp.maximum(t * hb - 1, 0), 0)),
                  pl.BlockSpec((1, HALO, cw), lambda bi, t: (bi, jnp.minimum((t + 1) * hb, n_hb - 1), 0)),
                  tokd, full(conv_w), full(conv_b), full(dt_bias), full(a_neg)],
        out_specs=[pl.BlockSpec((1, ROW_TILE, cw), lambda bi, t: (bi, t, 0)), tokd, tokd],
        out_shape=[jax.ShapeDtypeStruct((b, l, cw), F32), sd, sd],
        compiler_params=_cparams(2),
        name="ssd_prep",
    )(xbc, xbc, xbc, dt, conv_w, conv_b, dt_bias, a_neg)


def _ssd_scan_kernel(xcf_ref, dtf_ref, adtf_ref, adttf_ref, xcr_ref, dtr_ref, adtr_ref, adttr_ref, e_ref,
                     yf_ref, yr_ref, st_ref):
    i = pl.program_id(1)
    q = xcf_ref.shape[1]
    gw = yf_ref.shape[-1]
    n = SSD_STATE
    hpg = SSD_HEADS // SSD_GROUPS
    gcols = hpg * HEAD_DIM

    @pl.when(i == 0)
    def _():
        st_ref[...] = jnp.zeros_like(st_ref)

    row = lax.broadcasted_iota(jnp.int32, (q, q), 0)
    col = lax.broadcasted_iota(jnp.int32, (q, q), 1)
    chains = []
    dirs = ((1, xcf_ref, dtf_ref, adtf_ref, adttf_ref, yf_ref), (-1, xcr_ref, dtr_ref, adtr_ref, adttr_ref, yr_ref))
    for p, d in [(p, d) for p in range(xcf_ref.shape[0]) for d in range(2)]:
        sgn, xc_ref, dt_ref, adt_ref, adtt_ref, y_ref = dirs[d]
        incl = (row - col) * sgn >= 0
        tri = incl.astype(BF16)
        tri_t = ((col - row) * sgn >= 0).astype(BF16)
        cs = _dot_exact_by_f32(tri, adt_ref[p])
        cs_t = _dot_f32_by_exact(adtt_ref[p], tri_t)
        e = e_ref[d]
        cse = _dot_f32_by_exact(cs, e)
        dte = _dot_f32_by_exact(dt_ref[p], e, 2)
        tot = cse[q - 1:q, :] if sgn > 0 else cse[0:1, :]
        xc = xc_ref[p]
        xdt = xc[:, :gw] * dte
        e_cs = jnp.exp(cse)
        x_end = (xdt * jnp.exp(tot - cse)).astype(BF16)
        e_tot = jnp.exp(tot)
        xdt16 = xdt.astype(BF16)
        for g in range(SSD_GROUPS):
            gsl = slice(g * gcols, (g + 1) * gcols)
            lms = []
            for hh in range(hpg):
                h = g * hpg + hh
                colv = cse[:, h * HEAD_DIM:h * HEAD_DIM + 1]
                rowv = cs_t[d * SSD_HEADS + h:d * SSD_HEADS + h + 1, :]
                lms.append(jnp.where(incl, jnp.exp(colv - rowv), 0.0))
            chains.append(dict(st=(p, d, g), p=p, gsl=gsl, y_ref=y_ref, lms=lms, e_cs=e_cs[:, gsl], e_tot=e_tot[:, gsl],
                               x_end=x_end[:, gsl], xdt16=xdt16[:, gsl],
                               bg=xc[:, gw + g * n:gw + (g + 1) * n],
                               cg=xc[:, gw + (SSD_GROUPS + g) * n:gw + (SSD_GROUPS + g + 1) * n].astype(BF16)))
    for ch in chains:
        ch["cb"] = _dot_nt(ch["cg"], ch["bg"].astype(BF16))
        ch["s0"] = st_ref[ch["st"]]
    for ch in chains:
        ch["y_off"] = _dot(ch["cg"], ch["s0"].astype(BF16)) * ch["e_cs"]
    for ch in chains:
        ch["ys"] = [_dot((ch["cb"] * ch["lms"][hh]).astype(BF16), ch["xdt16"][:, hh * HEAD_DIM:(hh + 1) * HEAD_DIM])
                    for hh in range(hpg)]
    for ch in chains:
        ch["y_ref"][ch["p"], :, ch["gsl"]] = jnp.concatenate(ch["ys"], 1) + ch["y_off"]
        st_ref[ch["st"]] = ch["s0"] * ch["e_tot"] + _dot(ch["bg"].T.astype(BF16), ch["x_end"])


def _ssd_scan_call(xc, dt, adt, adt_t, e_sel, n_ctx):
    b, l, cw = xc.shape
    gw = e_sel.shape[-1]
    q = SSD_CHUNK
    nc = l // q
    ncc = n_ctx // q

    def rev(i):
        return jnp.where(i < ncc, ncc - 1 - i, nc - 1 - (i - ncc))

    bp = SCAN_BATCH_PACK if b % SCAN_BATCH_PACK == 0 else BATCH_PACK

    def specs(ch):
        tokd = pl.BlockSpec((bp, q, LANES), lambda bi, i: (bi, ch(i), 0))
        return [pl.BlockSpec((bp, q, cw), lambda bi, i: (bi, ch(i), 0)), tokd, tokd,
                pl.BlockSpec((bp, 2 * SSD_HEADS, q), lambda bi, i: (bi, 0, ch(i)))]

    ys = jax.ShapeDtypeStruct((b, l, gw), F32)
    return pl.pallas_call(
        _ssd_scan_kernel,
        grid=(b // bp, nc),
        in_specs=specs(lambda i: i) + specs(rev) + [pl.BlockSpec(e_sel.shape, lambda bi, i: (0, 0, 0))],
        out_specs=[pl.BlockSpec((bp, q, gw), lambda bi, i: (bi, i, 0)),
                   pl.BlockSpec((bp, q, gw), lambda bi, i: (bi, rev(i), 0))],
        out_shape=[ys, ys],
        scratch_shapes=[pltpu.VMEM((bp, 2, SSD_GROUPS, SSD_STATE, gw // SSD_GROUPS), F32)],
        compiler_params=_cparams(2),
        name="ssd_scan",
    )(xc, dt, adt, adt_t, xc, dt, adt, adt_t, e_sel)


def _ssd_mixer_out(yf_ref, yr_ref, xs_ref, z_ref, dsk_ref, ng_ref):
    flat = lambda r: r[...].reshape(-1, r.shape[-1])
    z = flat(z_ref)
    y = (flat(yf_ref) + flat(yr_ref) + dsk_ref[...] * flat(xs_ref)) * (z * _sigmoid(z))
    gwid = y.shape[1] // SSD_GROUPS
    outs = []
    for g in range(SSD_GROUPS):
        yg = y[:, g * gwid:(g + 1) * gwid]
        outs.append(yg * lax.rsqrt(jnp.mean(yg * yg, -1, keepdims=True) + LN_EPS))
    return jnp.concatenate(outs, 1) * ng_ref[...]


def _swa_kernel(sink_ref, q_ref, k_ref, v_ref, o_ref, *, n_ctx, n_lat):
    i = pl.program_id(1)
    tq = q_ref.shape[1]
    span = tq + 2 * SWA_WINDOW
    nk = n_ctx + span
    group = SWA_HEADS // SWA_KV_HEADS
    start = jnp.clip(i * tq - SWA_WINDOW, 0, n_lat - span)
    w0 = pl.multiple_of(n_ctx + start, LANES)
    kpos = start + lax.broadcasted_iota(jnp.int32, (span, tq), 0)
    qpos = i * tq + lax.broadcasted_iota(jnp.int32, (span, tq), 1)
    bias = jnp.where(jnp.abs(qpos - kpos) <= SWA_WINDOW, 0.0, -1e30)
    bias = jnp.concatenate([jnp.zeros((n_ctx, tq), F32), bias], 0)
    bias = jnp.concatenate([bias] * group, 1)
    k_low = lax.broadcasted_iota(jnp.int32, (nk, LANES), 1) < HEAD_DIM
    v_low = lax.broadcasted_iota(jnp.int32, (LANES, nk), 0) < HEAD_DIM
    sinks = [jnp.concatenate([jnp.full((1, tq), sink_ref[kvh * group + g], F32) for g in range(group)], 1)
             for kvh in range(SWA_KV_HEADS)]
    bps = range(q_ref.shape[0])
    vts, ss, es, rdens = [], {}, {}, {}
    for p in bps:
        kcat = jnp.concatenate([k_ref[p, :n_ctx, :], k_ref[p, pl.ds(w0, span), :]], 0).astype(F32)
        vts.append(jnp.concatenate([v_ref[p, :n_ctx, :], v_ref[p, pl.ds(w0, span), :]], 0).astype(F32).T)
        qs = jnp.concatenate([q_ref[p, :, g * LANES:(g + 1) * LANES] for g in range(group)], 0)
        for kvh in range(SWA_KV_HEADS):
            km = jnp.where(k_low == (kvh == 0), kcat, 0.0).astype(BF16)
            ss[p, kvh] = _dot_nt(km, qs) + bias
    for p in bps:
        for kvh in range(SWA_KV_HEADS):
            m = jnp.maximum(jnp.max(ss[p, kvh], 0, keepdims=True), sinks[kvh])
            e = jnp.exp2(ss[p, kvh] - m)
            rdens[p, kvh] = 1.0 / (jnp.exp2(sinks[kvh] - m) + jnp.sum(e, 0, keepdims=True))
            es[p, kvh] = e.astype(BF16)
    for p in bps:
        ot = (_dot(jnp.where(v_low, vts[p], 0.0).astype(BF16), es[p, 0]) * rdens[p, 0]
              + _dot(jnp.where(v_low, 0.0, vts[p]).astype(BF16), es[p, 1]) * rdens[p, 1])
        for g in range(group):
            o_ref[p, :, g * LANES:(g + 1) * LANES] = ot[:, g * tq:(g + 1) * tq].T


def _swa_call(sink, q, k, v, n_ctx):
    b, l, qw = q.shape
    kw = k.shape[-1]
    n_lat = l - n_ctx
    off = n_ctx // SWA_TILE
    bp = SCAN_BATCH_PACK if b % SCAN_BATCH_PACK == 0 else BATCH_PACK
    kv = pl.BlockSpec((bp, l, kw), lambda bi, t: (bi, 0, 0))
    return pl.pallas_call(
        functools.partial(_swa_kernel, n_ctx=n_ctx, n_lat=n_lat),
        grid=(b // bp, n_lat // SWA_TILE),
        in_specs=[pl.BlockSpec(memory_space=pltpu.SMEM),
                  pl.BlockSpec((bp, SWA_TILE, qw), lambda bi, t: (bi, t + off, 0)), kv, kv],
        out_specs=pl.BlockSpec((bp, SWA_TILE, qw), lambda bi, t: (bi, t, 0)),
        out_shape=jax.ShapeDtypeStruct((b, n_lat, qw), F32),
        compiler_params=_cparams(2),
        name="swa",
    )(sink, q, k, v)


def _rope_tables(n_ctx, n_lat, width):
    half = HEAD_DIM // 4
    inv = ROPE_BASE ** (-jnp.arange(half, dtype=F32) / half)
    pos = jnp.arange(n_lat, dtype=jnp.int32)
    rows = (pos // GRID_W).astype(F32)[:, None] * inv
    cols = (pos % GRID_W).astype(F32)[:, None] * inv
    cos = jnp.concatenate([jnp.cos(rows), jnp.cos(rows), jnp.cos(cols), jnp.cos(cols)], -1)
    sin = jnp.concatenate([-jnp.sin(rows), jnp.sin(rows), -jnp.sin(cols), jnp.sin(cols)], -1)
    cos = jnp.concatenate([jnp.ones((n_ctx, HEAD_DIM), F32), cos], 0)
    sin = jnp.concatenate([jnp.zeros((n_ctx, HEAD_DIM), F32), sin], 0)
    reps = width // HEAD_DIM
    return jnp.tile(cos, (1, reps)), jnp.tile(sin, (1, reps))


def kernel(x, c, ctx, c_ctx, mod_w, mod_b, ln_mix_g, ln_mix_b, ln_ffn_g, ln_ffn_b, ffn_w1, ffn_w2, w_out, ab_w_in, rwkv_mu, rwkv_w0, rwkv_w2, rwkv_a0, rwkv_a2, rwkv_g2, rwkv_k_k, rwkv_k_a, rwkv_r_k, rwkv_gn_g, rwkv_gn_b, diff_lq1, diff_lk1, diff_lq2, diff_lk2, diff_subln_g, cd_w_in, ssd_conv_w, ssd_conv_b, ssd_dt_bias, ssd_a_log, ssd_d, ssd_norm_g, swa_sink):
    b, n_lat, d = x.shape
    n_ctx = ctx.shape[1]
    depth = mod_w.shape[0]
    assert depth == 2 and n_ctx % ROW_TILE == 0 and n_lat % ROW_TILE == 0 and n_lat % GRID_W == 0
    assert n_lat >= SWA_TILE + 2 * SWA_WINDOW and b % BATCH_PACK == 0
    gw = d // 2
    n_ctx_tiles = n_ctx // ROW_TILE
    n_lat_tiles = n_lat // ROW_TILE
    alpha = (2 * depth) ** 0.25
    log2e = math.log2(math.e)
    q_scale = HEAD_DIM ** -0.5 * log2e

    rows_pad = -(-(b + 1) // 8) * 8
    c_pad = jnp.zeros((rows_pad, d), F32).at[:b].set(c).at[b].set(c_ctx)
    m = _mod_call(c_pad, mod_w, mod_b)

    def modsel(i):
        return jnp.stack([jnp.broadcast_to(m[i, b], (b, 6 * d)), m[i, :b]], 1).reshape(b, 2, 6, 1, d)

    cos_t, sin_t = _rope_tables(n_ctx, n_lat, 256)
    h0 = (ctx, x)

    rwkv_cols = 3 * gw + RWKV_LORA_W + RWKV_LORA_A + RWKV_LORA_G
    plan0 = ((0, rwkv_cols, False, 1.0, False), (rwkv_cols, gw, True, q_scale, False),
             (rwkv_cols + gw, gw, True, 1.0, False), (rwkv_cols + 2 * gw, gw, False, 1.0, True))
    p_rwkv, dq, dk, dvt = _inproj_call(h0, modsel(0), ab_w_in[0].astype(BF16), cos_t, sin_t, plan0,
                                       (F32, BF16, BF16, BF16), n_ctx_tiles, "inproj_ab")

    zw = jnp.zeros((RWKV_LORA_W, gw), F32)
    lora_w = jnp.concatenate([jnp.concatenate([rwkv_w2[0, 0], rwkv_w2[0, 1], zw, zw], 1),
                              jnp.concatenate([zw, zw, rwkv_a2[0, 0], rwkv_a2[0, 1]], 1)], 0)
    lora_b = jnp.concatenate([rwkv_w0[0, 0], rwkv_w0[0, 1], rwkv_a0[0, 0], rwkv_a0[0, 1]])[None, :]
    head_id = jnp.arange(gw, dtype=jnp.int32) // HEAD_DIM
    bd = (head_id[:, None] == head_id[None, :]).astype(BF16)
    sel = (head_id[:, None] == jnp.arange(LANES, dtype=jnp.int32)[None, :]).astype(BF16)
    at, bt, kt, rt, pe, rv, g, bonus = _rwkv_prep_call(p_rwkv, rwkv_mu[0][None, :], lora_b, _split_hilo(lora_w), _split_hilo(rwkv_g2[0]),
                                                       rwkv_k_k[0][None, :], rwkv_k_a[0][None, :],
                                                       rwkv_r_k[0].reshape(1, gw), bd, sel, sel.T, n_ctx_tiles)
    y_f, y_r = _rwkv_scan_call(at, bt, kt, rt, pe, rv, n_ctx)

    lam_init0 = 0.8 - 0.6 * math.exp(-0.3 * 0)
    lam = (jnp.exp(jnp.sum(diff_lq1[0] * diff_lk1[0])) - jnp.exp(jnp.sum(diff_lq2[0] * diff_lk2[0]))
           + lam_init0).reshape(1).astype(F32)
    o2 = _diff_attn_call(lam, dq, dk, dvt, diff_subln_g[0][:, None], n_ctx, lam_init0)

    row = lambda t: t[None, :]
    h1 = _outproj_ffn_call(_rwkv_mixer_out, (y_f, y_r, bonus, g), (row(rwkv_gn_g[0]), row(rwkv_gn_b[0])), o2, h0,
                           modsel(0), w_out[0].astype(BF16), ffn_w1[0].astype(BF16),
                           ffn_w2[0].astype(BF16), row(ln_mix_g[0]), row(ln_mix_b[0]), row(ln_ffn_g[0]),
                           row(ln_ffn_b[0]), h_off=0, n_out_tiles=n_ctx_tiles + n_lat_tiles,
                           n_ctx_tiles=n_ctx_tiles, alpha=alpha, name="outproj_ffn_0")

    conv_dim = gw + 2 * SSD_GROUPS * SSD_STATE
    ssd_cols = gw + conv_dim + 2 * SSD_HEADS
    kvw = SWA_KV_HEADS * HEAD_DIM
    wcd = cd_w_in[0]
    group = SWA_HEADS // SWA_KV_HEADS
    q_perm = jnp.arange(gw, dtype=jnp.int32).reshape(SWA_KV_HEADS, group, HEAD_DIM).transpose(1, 0, 2).reshape(-1)
    w1cat = jnp.concatenate([wcd[:, ssd_cols:ssd_cols + gw][:, q_perm], wcd[:, ssd_cols + gw:], wcd[:, :gw],
                             wcd[:, gw:gw + conv_dim], wcd[:, gw + conv_dim:ssd_cols],
                             jnp.zeros((d, LANES - 2 * SSD_HEADS), F32)], 1)
    w_out1 = jnp.concatenate([w_out[1, :gw], w_out[1, gw:][q_perm]], 0)
    c_q, c_k, c_v, c_z, c_x, c_dt = 0, gw, gw + kvw, gw + 2 * kvw, 2 * gw + 2 * kvw, 2 * gw + 2 * kvw + conv_dim
    plan1 = ((c_q, gw, True, q_scale, False), (c_k, kvw, True, 1.0, False), (c_v, kvw, False, 1.0, False),
             (c_z, gw, False, 1.0, False), (c_x, conv_dim, False, 1.0, False), (c_dt, LANES, False, 1.0, False))
    sq, sk, sv, pz, xbc, dt_raw = _inproj_call((h1,), modsel(1), w1cat.astype(BF16), cos_t, sin_t, plan1,
                                               (BF16, BF16, BF16, F32, F32, F32), n_ctx_tiles, "inproj_cd")

    pad_l = lambda t: jnp.concatenate([t, jnp.zeros((LANES - t.shape[0],), F32)])[None, :]
    a_neg = -jnp.exp(ssd_a_log[0].astype(F32))
    conv_w_pad = jnp.concatenate([ssd_conv_w[0], jnp.zeros((8 - SSD_CONV, conv_dim), F32)], 0)
    xc, dt_sp, adt = _ssd_prep_call(xbc, dt_raw, conv_w_pad, ssd_conv_b[0][None, :], pad_l(ssd_dt_bias[0].reshape(-1)),
                                    pad_l(a_neg.reshape(-1)), n_ctx_tiles)
    adt_t = jnp.swapaxes(adt[:, :, :2 * SSD_HEADS], 1, 2)
    jj = jnp.arange(LANES, dtype=jnp.int32)[:, None]
    hh = (jnp.arange(gw, dtype=jnp.int32) // HEAD_DIM)[None, :]
    e_sel = jnp.stack([(jj == hh), (jj == hh + SSD_HEADS)]).astype(BF16)
    ys_f, ys_r = _ssd_scan_call(xc, dt_sp, adt, adt_t, e_sel, n_ctx)
    o2 = _swa_call(swa_sink[0].astype(F32) * log2e, sq, sk, sv, n_ctx)

    return _outproj_ffn_call(_ssd_mixer_out, (ys_f, ys_r, xc, pz),
                             (row(jnp.repeat(ssd_d[0], HEAD_DIM)), row(ssd_norm_g[0])), o2, (h1,),
                             modsel(1), w_out1.astype(BF16), ffn_w1[1].astype(BF16),
                             ffn_w2[1].astype(BF16), row(ln_mix_g[1]), row(ln_mix_b[1]), row(ln_ffn_g[1]),
                             row(ln_ffn_b[1]), h_off=n_ctx_tiles, n_out_tiles=n_lat_tiles,
                             n_ctx_tiles=n_ctx_tiles, alpha=alpha, name="outproj_ffn_1")
```
